```python
import jax, jax.numpy as jnp
from jax import lax
import numpy as np

D_MODEL = 1024
BATCH = 2
SEQ = 16384
DEPTH = 4

FOX_HEADS = 8
FOX_HEAD_DIM = 64
FOX_WIDTH = FOX_HEADS * FOX_HEAD_DIM
RET_HEADS = 4
RET_HEAD_DIM = 128
RET_WIDTH = RET_HEADS * RET_HEAD_DIM
MIX_WIDTH = FOX_WIDTH + RET_WIDTH
IN_COLS = 3 * FOX_WIDTH + FOX_HEADS + 4 * RET_WIDTH
D_FF = 4 * D_MODEL
Q_BLOCK = 128
RET_CHUNK = 128
ROPE_BASE = 10000.0
LN_EPS = 1e-5
GN_EPS = 1e-5
DEEPNORM_ALPHA = (2 * DEPTH) ** 0.25
DEEPNORM_BETA = (8 * DEPTH) ** -0.25

kernel_name = "fox_retention_hybrid_deepnorm"

SPLITS = [FOX_WIDTH, 2 * FOX_WIDTH, 3 * FOX_WIDTH, 3 * FOX_WIDTH + FOX_HEADS,
          3 * FOX_WIDTH + FOX_HEADS + RET_WIDTH, 3 * FOX_WIDTH + FOX_HEADS + 2 * RET_WIDTH,
          3 * FOX_WIDTH + FOX_HEADS + 3 * RET_WIDTH]


def layernorm(x, g, b):
    xf = x.astype(jnp.float32)
    mu = jnp.mean(xf, axis=-1, keepdims=True)
    var = jnp.mean(jnp.square(xf - mu), axis=-1, keepdims=True)
    y = (xf - mu) * lax.rsqrt(var + LN_EPS)
    return (y * g + b).astype(x.dtype)


def rotary(x):
    s, d = x.shape[1], x.shape[-1]
    half = d // 2
    inv_freq = ROPE_BASE ** (-jnp.arange(half, dtype=jnp.float32) / half)
    ang = jnp.arange(s, dtype=jnp.float32)[:, None] * inv_freq[None, :]
    cos = jnp.cos(ang)[None, :, None, :]
    sin = jnp.sin(ang)[None, :, None, :]
    x1, x2 = x[..., :half].astype(jnp.float32), x[..., half:].astype(jnp.float32)
    return jnp.concatenate([x1 * cos - x2 * sin, x1 * sin + x2 * cos], axis=-1).astype(x.dtype)


def fox_attention(q, k, v, log_f):
    b, s, h, d = q.shape
    n_blocks = s // Q_BLOCK
    scale = d ** -0.5
    c = jnp.cumsum(log_f.astype(jnp.float32), axis=1).transpose(0, 2, 1)
    qb = q.reshape(b, n_blocks, Q_BLOCK, h, d).transpose(1, 0, 2, 3, 4)
    cb = c.reshape(b, h, n_blocks, Q_BLOCK).transpose(2, 0, 1, 3)
    key_pos = jnp.arange(s)

    def block(args):
        i, q_i, c_i = args
        scores = jnp.einsum('bqhd,bkhd->bhqk', q_i, k).astype(jnp.float32) * scale
        bias = c_i[..., :, None] - c[..., None, :]
        q_pos = i * Q_BLOCK + jnp.arange(Q_BLOCK)
        causal = key_pos[None, :] <= q_pos[:, None]
        logits = jnp.where(causal, scores + bias, -jnp.inf)
        p = jax.nn.softmax(logits, axis=-1)
        return jnp.einsum('bhqk,bkhd->bqhd', p.astype(v.dtype), v)

    out = lax.map(block, (jnp.arange(n_blocks), qb, cb))
    return out.transpose(1, 0, 2, 3, 4).reshape(b, s, h * d)


def retention_chunkwise(q, k, v, gamma):
    b, s, h, dk = q.shape
    dv = v.shape[-1]
    n_chunks = s // RET_CHUNK
    log_g = jnp.log(gamma)
    idx = jnp.arange(RET_CHUNK, dtype=jnp.float32)
    diff = idx[:, None] - idx[None, :]
    decay_mask = jnp.where(diff >= 0,
                           jnp.exp(log_g[:, None, None] * jnp.maximum(diff, 0.0)), 0.0)
    xi = jnp.exp(log_g[:, None] * (idx + 1.0))[..., None]
    zeta = jnp.exp(log_g[:, None] * (RET_CHUNK - 1.0 - idx))[..., None]
    chunk_decay = jnp.exp(log_g * RET_CHUNK)[:, None, None]

    def to_chunks(t):
        return t.reshape(b, n_chunks, RET_CHUNK, h, t.shape[-1]).transpose(1, 0, 3, 2, 4)

    def step(state, inp):
        q_i, k_i, v_i = inp
        inner = jnp.einsum('bhqd,bhkd->bhqk', q_i, k_i) * decay_mask
        o = (jnp.einsum('bhqk,bhkv->bhqv', inner, v_i)
             + jnp.einsum('bhqd,bhdv->bhqv', q_i * xi, state))
        state = chunk_decay * state + jnp.einsum('bhkd,bhkv->bhdv', k_i * zeta, v_i)
        return state, o

    state0 = jnp.zeros((b, h, dk, dv), jnp.float32)
    _, out = lax.scan(step, state0, (to_chunks(q), to_chunks(k), to_chunks(v)))
    return out.transpose(1, 0, 3, 2, 4).reshape(b, s, h, dv)


def head_groupnorm(x, g):
    xf = x.astype(jnp.float32)
    mu = jnp.mean(xf, axis=-1, keepdims=True)
    var = jnp.mean(jnp.square(xf - mu), axis=-1, keepdims=True)
    return ((xf - mu) * lax.rsqrt(var + GN_EPS) * g).astype(x.dtype)


def hybrid_layer(x, w_in, w_out, w_ff1, w_ff2, ln1_g, ln1_b, ln2_g, ln2_b, b_forget, ret_gn_g, gamma):
    b, s, _ = x.shape
    proj = jnp.einsum('bsd,dc->bsc', x, w_in)
    fq, fk, fv, f_logit, rq, rk, rv, rg = jnp.split(proj, SPLITS, axis=-1)

    fshape = (b, s, FOX_HEADS, FOX_HEAD_DIM)
    log_f = jax.nn.log_sigmoid((f_logit + b_forget).astype(jnp.float32))
    fox_out = fox_attention(fq.reshape(fshape), fk.reshape(fshape), fv.reshape(fshape), log_f)

    rshape = (b, s, RET_HEADS, RET_HEAD_DIM)
    q_r = rotary(rq.reshape(rshape))
    k_r = rotary(rk.reshape(rshape)) * (RET_HEAD_DIM ** -0.5)
    ret = retention_chunkwise(q_r, k_r, rv.reshape(rshape), gamma)
    ret = head_groupnorm(ret, ret_gn_g).reshape(b, s, RET_WIDTH).astype(x.dtype)
    ret_out = jax.nn.silu(rg) * ret

    mixed = jnp.concatenate([fox_out.astype(x.dtype), ret_out], axis=-1)
    mix = jnp.einsum('bsc,cd->bsd', mixed, w_out)
    x = layernorm(DEEPNORM_ALPHA * x + mix, ln1_g, ln1_b)

    hid = jnp.square(jax.nn.relu(jnp.einsum('bsd,df->bsf', x, w_ff1)))
    ff = jnp.einsum('bsf,fd->bsd', hid, w_ff2)
    return layernorm(DEEPNORM_ALPHA * x + ff, ln2_g, ln2_b)


def setup_inputs(seed: int = 0) -> dict:
    key = jax.random.key(seed)
    ks = jax.random.split(key, 12)
    f32 = jnp.float32
    x = jax.random.normal(ks[0], (BATCH, SEQ, D_MODEL), f32)
    w_in = jax.random.normal(ks[1], (DEPTH, D_MODEL, IN_COLS), f32) * D_MODEL ** -0.5
    col = jnp.arange(IN_COLS)
    is_value = (((col >= SPLITS[1]) & (col < SPLITS[2]))
                | ((col >= SPLITS[5]) & (col < SPLITS[6])))
    w_in = w_in * jnp.where(is_value, DEEPNORM_BETA, 1.0).astype(f32)
    w_out = jax.random.normal(ks[2], (DEPTH, MIX_WIDTH, D_MODEL), f32) * (MIX_WIDTH ** -0.5 * DEEPNORM_BETA)
    w_ff1 = jax.random.normal(ks[3], (DEPTH, D_MODEL, D_FF), f32) * (D_MODEL ** -0.5 * DEEPNORM_BETA)
    w_ff2 = jax.random.normal(ks[4], (DEPTH, D_FF, D_MODEL), f32) * (D_FF ** -0.5 * DEEPNORM_BETA)
    ln1_g = 1.0 + 0.02 * jax.random.normal(ks[5], (DEPTH, D_MODEL), f32)
    ln1_b = 0.02 * jax.random.normal(ks[6], (DEPTH, D_MODEL), f32)
    ln2_g = 1.0 + 0.02 * jax.random.normal(ks[7], (DEPTH, D_MODEL), f32)
    ln2_b = 0.02 * jax.random.normal(ks[8], (DEPTH, D_MODEL), f32)
    b_forget = jax.random.uniform(ks[9], (DEPTH, FOX_HEADS), f32, 1.0, 5.0)
    ret_gn_g = 1.0 + 0.02 * jax.random.normal(ks[10], (DEPTH, RET_HEADS, RET_HEAD_DIM), f32)
    return {"x": x, "w_in": w_in, "w_out": w_out, "w_ff1": w_ff1, "w_ff2": w_ff2,
            "ln1_g": ln1_g, "ln1_b": ln1_b, "ln2_g": ln2_g, "ln2_b": ln2_b,
            "b_forget": b_forget, "ret_gn_g": ret_gn_g}


def reference(x, w_in, w_out, w_ff1, w_ff2, ln1_g, ln1_b, ln2_g, ln2_b, b_forget, ret_gn_g):
    gamma = 1.0 - jnp.exp2(-5.0 - jnp.arange(RET_HEADS, dtype=jnp.float32))
    for layer in range(DEPTH):
        x = hybrid_layer(x, w_in[layer], w_out[layer], w_ff1[layer], w_ff2[layer],
                         ln1_g[layer], ln1_b[layer], ln2_g[layer], ln2_b[layer],
                         b_forget[layer], ret_gn_g[layer], gamma)
    return x
```

```python
import functools

import jax
import jax.numpy as jnp
from jax import lax
from jax.experimental import pallas as pl
from jax.experimental.pallas import tpu as pltpu

F32 = jnp.float32
BF16 = jnp.bfloat16

D_MODEL = 1024
FOX_HEADS = 8
FOX_HEAD_DIM = 64
FOX_WIDTH = FOX_HEADS * FOX_HEAD_DIM
RET_HEADS = 4
RET_HEAD_DIM = 128
RET_WIDTH = RET_HEADS * RET_HEAD_DIM
D_FF = 4 * D_MODEL
RET_CHUNK = 128
ROPE_BASE = 10000.0
LN_EPS = 1e-5
GN_EPS = 1e-5

LANES = 128
VMEM_LIMIT = 56 * 1024 * 1024

TOKEN_TILE = 512
ATTN_TILE = 512
RET_TILE = 512
FF_CHUNK = 1024

_OFF_FQ, _OFF_FK, _OFF_FV = 0, FOX_WIDTH, 2 * FOX_WIDTH
_OFF_RQ = 3 * FOX_WIDTH
_OFF_RK, _OFF_RV, _OFF_RG = _OFF_RQ + RET_WIDTH, _OFF_RQ + 2 * RET_WIDTH, _OFF_RQ + 3 * RET_WIDTH
_OFF_FL = _OFF_RQ + 4 * RET_WIDTH
PACKED_COLS = _OFF_FL + LANES


def _params(*semantics):
    return pltpu.CompilerParams(dimension_semantics=semantics, vmem_limit_bytes=VMEM_LIMIT)


def _inproj_kernel(x_ref, w_ref, bf_ref, cs_ref, sn_ref,
                   fq_ref, fk_ref, fv_ref, lf_ref, rq_ref, rk_ref, rv_ref, rg_ref):
    xb = x_ref[...].astype(BF16)

    def proj(lo, width):
        return jnp.dot(xb, w_ref[:, lo:lo + width], preferred_element_type=F32)

    fq_ref[...] = (proj(_OFF_FQ, FOX_WIDTH) * (FOX_HEAD_DIM ** -0.5)).astype(BF16)
    fk_ref[...] = proj(_OFF_FK, FOX_WIDTH).astype(BF16)
    fv_ref[...] = proj(_OFF_FV, FOX_WIDTH).astype(BF16)
    rv_ref[...] = proj(_OFF_RV, RET_WIDTH).astype(BF16)
    rg_ref[...] = proj(_OFF_RG, RET_WIDTH)

    z = proj(_OFF_FL, LANES) + bf_ref[...]
    lf_ref[...] = -(jnp.maximum(-z, 0.0) + jnp.log1p(jnp.exp(-jnp.abs(z))))

    cs = cs_ref[...]
    sn = sn_ref[...]

    def rotary(t, scale):
        for h in range(RET_HEADS):
            th = t[:, h * RET_HEAD_DIM:(h + 1) * RET_HEAD_DIM]
            r = th * cs + pltpu.roll(th, RET_HEAD_DIM // 2, 1) * sn
            if scale is not None:
                r = r * scale
            yield h, r.astype(BF16)

    for h, r in rotary(proj(_OFF_RQ, RET_WIDTH), None):
        rq_ref[:, h * RET_HEAD_DIM:(h + 1) * RET_HEAD_DIM] = r
    for h, r in rotary(proj(_OFF_RK, RET_WIDTH), RET_HEAD_DIM ** -0.5):
        rk_ref[:, h * RET_HEAD_DIM:(h + 1) * RET_HEAD_DIM] = r


def _inproj(x2, w_packed, bf_pad, cs, sn, seq):
    n = x2.shape[0]
    tm = TOKEN_TILE
    pos_blocks = seq // tm
    row = lambda i: (i, 0)
    const = lambda i: (0, 0)
    pos = lambda i: (i % pos_blocks, 0)
    wide = lambda w, dt: jax.ShapeDtypeStruct((n, w), dt)
    return pl.pallas_call(
        _inproj_kernel,
        grid=(n // tm,),
        in_specs=[pl.BlockSpec((tm, D_MODEL), row),
                  pl.BlockSpec((D_MODEL, PACKED_COLS), const),
                  pl.BlockSpec((1, LANES), const),
                  pl.BlockSpec((tm, LANES), pos),
                  pl.BlockSpec((tm, LANES), pos)],
        out_specs=[pl.BlockSpec((tm, FOX_WIDTH), row)] * 3
                  + [pl.BlockSpec((tm, LANES), row)]
                  + [pl.BlockSpec((tm, RET_WIDTH), row)] * 4,
        out_shape=[wide(FOX_WIDTH, BF16)] * 3 + [wide(LANES, F32)]
                  + [wide(RET_WIDTH, BF16)] * 3 + [wide(RET_WIDTH, F32)],
        compiler_params=_params("parallel"),
        name="in_proj",
    )(x2, w_packed, bf_pad, cs, sn)


def _split3(t):
    hi = t.astype(BF16)
    r1 = t - hi.astype(F32)
    mid = r1.astype(BF16)
    lo = (r1 - mid.astype(F32)).astype(BF16)
    return hi, mid, lo


def _dot3(pieces, m):
    return sum(jnp.dot(p, m, preferred_element_type=F32) for p in pieces)


def _cumsum_kernel(x_ref, o_ref):
    heads, rows, lanes = x_ref.shape[1:]
    j = lax.broadcasted_iota(jnp.int32, (lanes, lanes), 0)
    l = lax.broadcasted_iota(jnp.int32, (lanes, lanes), 1)
    upper = (j <= l).astype(BF16)
    ones = jnp.ones((lanes, lanes), BF16)
    r_out = lax.broadcasted_iota(jnp.int32, (rows, rows), 0)
    r_in = lax.broadcasted_iota(jnp.int32, (rows, rows), 1)
    strict = (r_in < r_out).astype(BF16)
    for h in range(heads):
        pieces = _split3(x_ref[0, h])
        within = _dot3(pieces, upper)
        totals = _dot3(pieces, ones)
        tp = _split3(totals)
        offs = sum(jnp.dot(strict, p, preferred_element_type=F32) for p in tp)
        o_ref[0, h] = within + offs


def _cumsum(lf):
    b, h, s = lf.shape
    rows = s // LANES
    x4 = lf.reshape(b, h, rows, LANES)
    spec = pl.BlockSpec((1, h, rows, LANES), lambda i: (i, 0, 0, 0))
    out = pl.pallas_call(
        _cumsum_kernel,
        grid=(b,),
        in_specs=[spec],
        out_specs=spec,
        out_shape=jax.ShapeDtypeStruct(x4.shape, F32),
        compiler_params=_params("parallel"),
        name="forget_cumsum",
    )(x4)
    return out.reshape(b, h, s)


def _fox_kernel(q_ref, k_ref, v_ref, c_ref, o_ref, qm_ref, m_ref, l_ref, acc_ref):
    t = ATTN_TILE
    qi = pl.program_id(2)
    lane = lax.broadcasted_iota(jnp.int32, (1, LANES), 1)
    first = lane < FOX_HEAD_DIM
    q = q_ref[0]
    qm_ref[0] = jnp.where(first, q, jnp.zeros_like(q))
    qm_ref[1] = jnp.where(first, jnp.zeros_like(q), q)
    m_ref[...] = jnp.full(m_ref.shape, -jnp.inf, F32)
    l_ref[...] = jnp.zeros(l_ref.shape, F32)
    acc_ref[...] = jnp.zeros(acc_ref.shape, F32)

    q_start = pl.multiple_of(qi * t, t)
    c_q = c_ref[0, 0, :, pl.ds(q_start, t)]
    c_base = jnp.min(c_q, axis=1, keepdims=True)

    def tile(ki, masked):
        start = pl.multiple_of(ki * t, t)
        k = k_ref[0, pl.ds(start, t), :]
        v = v_ref[0, pl.ds(start, t), :]
        bias = c_base - c_ref[0, 0, :, pl.ds(start, t)]
        if masked:
            row = lax.broadcasted_iota(jnp.int32, (t, t), 0)
            col = lax.broadcasted_iota(jnp.int32, (t, t), 1)
            keep = col <= row
        for a in range(2):
            s = lax.dot_general(qm_ref[a], k, (((1,), (1,)), ((), ())),
                                preferred_element_type=F32)
            s = s + bias[a:a + 1, :]
            if masked:
                s = jnp.where(keep, s, -jnp.inf)
            m_prev = m_ref[a]
            m_new = jnp.maximum(m_prev, jnp.max(s, axis=1, keepdims=True))
            alpha = jnp.exp(m_prev - m_new)
            p = jnp.exp(s - m_new)
            l_ref[a] = alpha * l_ref[a] + jnp.sum(p, axis=1, keepdims=True)
            acc_ref[a] = alpha * acc_ref[a] + jnp.dot(p.astype(BF16), v,
                                                      preferred_element_type=F32)
            m_ref[a] = m_new

    def body(ki, carry):
        tile(ki, False)
        return carry

    lax.fori_loop(0, qi, body, 0)
    tile(qi, True)

    out = jnp.where(first, acc_ref[0] / l_ref[0], acc_ref[1] / l_ref[1])
    o_ref[0] = out.astype(o_ref.dtype)


def _fox_attention(fq, fk, fv, c):
    b, s, _ = fq.shape
    t = ATTN_TILE
    pairs = FOX_HEADS // 2
    return pl.pallas_call(
        _fox_kernel,
        grid=(b, pairs, s // t),
        in_specs=[pl.BlockSpec((1, t, LANES), lambda bi, hp, qi: (bi, qi, hp)),
                  pl.BlockSpec((1, s, LANES), lambda bi, hp, qi: (bi, 0, hp)),
                  pl.BlockSpec((1, s, LANES), lambda bi, hp, qi: (bi, 0, hp)),
                  pl.BlockSpec((1, 1, 2, s), lambda bi, hp, qi: (bi, hp, 0, 0))],
        out_specs=pl.BlockSpec((1, t, LANES), lambda bi, hp, qi: (bi, qi, hp)),
        out_shape=jax.ShapeDtypeStruct((b, s, FOX_WIDTH), BF16),
        scratch_shapes=[pltpu.VMEM((2, t, LANES), BF16),
                        pltpu.VMEM((2, t, 1), F32),
                        pltpu.VMEM((2, t, 1), F32),
                        pltpu.VMEM((2, t, LANES), F32)],
        compiler_params=_params("parallel", "parallel", "arbitrary"),
        name="fox_attention",
    )(fq, fk, fv, c)


def _ret_kernel(q_ref, k_ref, v_ref, g_ref, dm_ref, xi_ref, zeta_ref, cd_ref, gn_ref,
                o_ref, state_ref):
    @pl.when(pl.program_id(1) == 0)
    def _():
        state_ref[...] = jnp.zeros(state_ref.shape, F32)

    c = RET_CHUNK
    for h in range(RET_HEADS):
        cols = slice(h * RET_HEAD_DIM, (h + 1) * RET_HEAD_DIM)
        st = state_ref[h]
        for n in range(RET_TILE // c):
            rows = slice(n * c, (n + 1) * c)
            q = q_ref[0, rows, cols]
            k = k_ref[0, rows, cols]
            v = v_ref[0, rows, cols]
            inner = lax.dot_general(q, k, (((1,), (1,)), ((), ())),
                                    preferred_element_type=F32) * dm_ref[h]
            o = (jnp.dot(inner.astype(BF16), v, preferred_element_type=F32)
                 + xi_ref[h] * jnp.dot(q, st.astype(BF16), preferred_element_type=F32))
            kz = (k.astype(F32) * zeta_ref[h]).astype(BF16)
            st = cd_ref[h] * st + lax.dot_general(kz, v, (((0,), (0,)), ((), ())),
                                                  preferred_element_type=F32)
            mu = jnp.mean(o, axis=-1, keepdims=True)
            d = o - mu
            var = jnp.mean(d * d, axis=-1, keepdims=True)
            y = d * lax.rsqrt(var + GN_EPS) * gn_ref[h:h + 1, :]
            g = g_ref[0, rows, cols]
            o_ref[0, rows, cols] = (g * jax.nn.sigmoid(g) * y).astype(o_ref.dtype)
        state_ref[h] = st


def _retention(rq, rk, rv, rg, dm, xi, zeta, cd, gn):
    b, s, _ = rq.shape
    t = RET_TILE
    tok = pl.BlockSpec((1, t, RET_WIDTH), lambda bi, si: (bi, si, 0))
    tab = pl.BlockSpec((RET_HEADS, RET_CHUNK, RET_HEAD_DIM), lambda bi, si: (0, 0, 0))
    return pl.pallas_call(
        _ret_kernel,
        grid=(b, s // t),
        in_specs=[tok, tok, tok, tok, tab, tab, tab, tab,
                  pl.BlockSpec((RET_HEADS, RET_HEAD_DIM), lambda bi, si: (0, 0))],
        out_specs=tok,
        out_shape=jax.ShapeDtypeStruct((b, s, RET_WIDTH), BF16),
        scratch_shapes=[pltpu.VMEM((RET_HEADS, RET_HEAD_DIM, RET_HEAD_DIM), F32)],
        compiler_params=_params("parallel", "arbitrary"),
        name="retention",
    )(rq, rk, rv, rg, dm, xi, zeta, cd, gn)


def _layernorm(y, g, b):
    mu = jnp.mean(y, axis=-1, keepdims=True)
    d = y - mu
    var = jnp.mean(d * d, axis=-1, keepdims=True)
    return d * lax.rsqrt(var + LN_EPS) * g + b


def _outproj_kernel(alpha, fox_ref, ret_ref, x_ref, wa_ref, wb_ref, g_ref, b_ref, o_ref):
    mix = (jnp.dot(fox_ref[...], wa_ref[...], preferred_element_type=F32)
           + jnp.dot(ret_ref[...], wb_ref[...], preferred_element_type=F32))
    o_ref[...] = _layernorm(alpha * x_ref[...] + mix, g_ref[...], b_ref[...])


def _outproj(alpha, fox, ret, x2, wa, wb, g, b):
    n = x2.shape[0]
    tm = TOKEN_TILE
    row = lambda i: (i, 0)
    const = lambda i: (0, 0)
    return pl.pallas_call(
        functools.partial(_outproj_kernel, alpha),
        grid=(n // tm,),
        in_specs=[pl.BlockSpec((tm, FOX_WIDTH), row),
                  pl.BlockSpec((tm, RET_WIDTH), row),
                  pl.BlockSpec((tm, D_MODEL), row),
                  pl.BlockSpec((FOX_WIDTH, D_MODEL), const),
                  pl.BlockSpec((RET_WIDTH, D_MODEL), const),
                  pl.BlockSpec((1, D_MODEL), const),
                  pl.BlockSpec((1, D_MODEL), const)],
        out_specs=pl.BlockSpec((tm, D_MODEL), row),
        out_shape=jax.ShapeDtypeStruct((n, D_MODEL), F32),
        compiler_params=_params("parallel"),
        name="out_proj_ln",
    )(fox, ret, x2, wa, wb, g, b)


def _ffn_kernel(alpha, x_ref, w1_ref, w2_ref, g_ref, b_ref, o_ref):
    x = x_ref[...]
    xb = x.astype(BF16)
    acc = alpha * x
    for c in range(D_FF // FF_CHUNK):
        cols = slice(c * FF_CHUNK, (c + 1) * FF_CHUNK)
        h = jnp.maximum(jnp.dot(xb, w1_ref[:, cols], preferred_element_type=F32), 0.0)
        acc = acc + jnp.dot((h * h).astype(BF16), w2_ref[cols, :], preferred_element_type=F32)
    o_ref[...] = _layernorm(acc, g_ref[...], b_ref[...])


def _ffn(alpha, x2, w1, w2, g, b):
    n = x2.shape[0]
    tm = TOKEN_TILE
    row = lambda i: (i, 0)
    const = lambda i: (0, 0)
    return pl.pallas_call(
        functools.partial(_ffn_kernel, alpha),
        grid=(n // tm,),
        in_specs=[pl.BlockSpec((tm, D_MODEL), row),
                  pl.BlockSpec((D_MODEL, D_FF), const),
                  pl.BlockSpec((D_FF, D_MODEL), const),
                  pl.BlockSpec((1, D_MODEL), const),
                  pl.BlockSpec((1, D_MODEL), const)],
        out_specs=pl.BlockSpec((tm, D_MODEL), row),
        out_shape=jax.ShapeDtypeStruct((n, D_MODEL), F32),
        compiler_params=_params("parallel"),
        name="ffn_ln",
    )(x2, w1, w2, g, b)


def _rotary_tables(seq):
    half = RET_HEAD_DIM // 2
    inv_freq = ROPE_BASE ** (-jnp.arange(half, dtype=F32) / half)
    ang = jnp.arange(seq, dtype=F32)[:, None] * inv_freq[None, :]
    cos, sin = jnp.cos(ang), jnp.sin(ang)
    return jnp.concatenate([cos, cos], axis=-1), jnp.concatenate([-sin, sin], axis=-1)


def _decay_tables():
    gamma = 1.0 - jnp.exp2(-5.0 - jnp.arange(RET_HEADS, dtype=F32))
    log_g = jnp.log(gamma)
    idx = jnp.arange(RET_CHUNK, dtype=F32)
    diff = idx[:, None] - idx[None, :]
    dm = jnp.where(diff >= 0, jnp.exp(log_g[:, None, None] * jnp.maximum(diff, 0.0)), 0.0)
    shape = (RET_HEADS, RET_CHUNK, RET_HEAD_DIM)
    xi = jnp.broadcast_to(jnp.exp(log_g[:, None] * (idx + 1.0))[..., None], shape)
    zeta = jnp.broadcast_to(jnp.exp(log_g[:, None] * (RET_CHUNK - 1.0 - idx))[..., None], shape)
    cd = jnp.broadcast_to(jnp.exp(log_g * RET_CHUNK)[:, None, None], shape)
    return dm, xi, zeta, cd


def _pack_w_in(w):
    lo = 3 * FOX_WIDTH
    logits = w[:, lo:lo + FOX_HEADS]
    pad = jnp.zeros((w.shape[0], LANES - FOX_HEADS), w.dtype)
    return jnp.concatenate([w[:, :lo], w[:, lo + FOX_HEADS:], logits, pad], axis=1).astype(BF16)


@jax.jit
def kernel(x, w_in, w_out, w_ff1, w_ff2, ln1_g, ln1_b, ln2_g, ln2_b, b_forget, ret_gn_g):
    batch, seq, _ = x.shape
    depth = w_in.shape[0]
    alpha = (2 * depth) ** 0.25
    cs, sn = _rotary_tables(seq)
    dm, xi, zeta, cd = _decay_tables()
    x2 = x.reshape(batch * seq, D_MODEL)
    for layer in range(depth):
        bf_pad = jnp.pad(b_forget[layer], (0, LANES - FOX_HEADS)).reshape(1, LANES)
        fq, fk, fv, lf, rq, rk, rv, rg = _inproj(x2, _pack_w_in(w_in[layer]), bf_pad, cs, sn, seq)
        seq3 = lambda t: t.reshape(batch, seq, t.shape[-1])
        lf = seq3(lf)[:, :, :FOX_HEADS].transpose(0, 2, 1)
        c = _cumsum(lf).reshape(batch, FOX_HEADS // 2, 2, seq)
        fox = _fox_attention(seq3(fq), seq3(fk), seq3(fv), c)
        ret = _retention(seq3(rq), seq3(rk), seq3(rv), seq3(rg), dm, xi, zeta, cd, ret_gn_g[layer])
        wo = w_out[layer].astype(BF16)
        x2 = _outproj(alpha, fox.reshape(batch * seq, FOX_WIDTH), ret.reshape(batch * seq, RET_WIDTH),
                      x2, wo[:FOX_WIDTH], wo[FOX_WIDTH:],
                      ln1_g[layer].reshape(1, D_MODEL), ln1_b[layer].reshape(1, D_MODEL))
        x2 = _ffn(alpha, x2, w_ff1[layer].astype(BF16), w_ff2[layer].astype(BF16),
                  ln2_g[layer].reshape(1, D_MODEL), ln2_b[layer].reshape(1, D_MODEL))
    return x2.reshape(batch, seq, D_MODEL)
```

```python
import functools
import math

import jax
import jax.numpy as jnp
from jax import lax
from jax.experimental import pallas as pl
from jax.experimental.pallas import tpu as pltpu

F32 = jnp.float32
BF16 = jnp.bfloat16

D_MODEL = 1024
FOX_HEADS = 8
FOX_HEAD_DIM = 64
FOX_WIDTH = FOX_HEADS * FOX_HEAD_DIM
RET_HEADS = 4
RET_HEAD_DIM = 128
RET_WIDTH = RET_HEADS * RET_HEAD_DIM
D_FF = 4 * D_MODEL
RET_CHUNK = 128
ROPE_BASE = 10000.0
LN_EPS = 1e-5
GN_EPS = 1e-5
LOG2E = math.log2(math.e)

LANES = 128
MXU_DEPTH = 256
VMEM_LIMIT = 56 * 1024 * 1024

TOKEN_TILE = 512
ATTN_TILE = 512
RET_TILE = 512
FF_CHUNK = 1024

FOX_PAIRS = FOX_HEADS // 2
V_ROWS = LANES
BIAS_PIECES = 3

_OFF_FK = 0
_OFF_RQ = FOX_WIDTH
_OFF_RK, _OFF_RV, _OFF_RG = _OFF_RQ + RET_WIDTH, _OFF_RQ + 2 * RET_WIDTH, _OFF_RQ + 3 * RET_WIDTH
_OFF_FL = _OFF_RQ + 4 * RET_WIDTH
PACKED_COLS = _OFF_FL + LANES
_ROW_FV = FOX_WIDTH
PACKED_ROWS = FOX_WIDTH + FOX_HEADS * V_ROWS

_NT = (((1,), (1,)), ((), ()))


def _params(*semantics):
    return pltpu.CompilerParams(dimension_semantics=semantics, vmem_limit_bytes=VMEM_LIMIT)


def _inproj_kernel(x_ref, w_ref, wt_ref, bf_ref, cs_ref, sn_ref,
                   fqt_ref, fk_ref, fvt_ref, lf_ref, rq_ref, rk_ref, rv_ref, rg_ref):
    xb = x_ref[...].astype(BF16)

    def proj(lo, width):
        return jnp.dot(xb, w_ref[:, lo:lo + width], preferred_element_type=F32)

    def proj_t(lo, rows):
        return lax.dot_general(wt_ref[lo:lo + rows, :], xb, _NT, preferred_element_type=F32)

    fqt_ref[...] = (proj_t(0, FOX_WIDTH) * (FOX_HEAD_DIM ** -0.5 * LOG2E)).astype(BF16)
    fk_ref[...] = proj(_OFF_FK, FOX_WIDTH).astype(BF16)
    vt = proj_t(_ROW_FV, FOX_HEADS * V_ROWS)
    row = lax.broadcasted_iota(jnp.int32, vt.shape, 0)
    fvt_ref[...] = jnp.where(row % V_ROWS == FOX_HEAD_DIM, 1.0, vt).astype(BF16)
    rv_ref[...] = proj(_OFF_RV, RET_WIDTH).astype(BF16)
    rg_ref[...] = proj(_OFF_RG, RET_WIDTH)

    z = proj(_OFF_FL, LANES) + bf_ref[...]
    lf_ref[...] = -(jnp.maximum(-z, 0.0) + jnp.log1p(jnp.exp(-jnp.abs(z))))

    cs = cs_ref[...]
    sn = sn_ref[...]

    def rotary(t, scale):
        for h in range(RET_HEADS):
            th = t[:, h * RET_HEAD_DIM:(h + 1) * RET_HEAD_DIM]
            r = th * cs + pltpu.roll(th, RET_HEAD_DIM // 2, 1) * sn
            if scale is not None:
                r = r * scale
            yield h, r.astype(BF16)

    for h, r in rotary(proj(_OFF_RQ, RET_WIDTH), None):
        rq_ref[:, h * RET_HEAD_DIM:(h + 1) * RET_HEAD_DIM] = r
    for h, r in rotary(proj(_OFF_RK, RET_WIDTH), RET_HEAD_DIM ** -0.5):
        rk_ref[:, h * RET_HEAD_DIM:(h + 1) * RET_HEAD_DIM] = r


def _inproj(x2, w_packed, wt_packed, bf_pad, cs, sn, seq):
    n = x2.shape[0]
    tm = TOKEN_TILE
    pos_blocks = seq // tm
    row = lambda i: (i, 0)
    col = lambda i: (0, i)
    const = lambda i: (0, 0)
    pos = lambda i: (i % pos_blocks, 0)
    wide = lambda w, dt: jax.ShapeDtypeStruct((n, w), dt)
    tall = lambda r: jax.ShapeDtypeStruct((r, n), BF16)
    return pl.pallas_call(
        _inproj_kernel,
        grid=(n // tm,),
        in_specs=[pl.BlockSpec((tm, D_MODEL), row),
                  pl.BlockSpec((D_MODEL, PACKED_COLS), const),
                  pl.BlockSpec((PACKED_ROWS, D_MODEL), const),
                  pl.BlockSpec((1, LANES), const),
                  pl.BlockSpec((tm, LANES), pos),
                  pl.BlockSpec((tm, LANES), pos)],
        out_specs=[pl.BlockSpec((FOX_WIDTH, tm), col),
                   pl.BlockSpec((tm, FOX_WIDTH), row),
                   pl.BlockSpec((FOX_HEADS * V_ROWS, tm), col),
                   pl.BlockSpec((tm, LANES), row)]
                  + [pl.BlockSpec((tm, RET_WIDTH), row)] * 4,
        out_shape=[tall(FOX_WIDTH), wide(FOX_WIDTH, BF16), tall(FOX_HEADS * V_ROWS), wide(LANES, F32)]
                  + [wide(RET_WIDTH, BF16)] * 3 + [wide(RET_WIDTH, F32)],
        compiler_params=_params("parallel"),
        name="in_proj",
    )(x2, w_packed, wt_packed, bf_pad, cs, sn)


def _split3(t):
    hi = t.astype(BF16)
    r1 = t - hi.astype(F32)
    mid = r1.astype(BF16)
    lo = (r1 - mid.astype(F32)).astype(BF16)
    return hi, mid, lo


def _dot3(pieces, m):
    return sum(jnp.dot(p, m, preferred_element_type=F32) for p in pieces)


def _cumsum_kernel(x_ref, o_ref):
    heads, rows, lanes = x_ref.shape[1:]
    j = lax.broadcasted_iota(jnp.int32, (lanes, lanes), 0)
    l = lax.broadcasted_iota(jnp.int32, (lanes, lanes), 1)
    upper = (j <= l).astype(BF16)
    ones = jnp.ones((lanes, lanes), BF16)
    r_out = lax.broadcasted_iota(jnp.int32, (rows, rows), 0)
    r_in = lax.broadcasted_iota(jnp.int32, (rows, rows), 1)
    strict = (r_in < r_out).astype(BF16)
    for h in range(heads):
        pieces = _split3(x_ref[0, h] * (-LOG2E))
        within = _dot3(pieces, upper)
        totals = _dot3(pieces, ones)
        offs = sum(jnp.dot(strict, p, preferred_element_type=F32) for p in _split3(totals))
        for t, piece in enumerate(_split3(within + offs)):
            o_ref[0, t, h] = piece


def _neg_cumsum_pieces(lf):
    b, h, s = lf.shape
    rows = s // LANES
    return pl.pallas_call(
        _cumsum_kernel,
        grid=(b,),
        in_specs=[pl.BlockSpec((1, h, rows, LANES), lambda i: (i, 0, 0, 0))],
        out_specs=pl.BlockSpec((1, BIAS_PIECES, h, rows, LANES), lambda i: (i, 0, 0, 0, 0)),
        out_shape=jax.ShapeDtypeStruct((b, BIAS_PIECES, h, rows, LANES), BF16),
        compiler_params=_params("parallel"),
        name="forget_cumsum",
    )(lf.reshape(b, h, rows, LANES)).reshape(b, BIAS_PIECES, h, s)


def _fox_kernel(qt_ref, k_ref, cb_ref, vt_ref, o_ref, qaug_ref, m_ref, acc_ref):
    t = ATTN_TILE
    qi = pl.program_id(2)
    qt = qt_ref[...]
    sub = lax.broadcasted_iota(jnp.int32, (LANES, t), 0)
    for a in range(2):
        mine = (sub >= a * FOX_HEAD_DIM) & (sub < (a + 1) * FOX_HEAD_DIM)
        qaug_ref[a, 0:LANES, :] = jnp.where(mine, qt, jnp.zeros_like(qt))
        pick = (sub >= a * BIAS_PIECES) & (sub < (a + 1) * BIAS_PIECES)
        qaug_ref[a, LANES:MXU_DEPTH, :] = jnp.where(pick, 1.0, 0.0).astype(BF16)
    m_ref[...] = jnp.full(m_ref.shape, -jnp.inf, F32)
    acc_ref[...] = jnp.zeros(acc_ref.shape, F32)

    def tile(ki, masked):
        start = pl.multiple_of(ki * t, t)
        kaug = jnp.concatenate([k_ref[0, pl.ds(start, t), :], cb_ref[0, 0, pl.ds(start, t), :]], axis=1)
        if masked:
            key = lax.broadcasted_iota(jnp.int32, (t, t), 0)
            qry = lax.broadcasted_iota(jnp.int32, (t, t), 1)
            keep = key <= qry
        for a in range(2):
            s = jnp.dot(kaug, qaug_ref[a], preferred_element_type=F32)
            if masked:
                s = jnp.where(keep, s, -jnp.inf)
            m_prev = m_ref[a]
            m_new = jnp.maximum(m_prev, jnp.max(s, axis=0, keepdims=True))
            p = jnp.exp2(s - m_new).astype(BF16)
            alpha = jnp.exp2(m_prev - m_new)
            vt = vt_ref[a * V_ROWS:(a + 1) * V_ROWS, pl.ds(start, t)]
            acc_ref[a] = alpha * acc_ref[a] + jnp.dot(vt, p, preferred_element_type=F32)
            m_ref[a] = m_new

    def body(ki, carry):
        tile(ki, False)
        return carry

    lax.fori_loop(0, qi, body, 0)
    tile(qi, True)

    outs = [acc_ref[a, 0:FOX_HEAD_DIM, :] / acc_ref[a, FOX_HEAD_DIM:FOX_HEAD_DIM + 1, :] for a in range(2)]
    o_ref[0] = jnp.concatenate(outs, axis=0).T.astype(o_ref.dtype)


def _fox_attention(fqt, fk, cb, fvt, batch, seq):
    t = ATTN_TILE
    nq = seq // t
    return pl.pallas_call(
        _fox_kernel,
        grid=(batch, FOX_PAIRS, nq),
        in_specs=[pl.BlockSpec((LANES, t), lambda bi, hp, qi: (hp, bi * nq + qi)),
                  pl.BlockSpec((1, seq, LANES), lambda bi, hp, qi: (bi, 0, hp)),
                  pl.BlockSpec((1, 1, seq, LANES), lambda bi, hp, qi: (bi, hp, 0, 0)),
                  pl.BlockSpec((2 * V_ROWS, seq), lambda bi, hp, qi: (hp, bi))],
        out_specs=pl.BlockSpec((1, t, LANES), lambda bi, hp, qi: (bi, qi, hp)),
        out_shape=jax.ShapeDtypeStruct((batch, seq, FOX_WIDTH), BF16),
        scratch_shapes=[pltpu.VMEM((2, MXU_DEPTH, t), BF16),
                        pltpu.VMEM((2, 1, t), F32),
                        pltpu.VMEM((2, V_ROWS, t), F32)],
        compiler_params=_params("parallel", "parallel", "arbitrary"),
        name="fox_attention",
    )(fqt, fk, cb, fvt)


def _ret_kernel(q_ref, k_ref, v_ref, g_ref, dm_ref, xi_ref, zeta_ref, cd_ref, gn_ref,
                o_ref, state_ref):
    @pl.when(pl.program_id(1) == 0)
    def _():
        state_ref[...] = jnp.zeros(state_ref.shape, F32)

    c = RET_CHUNK
    for h in range(RET_HEADS):
        cols = slice(h * RET_HEAD_DIM, (h + 1) * RET_HEAD_DIM)
        st = state_ref[h]
        for n in range(RET_TILE // c):
            rows = slice(n * c, (n + 1) * c)
            q = q_ref[0, rows, cols]
            k = k_ref[0, rows, cols]
            v = v_ref[0, rows, cols]
            inner = lax.dot_general(q, k, _NT, preferred_element_type=F32) * dm_ref[h]
            o = (jnp.dot(inner.astype(BF16), v, preferred_element_type=F32)
                 + xi_ref[h] * jnp.dot(q, st.astype(BF16), preferred_element_type=F32))
            kz = (k.astype(F32) * zeta_ref[h]).astype(BF16)
            st = cd_ref[h] * st + lax.dot_general(kz, v, (((0,), (0,)), ((), ())),
                                                  preferred_element_type=F32)
            mu = jnp.mean(o, axis=-1, keepdims=True)
            d = o - mu
            var = jnp.mean(d * d, axis=-1, keepdims=True)
            y = d * lax.rsqrt(var + GN_EPS) * gn_ref[h:h + 1, :]
            g = g_ref[0, rows, cols]
            o_ref[0, rows, cols] = (g * jax.nn.sigmoid(g) * y).astype(o_ref.dtype)
        state_ref[h] = st


def _retention(rq, rk, rv, rg, dm, xi, zeta, cd, gn):
    b, s, _ = rq.shape
    t = RET_TILE
    tok = pl.BlockSpec((1, t, RET_WIDTH), lambda bi, si: (bi, si, 0))
    tab = pl.BlockSpec((RET_HEADS, RET_CHUNK, RET_HEAD_DIM), lambda bi, si: (0, 0, 0))
    return pl.pallas_call(
        _ret_kernel,
        grid=(b, s // t),
        in_specs=[tok, tok, tok, tok, tab, tab, tab, tab,
                  pl.BlockSpec((RET_HEADS, RET_HEAD_DIM), lambda bi, si: (0, 0))],
        out_specs=tok,
        out_shape=jax.ShapeDtypeStruct((b, s, RET_WIDTH), BF16),
        scratch_shapes=[pltpu.VMEM((RET_HEADS, RET_HEAD_DIM, RET_HEAD_DIM), F32)],
        compiler_params=_params("parallel", "arbitrary"),
        name="retention",
    )(rq, rk, rv, rg, dm, xi, zeta, cd, gn)


def _layernorm(y, g, b):
    mu = jnp.mean(y, axis=-1, keepdims=True)
    d = y - mu
    var = jnp.mean(d * d, axis=-1, keepdims=True)
    return d * lax.rsqrt(var + LN_EPS) * g + b


def _outproj_kernel(alpha, fox_ref, ret_ref, x_ref, wa_ref, wb_ref, g_ref, b_ref, o_ref):
    mix = (jnp.dot(fox_ref[...], wa_ref[...], preferred_element_type=F32)
           + jnp.dot(ret_ref[...], wb_ref[...], preferred_element_type=F32))
    o_ref[...] = _layernorm(alpha * x_ref[...] + mix, g_ref[...], b_ref[...])


def _outproj(alpha, fox, ret, x2, wa, wb, g, b):
    n = x2.shape[0]
    tm = TOKEN_TILE
    row = lambda i: (i, 0)
    const = lambda i: (0, 0)
    return pl.pallas_call(
        functools.partial(_outproj_kernel, alpha),
        grid=(n // tm,),
        in_specs=[pl.BlockSpec((tm, FOX_WIDTH), row),
                  pl.BlockSpec((tm, RET_WIDTH), row),
                  pl.BlockSpec((tm, D_MODEL), row),
                  pl.BlockSpec((FOX_WIDTH, D_MODEL), const),
                  pl.BlockSpec((RET_WIDTH, D_MODEL), const),
                  pl.BlockSpec((1, D_MODEL), const),
                  pl.BlockSpec((1, D_MODEL), const)],
        out_specs=pl.BlockSpec((tm, D_MODEL), row),
        out_shape=jax.ShapeDtypeStruct((n, D_MODEL), F32),
        compiler_params=_params("parallel"),
        name="out_proj_ln",
    )(fox, ret, x2, wa, wb, g, b)


def _ffn_kernel(alpha, x_ref, w1_ref, w2_ref, g_ref, b_ref, o_ref):
    x = x_ref[...]
    xb = x.astype(BF16)
    acc = alpha * x
    for c in range(D_FF // FF_CHUNK):
        cols = slice(c * FF_CHUNK, (c + 1) * FF_CHUNK)
        h = jnp.maximum(jnp.dot(xb, w1_ref[:, cols], preferred_element_type=F32), 0.0)
        acc = acc + jnp.dot((h * h).astype(BF16), w2_ref[cols, :], preferred_element_type=F32)
    o_ref[...] = _layernorm(acc, g_ref[...], b_ref[...])


def _ffn(alpha, x2, w1, w2, g, b):
    n = x2.shape[0]
    tm = TOKEN_TILE
    row = lambda i: (i, 0)
    const = lambda i: (0, 0)
    return pl.pallas_call(
        functools.partial(_ffn_kernel, alpha),
        grid=(n // tm,),
        in_specs=[pl.BlockSpec((tm, D_MODEL), row),
                  pl.BlockSpec((D_MODEL, D_FF), const),
                  pl.BlockSpec((D_FF, D_MODEL), const),
                  pl.BlockSpec((1, D_MODEL), const),
                  pl.BlockSpec((1, D_MODEL), const)],
        out_specs=pl.BlockSpec((tm, D_MODEL), row),
        out_shape=jax.ShapeDtypeStruct((n, D_MODEL), F32),
        compiler_params=_params("parallel"),
        name="ffn_ln",
    )(x2, w1, w2, g, b)


def _rotary_tables(seq):
    half = RET_HEAD_DIM // 2
    inv_freq = ROPE_BASE ** (-jnp.arange(half, dtype=F32) / half)
    ang = jnp.arange(seq, dtype=F32)[:, None] * inv_freq[None, :]
    cos, sin = jnp.cos(ang), jnp.sin(ang)
    return jnp.concatenate([cos, cos], axis=-1), jnp.concatenate([-sin, sin], axis=-1)


def _decay_tables():
    gamma = 1.0 - jnp.exp2(-5.0 - jnp.arange(RET_HEADS, dtype=F32))
    log_g = jnp.log(gamma)
    idx = jnp.arange(RET_CHUNK, dtype=F32)
    diff = idx[:, None] - idx[None, :]
    dm = jnp.where(diff >= 0, jnp.exp(log_g[:, None, None] * jnp.maximum(diff, 0.0)), 0.0)
    shape = (RET_HEADS, RET_CHUNK, RET_HEAD_DIM)
    xi = jnp.broadcast_to(jnp.exp(log_g[:, None] * (idx + 1.0))[..., None], shape)
    zeta = jnp.broadcast_to(jnp.exp(log_g[:, None] * (RET_CHUNK - 1.0 - idx))[..., None], shape)
    cd = jnp.broadcast_to(jnp.exp(log_g * RET_CHUNK)[:, None, None], shape)
    return dm, xi, zeta, cd


def _pack_w_in(w):
    d = w.shape[0]
    wq, wk, wv = (w[:, i * FOX_WIDTH:(i + 1) * FOX_WIDTH] for i in range(3))
    lo = 3 * FOX_WIDTH
    logits = w[:, lo:lo + FOX_HEADS]
    pad = jnp.zeros((d, LANES - FOX_HEADS), w.dtype)
    packed = jnp.concatenate([wk, w[:, lo + FOX_HEADS:], logits, pad], axis=1).astype(BF16)
    wv_t = wv.T.reshape(FOX_HEADS, FOX_HEAD_DIM, d)
    wv_t = jnp.pad(wv_t, ((0, 0), (0, V_ROWS - FOX_HEAD_DIM), (0, 0))).reshape(FOX_HEADS * V_ROWS, d)
    packed_t = jnp.concatenate([wq.T, wv_t], axis=0).astype(BF16)
    return packed, packed_t


def _bias_operand(pieces, batch, seq):
    p = pieces.reshape(batch, BIAS_PIECES, FOX_PAIRS, 2, seq)
    p = p.transpose(0, 2, 4, 3, 1).reshape(batch, FOX_PAIRS, seq, 2 * BIAS_PIECES)
    return jnp.pad(p, ((0, 0), (0, 0), (0, 0), (0, LANES - 2 * BIAS_PIECES)))


@jax.jit
def kernel(x, w_in, w_out, w_ff1, w_ff2, ln1_g, ln1_b, ln2_g, ln2_b, b_forget, ret_gn_g):
    batch, seq, _ = x.shape
    depth = w_in.shape[0]
    alpha = (2 * depth) ** 0.25
    cs, sn = _rotary_tables(seq)
    dm, xi, zeta, cd = _decay_tables()
    x2 = x.reshape(batch * seq, D_MODEL)
    for layer in range(depth):
        bf_pad = jnp.pad(b_forget[layer], (0, LANES - FOX_HEADS)).reshape(1, LANES)
        w_packed, wt_packed = _pack_w_in(w_in[layer])
        fqt, fk, fvt, lf, rq, rk, rv, rg = _inproj(x2, w_packed, wt_packed, bf_pad, cs, sn, seq)
        seq3 = lambda t: t.reshape(batch, seq, t.shape[-1])
        lf = seq3(lf)[:, :, :FOX_HEADS].transpose(0, 2, 1)
        cb = _bias_operand(_neg_cumsum_pieces(lf), batch, seq)
        fox = _fox_attention(fqt, seq3(fk), cb, fvt, batch, seq)
        ret = _retention(seq3(rq), seq3(rk), seq3(rv), seq3(rg), dm, xi, zeta, cd, ret_gn_g[layer])
        wo = w_out[layer].astype(BF16)
        x2 = _outproj(alpha, fox.reshape(batch * seq, FOX_WIDTH), ret.reshape(batch * seq, RET_WIDTH),
                      x2, wo[:FOX_WIDTH], wo[FOX_WIDTH:],
                      ln1_g[layer].reshape(1, D_MODEL), ln1_b[layer].reshape(1, D_MODEL))
        x2 = _ffn(alpha, x2, w_ff1[layer].astype(BF16), w_ff2[layer].astype(BF16),
                  ln2_g[layer].reshape(1, D_MODEL), ln2_b[layer].reshape(1, D_MODEL))
    return x2.reshape(batch, seq, D_MODEL)
```

```python
import functools
import math

import jax
import jax.numpy as jnp
from jax import lax
from jax.experimental import pallas as pl
from jax.experimental.pallas import tpu as pltpu

F32 = jnp.float32
BF16 = jnp.bfloat16

D_MODEL = 1024
FOX_HEADS = 8
FOX_HEAD_DIM = 64
FOX_WIDTH = FOX_HEADS * FOX_HEAD_DIM
RET_HEADS = 4
RET_HEAD_DIM = 128
RET_WIDTH = RET_HEADS * RET_HEAD_DIM
D_FF = 4 * D_MODEL
RET_CHUNK = 128
ROPE_BASE = 10000.0
LN_EPS = 1e-5
GN_EPS = 1e-5
LOG2E = math.log2(math.e)

LANES = 128
MXU_DEPTH = 256
VMEM_LIMIT = 56 * 1024 * 1024

TOKEN_TILE = 512
ATTN_TILE = 512
ATTN_UNROLL = 4
RET_TILE = 512
FF_CHUNK = 1024

FOX_PAIRS = FOX_HEADS // 2
V_ROWS = LANES
BIAS_PIECES = 3

_OFF_FK = 0
_OFF_RQ = FOX_WIDTH
_OFF_RK, _OFF_RV, _OFF_RG = _OFF_RQ + RET_WIDTH, _OFF_RQ + 2 * RET_WIDTH, _OFF_RQ + 3 * RET_WIDTH
_OFF_FL = _OFF_RQ + 4 * RET_WIDTH
PACKED_COLS = _OFF_FL + LANES
_ROW_FV = FOX_WIDTH
PACKED_ROWS = FOX_WIDTH + FOX_HEADS * V_ROWS

_NT = (((1,), (1,)), ((), ()))


def _params(*semantics):
    return pltpu.CompilerParams(dimension_semantics=semantics, vmem_limit_bytes=VMEM_LIMIT)


def _inproj_kernel(x_ref, w_ref, wt_ref, bf_ref, cs_ref, sn_ref,
                   fqt_ref, fk_ref, fvt_ref, lf_ref, rq_ref, rk_ref, rv_ref, rg_ref):
    xb = x_ref[...].astype(BF16)

    def proj(lo, width):
        return jnp.dot(xb, w_ref[:, lo:lo + width], preferred_element_type=F32)

    def proj_t(lo, rows):
        return lax.dot_general(wt_ref[lo:lo + rows, :], xb, _NT, preferred_element_type=F32)

    fqt_ref[...] = (proj_t(0, FOX_WIDTH) * (FOX_HEAD_DIM ** -0.5 * LOG2E)).astype(BF16)
    fk_ref[...] = proj(_OFF_FK, FOX_WIDTH).astype(BF16)
    vt = proj_t(_ROW_FV, FOX_HEADS * V_ROWS)
    row = lax.broadcasted_iota(jnp.int32, vt.shape, 0)
    fvt_ref[...] = jnp.where(row % V_ROWS == FOX_HEAD_DIM, 1.0, vt).astype(BF16)
    rv_ref[...] = proj(_OFF_RV, RET_WIDTH).astype(BF16)
    rg_ref[...] = proj(_OFF_RG, RET_WIDTH)

    z = proj(_OFF_FL, LANES) + bf_ref[...]
    lf_ref[...] = -(jnp.maximum(-z, 0.0) + jnp.log1p(jnp.exp(-jnp.abs(z))))

    cs = cs_ref[...]
    sn = sn_ref[...]

    def rotary(t, scale):
        for h in range(RET_HEADS):
            th = t[:, h * RET_HEAD_DIM:(h + 1) * RET_HEAD_DIM]
            r = th * cs + pltpu.roll(th, RET_HEAD_DIM // 2, 1) * sn
            if scale is not None:
                r = r * scale
            yield h, r.astype(BF16)

    for h, r in rotary(proj(_OFF_RQ, RET_WIDTH), None):
        rq_ref[:, h * RET_HEAD_DIM:(h + 1) * RET_HEAD_DIM] = r
    for h, r in rotary(proj(_OFF_RK, RET_WIDTH), RET_HEAD_DIM ** -0.5):
        rk_ref[:, h * RET_HEAD_DIM:(h + 1) * RET_HEAD_DIM] = r


def _inproj(x2, w_packed, wt_packed, bf_pad, cs, sn, seq):
    n = x2.shape[0]
    tm = TOKEN_TILE
    pos_blocks = seq // tm
    row = lambda i: (i, 0)
    col = lambda i: (0, i)
    const = lambda i: (0, 0)
    pos = lambda i: (i % pos_blocks, 0)
    wide = lambda w, dt: jax.ShapeDtypeStruct((n, w), dt)
    tall = lambda r: jax.ShapeDtypeStruct((r, n), BF16)
    return pl.pallas_call(
        _inproj_kernel,
        grid=(n // tm,),
        in_specs=[pl.BlockSpec((tm, D_MODEL), row),
                  pl.BlockSpec((D_MODEL, PACKED_COLS), const),
                  pl.BlockSpec((PACKED_ROWS, D_MODEL), const),
                  pl.BlockSpec((1, LANES), const),
                  pl.BlockSpec((tm, LANES), pos),
                  pl.BlockSpec((tm, LANES), pos)],
        out_specs=[pl.BlockSpec((FOX_WIDTH, tm), col),
                   pl.BlockSpec((tm, FOX_WIDTH), row),
                   pl.BlockSpec((FOX_HEADS * V_ROWS, tm), col),
                   pl.BlockSpec((tm, LANES), row)]
                  + [pl.BlockSpec((tm, RET_WIDTH), row)] * 4,
        out_shape=[tall(FOX_WIDTH), wide(FOX_WIDTH, BF16), tall(FOX_HEADS * V_ROWS), wide(LANES, F32)]
                  + [wide(RET_WIDTH, BF16)] * 3 + [wide(RET_WIDTH, F32)],
        compiler_params=_params("parallel"),
        name="in_proj",
    )(x2, w_packed, wt_packed, bf_pad, cs, sn)


def _split3(t):
    hi = t.astype(BF16)
    r1 = t - hi.astype(F32)
    mid = r1.astype(BF16)
    lo = (r1 - mid.astype(F32)).astype(BF16)
    return hi, mid, lo


def _dot3(pieces, m):
    return sum(jnp.dot(p, m, preferred_element_type=F32) for p in pieces)


def _cumsum_kernel(x_ref, o_ref):
    heads, rows, lanes = x_ref.shape[1:]
    j = lax.broadcasted_iota(jnp.int32, (lanes, lanes), 0)
    l = lax.broadcasted_iota(jnp.int32, (lanes, lanes), 1)
    upper = (j <= l).astype(BF16)
    ones = jnp.ones((lanes, lanes), BF16)
    r_out = lax.broadcasted_iota(jnp.int32, (rows, rows), 0)
    r_in = lax.broadcasted_iota(jnp.int32, (rows, rows), 1)
    strict = (r_in < r_out).astype(BF16)
    for h in range(heads):
        pieces = _split3(x_ref[0, h] * (-LOG2E))
        within = _dot3(pieces, upper)
        totals = _dot3(pieces, ones)
        offs = sum(jnp.dot(strict, p, preferred_element_type=F32) for p in _split3(totals))
        for t, piece in enumerate(_split3(within + offs)):
            o_ref[0, t, h] = piece


def _neg_cumsum_pieces(lf):
    b, h, s = lf.shape
    rows = s // LANES
    return pl.pallas_call(
        _cumsum_kernel,
        grid=(b,),
        in_specs=[pl.BlockSpec((1, h, rows, LANES), lambda i: (i, 0, 0, 0))],
        out_specs=pl.BlockSpec((1, BIAS_PIECES, h, rows, LANES), lambda i: (i, 0, 0, 0, 0)),
        out_shape=jax.ShapeDtypeStruct((b, BIAS_PIECES, h, rows, LANES), BF16),
        compiler_params=_params("parallel"),
        name="forget_cumsum",
    )(lf.reshape(b, h, rows, LANES)).reshape(b, BIAS_PIECES, h, s)


def _fox_kernel(qt_ref, k_ref, cb_ref, vt_ref, o_ref, qaug_ref, m_ref, acc_ref, s_ref, cmax_ref):
    t = ATTN_TILE
    qi = pl.program_id(2)
    qt = qt_ref[...]
    sub = lax.broadcasted_iota(jnp.int32, (LANES, t), 0)
    for a in range(2):
        mine = (sub >= a * FOX_HEAD_DIM) & (sub < (a + 1) * FOX_HEAD_DIM)
        qaug_ref[a, 0:LANES, :] = jnp.where(mine, qt, jnp.zeros_like(qt))
        pick = (sub >= a * BIAS_PIECES) & (sub < (a + 1) * BIAS_PIECES)
        qaug_ref[a, LANES:MXU_DEPTH, :] = jnp.where(pick, 1.0, 0.0).astype(BF16)
    m_ref[...] = jnp.full(m_ref.shape, -jnp.inf, F32)
    acc_ref[...] = jnp.zeros(acc_ref.shape, F32)

    def scores(ki, masked):
        start = pl.multiple_of(ki * t, t)
        kaug = jnp.concatenate([k_ref[0, pl.ds(start, t), :], cb_ref[0, 0, pl.ds(start, t), :]], axis=1)
        if masked:
            key = lax.broadcasted_iota(jnp.int32, (t, t), 0) + ki * t
            qry = lax.broadcasted_iota(jnp.int32, (t, t), 1) + qi * t
            keep = key <= qry
        for a in range(2):
            s = jnp.dot(kaug, qaug_ref[a], preferred_element_type=F32)
            if masked:
                s = jnp.where(keep, s, -jnp.inf)
            s_ref[a] = s
            cmax_ref[a] = jnp.max(s, axis=0, keepdims=True)

    def weights():
        out = []
        for a in range(2):
            m_prev = m_ref[a]
            m_new = jnp.maximum(m_prev, cmax_ref[a])
            out.append((jnp.exp2(s_ref[a] - m_new).astype(BF16), jnp.exp2(m_prev - m_new)))
            m_ref[a] = m_new
        return out

    def values(ki, pw):
        start = pl.multiple_of(ki * t, t)
        for a, (p, alpha) in enumerate(pw):
            vt = vt_ref[a * V_ROWS:(a + 1) * V_ROWS, pl.ds(start, t)]
            acc_ref[a] = alpha * acc_ref[a] + jnp.dot(vt, p, preferred_element_type=F32)

    def step(ki, masked):
        pw = weights()
        scores(ki + 1, masked)
        values(ki, pw)

    def body(kp, carry):
        for u in range(ATTN_UNROLL):
            step(kp * ATTN_UNROLL + u, False)
        return carry

    scores(0, True)
    full = jnp.maximum(qi - 1, 0)
    groups = full // ATTN_UNROLL
    lax.fori_loop(0, groups, body, 0)
    for u in range(ATTN_UNROLL - 1):
        @pl.when(groups * ATTN_UNROLL + u < full)
        def _():
            step(groups * ATTN_UNROLL + u, False)

    @pl.when(qi > 0)
    def _():
        step(qi - 1, True)

    values(qi, weights())

    outs = [acc_ref[a, 0:FOX_HEAD_DIM, :] / acc_ref[a, FOX_HEAD_DIM:FOX_HEAD_DIM + 1, :] for a in range(2)]
    o_ref[0] = jnp.concatenate(outs, axis=0).T.astype(o_ref.dtype)


def _fox_attention(fqt, fk, cb, fvt, batch, seq):
    t = ATTN_TILE
    nq = seq // t
    return pl.pallas_call(
        _fox_kernel,
        grid=(batch, FOX_PAIRS, nq),
        in_specs=[pl.BlockSpec((LANES, t), lambda bi, hp, qi: (hp, bi * nq + qi)),
                  pl.BlockSpec((1, seq, LANES), lambda bi, hp, qi: (bi, 0, hp)),
                  pl.BlockSpec((1, 1, seq, LANES), lambda bi, hp, qi: (bi, hp, 0, 0)),
                  pl.BlockSpec((2 * V_ROWS, seq), lambda bi, hp, qi: (hp, bi))],
        out_specs=pl.BlockSpec((1, t, LANES), lambda bi, hp, qi: (bi, qi, hp)),
        out_shape=jax.ShapeDtypeStruct((batch, seq, FOX_WIDTH), BF16),
        scratch_shapes=[pltpu.VMEM((2, MXU_DEPTH, t), BF16),
                        pltpu.VMEM((2, 1, t), F32),
                        pltpu.VMEM((2, V_ROWS, t), F32),
                        pltpu.VMEM((2, t, t), F32),
                        pltpu.VMEM((2, 1, t), F32)],
        compiler_params=_params("parallel", "parallel", "arbitrary"),
        name="fox_attention",
    )(fqt, fk, cb, fvt)


def _ret_kernel(q_ref, k_ref, v_ref, g_ref, dm_ref, xi_ref, zeta_ref, cd_ref, gn_ref,
                o_ref, state_ref):
    @pl.when(pl.program_id(1) == 0)
    def _():
        state_ref[...] = jnp.zeros(state_ref.shape, F32)

    c = RET_CHUNK
    for h in range(RET_HEADS):
        cols = slice(h * RET_HEAD_DIM, (h + 1) * RET_HEAD_DIM)
        st = state_ref[h]
        for n in range(RET_TILE // c):
            rows = slice(n * c, (n + 1) * c)
            q = q_ref[0, rows, cols]
            k = k_ref[0, rows, cols]
            v = v_ref[0, rows, cols]
            inner = lax.dot_general(q, k, _NT, preferred_element_type=F32) * dm_ref[h]
            o = (jnp.dot(inner.astype(BF16), v, preferred_element_type=F32)
                 + xi_ref[h] * jnp.dot(q, st.astype(BF16), preferred_element_type=F32))
            kz = (k.astype(F32) * zeta_ref[h]).astype(BF16)
            st = cd_ref[h] * st + lax.dot_general(kz, v, (((0,), (0,)), ((), ())),
                                                  preferred_element_type=F32)
            mu = jnp.mean(o, axis=-1, keepdims=True)
            d = o - mu
            var = jnp.mean(d * d, axis=-1, keepdims=True)
            y = d * lax.rsqrt(var + GN_EPS) * gn_ref[h:h + 1, :]
            g = g_ref[0, rows, cols]
            o_ref[0, rows, cols] = (g * jax.nn.sigmoid(g) * y).astype(o_ref.dtype)
        state_ref[h] = st


def _retention(rq, rk, rv, rg, dm, xi, zeta, cd, gn):
    b, s, _ = rq.shape
    t = RET_TILE
    tok = pl.BlockSpec((1, t, RET_WIDTH), lambda bi, si: (bi, si, 0))
    tab = pl.BlockSpec((RET_HEADS, RET_CHUNK, RET_HEAD_DIM), lambda bi, si: (0, 0, 0))
    return pl.pallas_call(
        _ret_kernel,
        grid=(b, s // t),
        in_specs=[tok, tok, tok, tok, tab, tab, tab, tab,
                  pl.BlockSpec((RET_HEADS, RET_HEAD_DIM), lambda bi, si: (0, 0))],
        out_specs=tok,
        out_shape=jax.ShapeDtypeStruct((b, s, RET_WIDTH), BF16),
        scratch_shapes=[pltpu.VMEM((RET_HEADS, RET_HEAD_DIM, RET_HEAD_DIM), F32)],
        compiler_params=_params("parallel", "arbitrary"),
        name="retention",
    )(rq, rk, rv, rg, dm, xi, zeta, cd, gn)


def _layernorm(y, g, b):
    mu = jnp.mean(y, axis=-1, keepdims=True)
    d = y - mu
    var = jnp.mean(d * d, axis=-1, keepdims=True)
    return d * lax.rsqrt(var + LN_EPS) * g + b


def _outproj_kernel(alpha, fox_ref, ret_ref, x_ref, wa_ref, wb_ref, g_ref, b_ref, o_ref):
    mix = (jnp.dot(fox_ref[...], wa_ref[...], preferred_element_type=F32)
           + jnp.dot(ret_ref[...], wb_ref[...], preferred_element_type=F32))
    o_ref[...] = _layernorm(alpha * x_ref[...] + mix, g_ref[...], b_ref[...])


def _outproj(alpha, fox, ret, x2, wa, wb, g, b):
    n = x2.shape[0]
    tm = TOKEN_TILE
    row = lambda i: (i, 0)
    const = lambda i: (0, 0)
    return pl.pallas_call(
        functools.partial(_outproj_kernel, alpha),
        grid=(n // tm,),
        in_specs=[pl.BlockSpec((tm, FOX_WIDTH), row),
                  pl.BlockSpec((tm, RET_WIDTH), row),
                  pl.BlockSpec((tm, D_MODEL), row),
                  pl.BlockSpec((FOX_WIDTH, D_MODEL), const),
                  pl.BlockSpec((RET_WIDTH, D_MODEL), const),
                  pl.BlockSpec((1, D_MODEL), const),
                  pl.BlockSpec((1, D_MODEL), const)],
        out_specs=pl.BlockSpec((tm, D_MODEL), row),
        out_shape=jax.ShapeDtypeStruct((n, D_MODEL), F32),
        compiler_params=_params("parallel"),
        name="out_proj_ln",
    )(fox, ret, x2, wa, wb, g, b)


def _ffn_kernel(alpha, x_ref, w1_ref, w2_ref, g_ref, b_ref, o_ref):
    x = x_ref[...]
    xb = x.astype(BF16)
    acc = alpha * x
    for c in range(D_FF // FF_CHUNK):
        cols = slice(c * FF_CHUNK, (c + 1) * FF_CHUNK)
        h = jnp.maximum(jnp.dot(xb, w1_ref[:, cols], preferred_element_type=F32), 0.0)
        acc = acc + jnp.dot((h * h).astype(BF16), w2_ref[cols, :], preferred_element_type=F32)
    o_ref[...] = _layernorm(acc, g_ref[...], b_ref[...])


def _ffn(alpha, x2, w1, w2, g, b):
    n = x2.shape[0]
    tm = TOKEN_TILE
    row = lambda i: (i, 0)
    const = lambda i: (0, 0)
    return pl.pallas_call(
        functools.partial(_ffn_kernel, alpha),
        grid=(n // tm,),
        in_specs=[pl.BlockSpec((tm, D_MODEL), row),
                  pl.BlockSpec((D_MODEL, D_FF), const),
                  pl.BlockSpec((D_FF, D_MODEL), const),
                  pl.BlockSpec((1, D_MODEL), const),
                  pl.BlockSpec((1, D_MODEL), const)],
        out_specs=pl.BlockSpec((tm, D_MODEL), row),
        out_shape=jax.ShapeDtypeStruct((n, D_MODEL), F32),
        compiler_params=_params("parallel"),
        name="ffn_ln",
    )(x2, w1, w2, g, b)


def _rotary_tables(seq):
    half = RET_HEAD_DIM // 2
    inv_freq = ROPE_BASE ** (-jnp.arange(half, dtype=F32) / half)
    ang = jnp.arange(seq, dtype=F32)[:, None] * inv_freq[None, :]
    cos, sin = jnp.cos(ang), jnp.sin(ang)
    return jnp.concatenate([cos, cos], axis=-1), jnp.concatenate([-sin, sin], axis=-1)


def _decay_tables():
    gamma = 1.0 - jnp.exp2(-5.0 - jnp.arange(RET_HEADS, dtype=F32))
    log_g = jnp.log(gamma)
    idx = jnp.arange(RET_CHUNK, dtype=F32)
    diff = idx[:, None] - idx[None, :]
    dm = jnp.where(diff >= 0, jnp.exp(log_g[:, None, None] * jnp.maximum(diff, 0.0)), 0.0)
    shape = (RET_HEADS, RET_CHUNK, RET_HEAD_DIM)
    xi = jnp.broadcast_to(jnp.exp(log_g[:, None] * (idx + 1.0))[..., None], shape)
    zeta = jnp.broadcast_to(jnp.exp(log_g[:, None] * (RET_CHUNK - 1.0 - idx))[..., None], shape)
    cd = jnp.broadcast_to(jnp.exp(log_g * RET_CHUNK)[:, None, None], shape)
    return dm, xi, zeta, cd


def _pack_w_in(w):
    d = w.shape[0]
    wq, wk, wv = (w[:, i * FOX_WIDTH:(i + 1) * FOX_WIDTH] for i in range(3))
    lo = 3 * FOX_WIDTH
    logits = w[:, lo:lo + FOX_HEADS]
    pad = jnp.zeros((d, LANES - FOX_HEADS), w.dtype)
    packed = jnp.concatenate([wk, w[:, lo + FOX_HEADS:], logits, pad], axis=1).astype(BF16)
    wv_t = wv.T.reshape(FOX_HEADS, FOX_HEAD_DIM, d)
    wv_t = jnp.pad(wv_t, ((0, 0), (0, V_ROWS - FOX_HEAD_DIM), (0, 0))).reshape(FOX_HEADS * V_ROWS, d)
    packed_t = jnp.concatenate([wq.T, wv_t], axis=0).astype(BF16)
    return packed, packed_t


def _bias_operand(pieces, batch, seq):
    p = pieces.reshape(batch, BIAS_PIECES, FOX_PAIRS, 2, seq)
    p = p.transpose(0, 2, 4, 3, 1).reshape(batch, FOX_PAIRS, seq, 2 * BIAS_PIECES)
    return jnp.pad(p, ((0, 0), (0, 0), (0, 0), (0, LANES - 2 * BIAS_PIECES)))


@jax.jit
def kernel(x, w_in, w_out, w_ff1, w_ff2, ln1_g, ln1_b, ln2_g, ln2_b, b_forget, ret_gn_g):
    batch, seq, _ = x.shape
    depth = w_in.shape[0]
    alpha = (2 * depth) ** 0.25
    cs, sn = _rotary_tables(seq)
    dm, xi, zeta, cd = _decay_tables()
    x2 = x.reshape(batch * seq, D_MODEL)
    for layer in range(depth):
        bf_pad = jnp.pad(b_forget[layer], (0, LANES - FOX_HEADS)).reshape(1, LANES)
        w_packed, wt_packed = _pack_w_in(w_in[layer])
        fqt, fk, fvt, lf, rq, rk, rv, rg = _inproj(x2, w_packed, wt_packed, bf_pad, cs, sn, seq)
        seq3 = lambda t: t.reshape(batch, seq, t.shape[-1])
        lf = seq3(lf)[:, :, :FOX_HEADS].transpose(0, 2, 1)
        cb = _bias_operand(_neg_cumsum_pieces(lf), batch, seq)
        fox = _fox_attention(fqt, seq3(fk), cb, fvt, batch, seq)
        ret = _retention(seq3(rq), seq3(rk), seq3(rv), seq3(rg), dm, xi, zeta, cd, ret_gn_g[layer])
        wo = w_out[layer].astype(BF16)
        x2 = _outproj(alpha, fox.reshape(batch * seq, FOX_WIDTH), ret.reshape(batch * seq, RET_WIDTH),
                      x2, wo[:FOX_WIDTH], wo[FOX_WIDTH:],
                      ln1_g[layer].reshape(1, D_MODEL), ln1_b[layer].reshape(1, D_MODEL))
        x2 = _ffn(alpha, x2, w_ff1[layer].astype(BF16), w_ff2[layer].astype(BF16),
                  ln2_g[layer].reshape(1, D_MODEL), ln2_b[layer].reshape(1, D_MODEL))
    return x2.reshape(batch, seq, D_MODEL)
```

```python
import functools
import math

import jax
import jax.numpy as jnp
from jax import lax
from jax.experimental import pallas as pl
from jax.experimental.pallas import tpu as pltpu

F32 = jnp.float32
BF16 = jnp.bfloat16

D_MODEL = 1024
FOX_HEADS = 8
FOX_HEAD_DIM = 64
FOX_WIDTH = FOX_HEADS * FOX_HEAD_DIM
RET_HEADS = 4
RET_HEAD_DIM = 128
RET_WIDTH = RET_HEADS * RET_HEAD_DIM
D_FF = 4 * D_MODEL
RET_CHUNK = 128
ROPE_BASE = 10000.0
LN_EPS = 1e-5
GN_EPS = 1e-5
LOG2E = math.log2(math.e)

LANES = 128
MXU_DEPTH = 256
VMEM_LIMIT = 56 * 1024 * 1024

TOKEN_TILE = 512
ATTN_TILE = 512
ZERO_WEIGHT_LOG2 = 150.0
BOUND_SLACK = 2.0
NORM_INFLATE = 1.01
RET_TILE = 512
FF_CHUNK = 1024

FOX_PAIRS = FOX_HEADS // 2
V_ROWS = LANES
BIAS_PIECES = 3

_OFF_FK = 0
_OFF_RQ = FOX_WIDTH
_OFF_RK, _OFF_RV, _OFF_RG = _OFF_RQ + RET_WIDTH, _OFF_RQ + 2 * RET_WIDTH, _OFF_RQ + 3 * RET_WIDTH
_OFF_FL = _OFF_RQ + 4 * RET_WIDTH
PACKED_COLS = _OFF_FL + LANES
_ROW_FV = FOX_WIDTH
PACKED_ROWS = FOX_WIDTH + FOX_HEADS * V_ROWS

_NT = (((1,), (1,)), ((), ()))


def _params(*semantics):
    return pltpu.CompilerParams(dimension_semantics=semantics, vmem_limit_bytes=VMEM_LIMIT)


def _inproj_kernel(x_ref, w_ref, wt_ref, bf_ref, cs_ref, sn_ref,
                   fqt_ref, fk_ref, fvt_ref, lf_ref, kn2_ref, rq_ref, rk_ref, rv_ref, rg_ref):
    xb = x_ref[...].astype(BF16)

    def proj(lo, width):
        return jnp.dot(xb, w_ref[:, lo:lo + width], preferred_element_type=F32)

    def proj_t(lo, rows):
        return lax.dot_general(wt_ref[lo:lo + rows, :], xb, _NT, preferred_element_type=F32)

    fqt_ref[...] = (proj_t(0, FOX_WIDTH) * (FOX_HEAD_DIM ** -0.5 * LOG2E)).astype(BF16)
    kb = proj(_OFF_FK, FOX_WIDTH).astype(BF16)
    fk_ref[...] = kb
    kf = kb.astype(F32)
    col = lax.broadcasted_iota(jnp.int32, (FOX_WIDTH, LANES), 0)
    lane = lax.broadcasted_iota(jnp.int32, (FOX_WIDTH, LANES), 1)
    sel = (col // FOX_HEAD_DIM == lane).astype(BF16)
    kn2_ref[...] = jnp.dot((kf * kf).astype(BF16), sel, preferred_element_type=F32)
    vt = proj_t(_ROW_FV, FOX_HEADS * V_ROWS)
    row = lax.broadcasted_iota(jnp.int32, vt.shape, 0)
    fvt_ref[...] = jnp.where(row % V_ROWS == FOX_HEAD_DIM, 1.0, vt).astype(BF16)
    rv_ref[...] = proj(_OFF_RV, RET_WIDTH).astype(BF16)
    rg_ref[...] = proj(_OFF_RG, RET_WIDTH)

    z = proj(_OFF_FL, LANES) + bf_ref[...]
    lf_ref[...] = -(jnp.maximum(-z, 0.0) + jnp.log1p(jnp.exp(-jnp.abs(z))))

    cs = cs_ref[...]
    sn = sn_ref[...]

    def rotary(t, scale):
        for h in range(RET_HEADS):
            th = t[:, h * RET_HEAD_DIM:(h + 1) * RET_HEAD_DIM]
            r = th * cs + pltpu.roll(th, RET_HEAD_DIM // 2, 1) * sn
            if scale is not None:
                r = r * scale
            yield h, r.astype(BF16)

    for h, r in rotary(proj(_OFF_RQ, RET_WIDTH), None):
        rq_ref[:, h * RET_HEAD_DIM:(h + 1) * RET_HEAD_DIM] = r
    for h, r in rotary(proj(_OFF_RK, RET_WIDTH), RET_HEAD_DIM ** -0.5):
        rk_ref[:, h * RET_HEAD_DIM:(h + 1) * RET_HEAD_DIM] = r


def _inproj(x2, w_packed, wt_packed, bf_pad, cs, sn, seq):
    n = x2.shape[0]
    tm = TOKEN_TILE
    pos_blocks = seq // tm
    row = lambda i: (i, 0)
    col = lambda i: (0, i)
    const = lambda i: (0, 0)
    pos = lambda i: (i % pos_blocks, 0)
    wide = lambda w, dt: jax.ShapeDtypeStruct((n, w), dt)
    tall = lambda r: jax.ShapeDtypeStruct((r, n), BF16)
    return pl.pallas_call(
        _inproj_kernel,
        grid=(n // tm,),
        in_specs=[pl.BlockSpec((tm, D_MODEL), row),
                  pl.BlockSpec((D_MODEL, PACKED_COLS), const),
                  pl.BlockSpec((PACKED_ROWS, D_MODEL), const),
                  pl.BlockSpec((1, LANES), const),
                  pl.BlockSpec((tm, LANES), pos),
                  pl.BlockSpec((tm, LANES), pos)],
        out_specs=[pl.BlockSpec((FOX_WIDTH, tm), col),
                   pl.BlockSpec((tm, FOX_WIDTH), row),
                   pl.BlockSpec((FOX_HEADS * V_ROWS, tm), col),
                   pl.BlockSpec((tm, LANES), row),
                   pl.BlockSpec((tm, LANES), row)]
                  + [pl.BlockSpec((tm, RET_WIDTH), row)] * 4,
        out_shape=[tall(FOX_WIDTH), wide(FOX_WIDTH, BF16), tall(FOX_HEADS * V_ROWS), wide(LANES, F32),
                   wide(LANES, F32)]
                  + [wide(RET_WIDTH, BF16)] * 3 + [wide(RET_WIDTH, F32)],
        compiler_params=_params("parallel"),
        name="in_proj",
    )(x2, w_packed, wt_packed, bf_pad, cs, sn)


def _split3(t):
    hi = t.astype(BF16)
    r1 = t - hi.astype(F32)
    mid = r1.astype(BF16)
    lo = (r1 - mid.astype(F32)).astype(BF16)
    return hi, mid, lo


def _dot3(pieces, m):
    return sum(jnp.dot(p, m, preferred_element_type=F32) for p in pieces)


def _cumsum_kernel(x_ref, o_ref):
    heads, rows, lanes = x_ref.shape[1:]
    j = lax.broadcasted_iota(jnp.int32, (lanes, lanes), 0)
    l = lax.broadcasted_iota(jnp.int32, (lanes, lanes), 1)
    upper = (j <= l).astype(BF16)
    ones = jnp.ones((lanes, lanes), BF16)
    r_out = lax.broadcasted_iota(jnp.int32, (rows, rows), 0)
    r_in = lax.broadcasted_iota(jnp.int32, (rows, rows), 1)
    strict = (r_in < r_out).astype(BF16)
    for h in range(heads):
        pieces = _split3(x_ref[0, h] * (-LOG2E))
        within = _dot3(pieces, upper)
        totals = _dot3(pieces, ones)
        offs = sum(jnp.dot(strict, p, preferred_element_type=F32) for p in _split3(totals))
        for t, piece in enumerate(_split3(within + offs)):
            o_ref[0, t, h] = piece


def _neg_cumsum_pieces(lf):
    b, h, s = lf.shape
    rows = s // LANES
    return pl.pallas_call(
        _cumsum_kernel,
        grid=(b,),
        in_specs=[pl.BlockSpec((1, h, rows, LANES), lambda i: (i, 0, 0, 0))],
        out_specs=pl.BlockSpec((1, BIAS_PIECES, h, rows, LANES), lambda i: (i, 0, 0, 0, 0)),
        out_shape=jax.ShapeDtypeStruct((b, BIAS_PIECES, h, rows, LANES), BF16),
        compiler_params=_params("parallel"),
        name="forget_cumsum",
    )(lf.reshape(b, h, rows, LANES)).reshape(b, BIAS_PIECES, h, s)


def _fox_kernel(nb_ref, kn_ref, qt_ref, k_ref, cb_ref, vt_ref, o_ref,
                qaug_ref, m_ref, acc_ref, s_ref, cmax_ref):
    t = ATTN_TILE
    qi = pl.program_id(2)
    head0 = pl.program_id(0) * FOX_HEADS + 2 * pl.program_id(1)
    qt = qt_ref[...]
    q2 = qt.astype(F32) * qt.astype(F32)
    qnorm = [jnp.sqrt(jnp.sum(q2[a * FOX_HEAD_DIM:(a + 1) * FOX_HEAD_DIM], axis=0, keepdims=True))
             for a in range(2)]
    sub = lax.broadcasted_iota(jnp.int32, (LANES, t), 0)
    for a in range(2):
        mine = (sub >= a * FOX_HEAD_DIM) & (sub < (a + 1) * FOX_HEAD_DIM)
        qaug_ref[a, 0:LANES, :] = jnp.where(mine, qt, jnp.zeros_like(qt))
        pick = (sub >= a * BIAS_PIECES) & (sub < (a + 1) * BIAS_PIECES)
        qaug_ref[a, LANES:MXU_DEPTH, :] = jnp.where(pick, 1.0, 0.0).astype(BF16)
    m_ref[...] = jnp.full(m_ref.shape, -jnp.inf, F32)
    acc_ref[...] = jnp.zeros(acc_ref.shape, F32)

    def scores(ki, masked):
        start = pl.multiple_of(ki * t, t)
        kaug = jnp.concatenate([k_ref[0, pl.ds(start, t), :], cb_ref[0, 0, pl.ds(start, t), :]], axis=1)
        if masked:
            key = lax.broadcasted_iota(jnp.int32, (t, t), 0) + ki * t
            qry = lax.broadcasted_iota(jnp.int32, (t, t), 1) + qi * t
            keep = key <= qry
        for a in range(2):
            s = jnp.dot(kaug, qaug_ref[a], preferred_element_type=F32)
            if masked:
                s = jnp.where(keep, s, -jnp.inf)
            s_ref[a] = s
            cmax_ref[a] = jnp.max(s, axis=0, keepdims=True)

    def weights():
        out = []
        for a in range(2):
            m_prev = m_ref[a]
            m_new = jnp.maximum(m_prev, cmax_ref[a])
            out.append((jnp.exp2(s_ref[a] - m_new).astype(BF16), jnp.exp2(m_prev - m_new)))
            m_ref[a] = m_new
        return out

    def values(ki, pw):
        start = pl.multiple_of(ki * t, t)
        for a, (p, alpha) in enumerate(pw):
            vt = vt_ref[a * V_ROWS:(a + 1) * V_ROWS, pl.ds(start, t)]
            acc_ref[a] = alpha * acc_ref[a] + jnp.dot(vt, p, preferred_element_type=F32)

    def negligible(kt):
        done = None
        for a in range(2):
            row = head0 + a
            bound = nb_ref[row, kt] + qnorm[a] * kn_ref[row, kt]
            ok = jnp.max(bound - m_ref[a]) < -ZERO_WEIGHT_LOG2
            done = ok if done is None else done & ok
        return done

    def body(carry):
        k, _ = carry
        pw = weights()
        older = jnp.maximum(k - 1, 0)
        scores(older, False)
        values(k, pw)
        return k - 1, (k >= 1) & jnp.logical_not(negligible(older))

    scores(qi, True)
    lax.while_loop(lambda c: c[1], body, (qi, jnp.bool_(True)))

    outs = [acc_ref[a, 0:FOX_HEAD_DIM, :] / acc_ref[a, FOX_HEAD_DIM:FOX_HEAD_DIM + 1, :] for a in range(2)]
    o_ref[0] = jnp.concatenate(outs, axis=0).T.astype(o_ref.dtype)


def _fox_attention(nb, kn, fqt, fk, cb, fvt, batch, seq):
    t = ATTN_TILE
    nq = seq // t
    grid_spec = pltpu.PrefetchScalarGridSpec(
        num_scalar_prefetch=2,
        grid=(batch, FOX_PAIRS, nq),
        in_specs=[pl.BlockSpec((LANES, t), lambda bi, hp, qi, *_: (hp, bi * nq + qi)),
                  pl.BlockSpec((1, seq, LANES), lambda bi, hp, qi, *_: (bi, 0, hp)),
                  pl.BlockSpec((1, 1, seq, LANES), lambda bi, hp, qi, *_: (bi, hp, 0, 0)),
                  pl.BlockSpec((2 * V_ROWS, seq), lambda bi, hp, qi, *_: (hp, bi))],
        out_specs=pl.BlockSpec((1, t, LANES), lambda bi, hp, qi, *_: (bi, qi, hp)),
        scratch_shapes=[pltpu.VMEM((2, MXU_DEPTH, t), BF16),
                        pltpu.VMEM((2, 1, t), F32),
                        pltpu.VMEM((2, V_ROWS, t), F32),
                        pltpu.VMEM((2, t, t), F32),
                        pltpu.VMEM((2, 1, t), F32)])
    return pl.pallas_call(
        _fox_kernel,
        grid_spec=grid_spec,
        out_shape=jax.ShapeDtypeStruct((batch, seq, FOX_WIDTH), BF16),
        compiler_params=_params("parallel", "parallel", "arbitrary"),
        name="fox_attention",
    )(nb, kn, fqt, fk, cb, fvt)


def _score_bounds(pieces, kn2, batch, seq):
    nt = seq // ATTN_TILE
    negc = pieces.astype(F32).sum(axis=1)
    nb = lax.cummax(negc.reshape(batch, FOX_HEADS, nt, ATTN_TILE).max(axis=-1), axis=2) + BOUND_SLACK
    k2 = kn2[:, :FOX_HEADS].reshape(batch, nt, ATTN_TILE, FOX_HEADS).max(axis=2)
    kn = lax.cummax(jnp.sqrt(k2).transpose(0, 2, 1), axis=2) * NORM_INFLATE
    return nb.reshape(batch * FOX_HEADS, nt), kn.reshape(batch * FOX_HEADS, nt)


def _ret_kernel(q_ref, k_ref, v_ref, g_ref, dm_ref, xi_ref, zeta_ref, cd_ref, gn_ref,
                o_ref, state_ref):
    @pl.when(pl.program_id(1) == 0)
    def _():
        state_ref[...] = jnp.zeros(state_ref.shape, F32)

    c = RET_CHUNK
    for h in range(RET_HEADS):
        cols = slice(h * RET_HEAD_DIM, (h + 1) * RET_HEAD_DIM)
        st = state_ref[h]
        for n in range(RET_TILE // c):
            rows = slice(n * c, (n + 1) * c)
            q = q_ref[0, rows, cols]
            k = k_ref[0, rows, cols]
            v = v_ref[0, rows, cols]
            inner = lax.dot_general(q, k, _NT, preferred_element_type=F32) * dm_ref[h]
            o = (jnp.dot(inner.astype(BF16), v, preferred_element_type=F32)
                 + xi_ref[h] * jnp.dot(q, st.astype(BF16), preferred_element_type=F32))
            kz = (k.astype(F32) * zeta_ref[h]).astype(BF16)
            st = cd_ref[h] * st + lax.dot_general(kz, v, (((0,), (0,)), ((), ())),
                                                  preferred_element_type=F32)
            mu = jnp.mean(o, axis=-1, keepdims=True)
            d = o - mu
            var = jnp.mean(d * d, axis=-1, keepdims=True)
            y = d * lax.rsqrt(var + GN_EPS) * gn_ref[h:h + 1, :]
            g = g_ref[0, rows, cols]
            o_ref[0, rows, cols] = (g * jax.nn.sigmoid(g) * y).astype(o_ref.dtype)
        state_ref[h] = st


def _retention(rq, rk, rv, rg, dm, xi, zeta, cd, gn):
    b, s, _ = rq.shape
    t = RET_TILE
    tok = pl.BlockSpec((1, t, RET_WIDTH), lambda bi, si: (bi, si, 0))
    tab = pl.BlockSpec((RET_HEADS, RET_CHUNK, RET_HEAD_DIM), lambda bi, si: (0, 0, 0))
    return pl.pallas_call(
        _ret_kernel,
        grid=(b, s // t),
        in_specs=[tok, tok, tok, tok, tab, tab, tab, tab,
                  pl.BlockSpec((RET_HEADS, RET_HEAD_DIM), lambda bi, si: (0, 0))],
        out_specs=tok,
        out_shape=jax.ShapeDtypeStruct((b, s, RET_WIDTH), BF16),
        scratch_shapes=[pltpu.VMEM((RET_HEADS, RET_HEAD_DIM, RET_HEAD_DIM), F32)],
        compiler_params=_params("parallel", "arbitrary"),
        name="retention",
    )(rq, rk, rv, rg, dm, xi, zeta, cd, gn)


def _layernorm(y, g, b):
    mu = jnp.mean(y, axis=-1, keepdims=True)
    d = y - mu
    var = jnp.mean(d * d, axis=-1, keepdims=True)
    return d * lax.rsqrt(var + LN_EPS) * g + b


def _outproj_kernel(alpha, fox_ref, ret_ref, x_ref, wa_ref, wb_ref, g_ref, b_ref, o_ref):
    mix = (jnp.dot(fox_ref[...], wa_ref[...], preferred_element_type=F32)
           + jnp.dot(ret_ref[...], wb_ref[...], preferred_element_type=F32))
    o_ref[...] = _layernorm(alpha * x_ref[...] + mix, g_ref[...], b_ref[...])


def _outproj(alpha, fox, ret, x2, wa, wb, g, b):
    n = x2.shape[0]
    tm = TOKEN_TILE
    row = lambda i: (i, 0)
    const = lambda i: (0, 0)
    return pl.pallas_call(
        functools.partial(_outproj_kernel, alpha),
        grid=(n // tm,),
        in_specs=[pl.BlockSpec((tm, FOX_WIDTH), row),
                  pl.BlockSpec((tm, RET_WIDTH), row),
                  pl.BlockSpec((tm, D_MODEL), row),
                  pl.BlockSpec((FOX_WIDTH, D_MODEL), const),
                  pl.BlockSpec((RET_WIDTH, D_MODEL), const),
                  pl.BlockSpec((1, D_MODEL), const),
                  pl.BlockSpec((1, D_MODEL), const)],
        out_specs=pl.BlockSpec((tm, D_MODEL), row),
        out_shape=jax.ShapeDtypeStruct((n, D_MODEL), F32),
        compiler_params=_params("parallel"),
        name="out_proj_ln",
    )(fox, ret, x2, wa, wb, g, b)


def _ffn_kernel(alpha, x_ref, w1_ref, w2_ref, g_ref, b_ref, o_ref):
    x = x_ref[...]
    xb = x.astype(BF16)
    acc = alpha * x
    for c in range(D_FF // FF_CHUNK):
        cols = slice(c * FF_CHUNK, (c + 1) * FF_CHUNK)
        h = jnp.maximum(jnp.dot(xb, w1_ref[:, cols], preferred_element_type=F32), 0.0)
        acc = acc + jnp.dot((h * h).astype(BF16), w2_ref[cols, :], preferred_element_type=F32)
    o_ref[...] = _layernorm(acc, g_ref[...], b_ref[...])


def _ffn(alpha, x2, w1, w2, g, b):
    n = x2.shape[0]
    tm = TOKEN_TILE
    row = lambda i: (i, 0)
    const = lambda i: (0, 0)
    return pl.pallas_call(
        functools.partial(_ffn_kernel, alpha),
        grid=(n // tm,),
        in_specs=[pl.BlockSpec((tm, D_MODEL), row),
                  pl.BlockSpec((D_MODEL, D_FF), const),
                  pl.BlockSpec((D_FF, D_MODEL), const),
                  pl.BlockSpec((1, D_MODEL), const),
                  pl.BlockSpec((1, D_MODEL), const)],
        out_specs=pl.BlockSpec((tm, D_MODEL), row),
        out_shape=jax.ShapeDtypeStruct((n, D_MODEL), F32),
        compiler_params=_params("parallel"),
        name="ffn_ln",
    )(x2, w1, w2, g, b)


def _rotary_tables(seq):
    half = RET_HEAD_DIM // 2
    inv_freq = ROPE_BASE ** (-jnp.arange(half, dtype=F32) / half)
    ang = jnp.arange(seq, dtype=F32)[:, None] * inv_freq[None, :]
    cos, sin = jnp.cos(ang), jnp.sin(ang)
    return jnp.concatenate([cos, cos], axis=-1), jnp.concatenate([-sin, sin], axis=-1)


def _decay_tables():
    gamma = 1.0 - jnp.exp2(-5.0 - jnp.arange(RET_HEADS, dtype=F32))
    log_g = jnp.log(gamma)
    idx = jnp.arange(RET_CHUNK, dtype=F32)
    diff = idx[:, None] - idx[None, :]
    dm = jnp.where(diff >= 0, jnp.exp(log_g[:, None, None] * jnp.maximum(diff, 0.0)), 0.0)
    shape = (RET_HEADS, RET_CHUNK, RET_HEAD_DIM)
    xi = jnp.broadcast_to(jnp.exp(log_g[:, None] * (idx + 1.0))[..., None], shape)
    zeta = jnp.broadcast_to(jnp.exp(log_g[:, None] * (RET_CHUNK - 1.0 - idx))[..., None], shape)
    cd = jnp.broadcast_to(jnp.exp(log_g * RET_CHUNK)[:, None, None], shape)
    return dm, xi, zeta, cd


def _pack_w_in(w):
    d = w.shape[0]
    wq, wk, wv = (w[:, i * FOX_WIDTH:(i + 1) * FOX_WIDTH] for i in range(3))
    lo = 3 * FOX_WIDTH
    logits = w[:, lo:lo + FOX_HEADS]
    pad = jnp.zeros((d, LANES - FOX_HEADS), w.dtype)
    packed = jnp.concatenate([wk, w[:, lo + FOX_HEADS:], logits, pad], axis=1).astype(BF16)
    wv_t = wv.T.reshape(FOX_HEADS, FOX_HEAD_DIM, d)
    wv_t = jnp.pad(wv_t, ((0, 0), (0, V_ROWS - FOX_HEAD_DIM), (0, 0))).reshape(FOX_HEADS * V_ROWS, d)
    packed_t = jnp.concatenate([wq.T, wv_t], axis=0).astype(BF16)
    return packed, packed_t


def _bias_operand(pieces, batch, seq):
    p = pieces.reshape(batch, BIAS_PIECES, FOX_PAIRS, 2, seq)
    p = p.transpose(0, 2, 4, 3, 1).reshape(batch, FOX_PAIRS, seq, 2 * BIAS_PIECES)
    return jnp.pad(p, ((0, 0), (0, 0), (0, 0), (0, LANES - 2 * BIAS_PIECES)))


@jax.jit
def kernel(x, w_in, w_out, w_ff1, w_ff2, ln1_g, ln1_b, ln2_g, ln2_b, b_forget, ret_gn_g):
    batch, seq, _ = x.shape
    depth = w_in.shape[0]
    alpha = (2 * depth) ** 0.25
    cs, sn = _rotary_tables(seq)
    dm, xi, zeta, cd = _decay_tables()
    x2 = x.reshape(batch * seq, D_MODEL)
    for layer in range(depth):
        bf_pad = jnp.pad(b_forget[layer], (0, LANES - FOX_HEADS)).reshape(1, LANES)
        w_packed, wt_packed = _pack_w_in(w_in[layer])
        fqt, fk, fvt, lf, kn2, rq, rk, rv, rg = _inproj(x2, w_packed, wt_packed, bf_pad, cs, sn, seq)
        seq3 = lambda t: t.reshape(batch, seq, t.shape[-1])
        lf = seq3(lf)[:, :, :FOX_HEADS].transpose(0, 2, 1)
        pieces = _neg_cumsum_pieces(lf)
        nb, kn = _score_bounds(pieces, kn2, batch, seq)
        fox = _fox_attention(nb, kn, fqt, seq3(fk), _bias_operand(pieces, batch, seq), fvt, batch, seq)
        ret = _retention(seq3(rq), seq3(rk), seq3(rv), seq3(rg), dm, xi, zeta, cd, ret_gn_g[layer])
        wo = w_out[layer].astype(BF16)
        x2 = _outproj(alpha, fox.reshape(batch * seq, FOX_WIDTH), ret.reshape(batch * seq, RET_WIDTH),
                      x2, wo[:FOX_WIDTH], wo[FOX_WIDTH:],
                      ln1_g[layer].reshape(1, D_MODEL), ln1_b[layer].reshape(1, D_MODEL))
        x2 = _ffn(alpha, x2, w_ff1[layer].astype(BF16), w_ff2[layer].astype(BF16),
                  ln2_g[layer].reshape(1, D_MODEL), ln2_b[layer].reshape(1, D_MODEL))
    return x2.reshape(batch, seq, D_MODEL)
```

```python
import functools
import math

import jax
import jax.numpy as jnp
from jax import lax
from jax.experimental import pallas as pl
from jax.experimental.pallas import tpu as pltpu

F32 = jnp.float32
BF16 = jnp.bfloat16

D_MODEL = 1024
FOX_HEADS = 8
FOX_HEAD_DIM = 64
FOX_WIDTH = FOX_HEADS * FOX_HEAD_DIM
RET_HEADS = 4
RET_HEAD_DIM = 128
RET_WIDTH = RET_HEADS * RET_HEAD_DIM
D_FF = 4 * D_MODEL
RET_CHUNK = 128
ROPE_BASE = 10000.0
LN_EPS = 1e-5
GN_EPS = 1e-5
LOG2E = math.log2(math.e)

LANES = 128
MXU_DEPTH = 256
VMEM_LIMIT = 56 * 1024 * 1024

TOKEN_TILE = 512
ATTN_TILE = 512
ATTN_UNROLL = 4
ZERO_WEIGHT_LOG2 = 150.0
BOUND_SLACK = 2.0
NORM_INFLATE = 1.01
RET_TILE = 512
FF_CHUNK = 1024

FOX_PAIRS = FOX_HEADS // 2
V_ROWS = LANES
BIAS_PIECES = 3

_OFF_FK = 0
_OFF_RQ = FOX_WIDTH
_OFF_RK, _OFF_RV, _OFF_RG = _OFF_RQ + RET_WIDTH, _OFF_RQ + 2 * RET_WIDTH, _OFF_RQ + 3 * RET_WIDTH
_OFF_FL = _OFF_RQ + 4 * RET_WIDTH
PACKED_COLS = _OFF_FL + LANES
_ROW_FV = FOX_WIDTH
PACKED_ROWS = FOX_WIDTH + FOX_HEADS * V_ROWS

_NT = (((1,), (1,)), ((), ()))


def _params(*semantics):
    return pltpu.CompilerParams(dimension_semantics=semantics, vmem_limit_bytes=VMEM_LIMIT)


def _inproj_kernel(x_ref, w_ref, wt_ref, bf_ref, cs_ref, sn_ref,
                   fqt_ref, fk_ref, fvt_ref, lf_ref, kn2_ref, rq_ref, rk_ref, rv_ref, rg_ref):
    xb = x_ref[...].astype(BF16)

    def proj(lo, width):
        return jnp.dot(xb, w_ref[:, lo:lo + width], preferred_element_type=F32)

    def proj_t(lo, rows):
        return lax.dot_general(wt_ref[lo:lo + rows, :], xb, _NT, preferred_element_type=F32)

    fqt_ref[...] = (proj_t(0, FOX_WIDTH) * (FOX_HEAD_DIM ** -0.5 * LOG2E)).astype(BF16)
    kb = proj(_OFF_FK, FOX_WIDTH).astype(BF16)
    fk_ref[...] = kb
    kf = kb.astype(F32)
    col = lax.broadcasted_iota(jnp.int32, (FOX_WIDTH, LANES), 0)
    lane = lax.broadcasted_iota(jnp.int32, (FOX_WIDTH, LANES), 1)
    sel = (col // FOX_HEAD_DIM == lane).astype(BF16)
    kn2_ref[...] = jnp.dot((kf * kf).astype(BF16), sel, preferred_element_type=F32)
    vt = proj_t(_ROW_FV, FOX_HEADS * V_ROWS)
    row = lax.broadcasted_iota(jnp.int32, vt.shape, 0)
    fvt_ref[...] = jnp.where(row % V_ROWS == FOX_HEAD_DIM, 1.0, vt).astype(BF16)
    rv_ref[...] = proj(_OFF_RV, RET_WIDTH).astype(BF16)
    rg_ref[...] = proj(_OFF_RG, RET_WIDTH)

    z = proj(_OFF_FL, LANES) + bf_ref[...]
    lf_ref[...] = -(jnp.maximum(-z, 0.0) + jnp.log1p(jnp.exp(-jnp.abs(z))))

    cs = cs_ref[...]
    sn = sn_ref[...]

    def rotary(t, scale):
        for h in range(RET_HEADS):
            th = t[:, h * RET_HEAD_DIM:(h + 1) * RET_HEAD_DIM]
            r = th * cs + pltpu.roll(th, RET_HEAD_DIM // 2, 1) * sn
            if scale is not None:
                r = r * scale
            yield h, r.astype(BF16)

    for h, r in rotary(proj(_OFF_RQ, RET_WIDTH), None):
        rq_ref[:, h * RET_HEAD_DIM:(h + 1) * RET_HEAD_DIM] = r
    for h, r in rotary(proj(_OFF_RK, RET_WIDTH), RET_HEAD_DIM ** -0.5):
        rk_ref[:, h * RET_HEAD_DIM:(h + 1) * RET_HEAD_DIM] = r


def _inproj(x2, w_packed, wt_packed, bf_pad, cs, sn, seq):
    n = x2.shape[0]
    tm = TOKEN_TILE
    pos_blocks = seq // tm
    row = lambda i: (i, 0)
    col = lambda i: (0, i)
    const = lambda i: (0, 0)
    pos = lambda i: (i % pos_blocks, 0)
    wide = lambda w, dt: jax.ShapeDtypeStruct((n, w), dt)
    tall = lambda r: jax.ShapeDtypeStruct((r, n), BF16)
    return pl.pallas_call(
        _inproj_kernel,
        grid=(n // tm,),
        in_specs=[pl.BlockSpec((tm, D_MODEL), row),
                  pl.BlockSpec((D_MODEL, PACKED_COLS), const),
                  pl.BlockSpec((PACKED_ROWS, D_MODEL), const),
                  pl.BlockSpec((1, LANES), const),
                  pl.BlockSpec((tm, LANES), pos),
                  pl.BlockSpec((tm, LANES), pos)],
        out_specs=[pl.BlockSpec((FOX_WIDTH, tm), col),
                   pl.BlockSpec((tm, FOX_WIDTH), row),
                   pl.BlockSpec((FOX_HEADS * V_ROWS, tm), col),
                   pl.BlockSpec((tm, LANES), row),
                   pl.BlockSpec((tm, LANES), row)]
                  + [pl.BlockSpec((tm, RET_WIDTH), row)] * 4,
        out_shape=[tall(FOX_WIDTH), wide(FOX_WIDTH, BF16), tall(FOX_HEADS * V_ROWS), wide(LANES, F32),
                   wide(LANES, F32)]
                  + [wide(RET_WIDTH, BF16)] * 3 + [wide(RET_WIDTH, F32)],
        compiler_params=_params("parallel"),
        name="in_proj",
    )(x2, w_packed, wt_packed, bf_pad, cs, sn)


def _split3(t):
    hi = t.astype(BF16)
    r1 = t - hi.astype(F32)
    mid = r1.astype(BF16)
    lo = (r1 - mid.astype(F32)).astype(BF16)
    return hi, mid, lo


def _dot3(pieces, m):
    return sum(jnp.dot(p, m, preferred_element_type=F32) for p in pieces)


def _cumsum_kernel(x_ref, o_ref):
    heads, rows, lanes = x_ref.shape[1:]
    j = lax.broadcasted_iota(jnp.int32, (lanes, lanes), 0)
    l = lax.broadcasted_iota(jnp.int32, (lanes, lanes), 1)
    upper = (j <= l).astype(BF16)
    ones = jnp.ones((lanes, lanes), BF16)
    r_out = lax.broadcasted_iota(jnp.int32, (rows, rows), 0)
    r_in = lax.broadcasted_iota(jnp.int32, (rows, rows), 1)
    strict = (r_in < r_out).astype(BF16)
    for h in range(heads):
        pieces = _split3(x_ref[0, h] * (-LOG2E))
        within = _dot3(pieces, upper)
        totals = _dot3(pieces, ones)
        offs = sum(jnp.dot(strict, p, preferred_element_type=F32) for p in _split3(totals))
        for t, piece in enumerate(_split3(within + offs)):
            o_ref[0, t, h] = piece


def _neg_cumsum_pieces(lf):
    b, h, s = lf.shape
    rows = s // LANES
    return pl.pallas_call(
        _cumsum_kernel,
        grid=(b,),
        in_specs=[pl.BlockSpec((1, h, rows, LANES), lambda i: (i, 0, 0, 0))],
        out_specs=pl.BlockSpec((1, BIAS_PIECES, h, rows, LANES), lambda i: (i, 0, 0, 0, 0)),
        out_shape=jax.ShapeDtypeStruct((b, BIAS_PIECES, h, rows, LANES), BF16),
        compiler_params=_params("parallel"),
        name="forget_cumsum",
    )(lf.reshape(b, h, rows, LANES)).reshape(b, BIAS_PIECES, h, s)


def _fox_kernel(nb_ref, kn_ref, qt_ref, k_ref, cb_ref, vt_ref, o_ref,
                qaug_ref, m_ref, acc_ref, s_ref, cmax_ref):
    t = ATTN_TILE
    qi = pl.program_id(2)
    qt = qt_ref[...]
    q2 = qt.astype(F32) * qt.astype(F32)
    qnorm = [jnp.sqrt(jnp.sum(q2[a * FOX_HEAD_DIM:(a + 1) * FOX_HEAD_DIM], axis=0, keepdims=True))
             for a in range(2)]
    sub = lax.broadcasted_iota(jnp.int32, (LANES, t), 0)
    for a in range(2):
        mine = (sub >= a * FOX_HEAD_DIM) & (sub < (a + 1) * FOX_HEAD_DIM)
        qaug_ref[a, 0:LANES, :] = jnp.where(mine, qt, jnp.zeros_like(qt))
        pick = (sub >= a * BIAS_PIECES) & (sub < (a + 1) * BIAS_PIECES)
        qaug_ref[a, LANES:MXU_DEPTH, :] = jnp.where(pick, 1.0, 0.0).astype(BF16)
    m_ref[...] = jnp.full(m_ref.shape, -jnp.inf, F32)
    acc_ref[...] = jnp.zeros(acc_ref.shape, F32)

    def scores(ki, masked):
        start = pl.multiple_of(ki * t, t)
        kaug = jnp.concatenate([k_ref[0, pl.ds(start, t), :], cb_ref[0, 0, pl.ds(start, t), :]], axis=1)
        if masked:
            key = lax.broadcasted_iota(jnp.int32, (t, t), 0) + ki * t
            qry = lax.broadcasted_iota(jnp.int32, (t, t), 1) + qi * t
            keep = key <= qry
        for a in range(2):
            s = jnp.dot(kaug, qaug_ref[a], preferred_element_type=F32)
            if masked:
                s = jnp.where(keep, s, -jnp.inf)
            s_ref[a] = s
            cmax_ref[a] = jnp.max(s, axis=0, keepdims=True)

    def weights():
        out = []
        for a in range(2):
            m_prev = m_ref[a]
            m_new = jnp.maximum(m_prev, cmax_ref[a])
            out.append((jnp.exp2(s_ref[a] - m_new).astype(BF16), jnp.exp2(m_prev - m_new)))
            m_ref[a] = m_new
        return out

    def values(ki, pw):
        start = pl.multiple_of(ki * t, t)
        for a, (p, alpha) in enumerate(pw):
            vt = vt_ref[a * V_ROWS:(a + 1) * V_ROWS, pl.ds(start, t)]
            acc_ref[a] = alpha * acc_ref[a] + jnp.dot(vt, p, preferred_element_type=F32)

    def tiles_needed():
        need = None
        for a in range(2):
            bound = nb_ref[a] + qnorm[a] * kn_ref[a] - cmax_ref[a]
            hit = jnp.max(bound, axis=1, keepdims=True) >= -ZERO_WEIGHT_LOG2
            need = hit if need is None else need | hit
        older = lax.broadcasted_iota(jnp.int32, need.shape, 0) < qi
        return jnp.sum((need & older).astype(jnp.int32))

    def step(k):
        pw = weights()
        scores(k - 1, False)
        values(k, pw)

    def body(g, carry):
        for u in range(ATTN_UNROLL):
            step(qi - (g * ATTN_UNROLL + u))
        return carry

    scores(qi, True)
    n = tiles_needed()
    groups = n // ATTN_UNROLL
    lax.fori_loop(0, groups, body, 0)
    for u in range(ATTN_UNROLL - 1):
        @pl.when(groups * ATTN_UNROLL + u < n)
        def _():
            step(qi - (groups * ATTN_UNROLL + u))
    values(qi - n, weights())

    outs = [acc_ref[a, 0:FOX_HEAD_DIM, :] / acc_ref[a, FOX_HEAD_DIM:FOX_HEAD_DIM + 1, :] for a in range(2)]
    o_ref[0] = jnp.concatenate(outs, axis=0).T.astype(o_ref.dtype)


def _fox_attention(nb, kn, fqt, fk, cb, fvt, batch, seq):
    t = ATTN_TILE
    nq = seq // t
    bound = pl.BlockSpec((2, nq, t), lambda bi, hp, qi: (bi * FOX_PAIRS + hp, 0, 0))
    return pl.pallas_call(
        _fox_kernel,
        grid=(batch, FOX_PAIRS, nq),
        in_specs=[bound, bound,
                  pl.BlockSpec((LANES, t), lambda bi, hp, qi: (hp, bi * nq + qi)),
                  pl.BlockSpec((1, seq, LANES), lambda bi, hp, qi: (bi, 0, hp)),
                  pl.BlockSpec((1, 1, seq, LANES), lambda bi, hp, qi: (bi, hp, 0, 0)),
                  pl.BlockSpec((2 * V_ROWS, seq), lambda bi, hp, qi: (hp, bi))],
        out_specs=pl.BlockSpec((1, t, LANES), lambda bi, hp, qi: (bi, qi, hp)),
        out_shape=jax.ShapeDtypeStruct((batch, seq, FOX_WIDTH), BF16),
        scratch_shapes=[pltpu.VMEM((2, MXU_DEPTH, t), BF16),
                        pltpu.VMEM((2, 1, t), F32),
                        pltpu.VMEM((2, V_ROWS, t), F32),
                        pltpu.VMEM((2, t, t), F32),
                        pltpu.VMEM((2, 1, t), F32)],
        compiler_params=_params("parallel", "parallel", "arbitrary"),
        name="fox_attention",
    )(nb, kn, fqt, fk, cb, fvt)


def _score_bounds(pieces, kn2, batch, seq):
    nt = seq // ATTN_TILE
    negc = pieces.astype(F32).sum(axis=1)
    nb = lax.cummax(negc.reshape(batch, FOX_HEADS, nt, ATTN_TILE).max(axis=-1), axis=2) + BOUND_SLACK
    k2 = kn2[:, :FOX_HEADS].reshape(batch, nt, ATTN_TILE, FOX_HEADS).max(axis=2)
    kn = lax.cummax(jnp.sqrt(k2).transpose(0, 2, 1), axis=2) * NORM_INFLATE
    rep = lambda b: jnp.broadcast_to(b.reshape(batch * FOX_HEADS, nt, 1), (batch * FOX_HEADS, nt, ATTN_TILE))
    return rep(nb), rep(kn)


def _ret_kernel(q_ref, k_ref, v_ref, g_ref, dm_ref, xi_ref, zeta_ref, cd_ref, gn_ref,
                o_ref, state_ref):
    @pl.when(pl.program_id(1) == 0)
    def _():
        state_ref[...] = jnp.zeros(state_ref.shape, F32)

    c = RET_CHUNK
    for h in range(RET_HEADS):
        cols = slice(h * RET_HEAD_DIM, (h + 1) * RET_HEAD_DIM)
        st = state_ref[h]
        for n in range(RET_TILE // c):
            rows = slice(n * c, (n + 1) * c)
            q = q_ref[0, rows, cols]
            k = k_ref[0, rows, cols]
            v = v_ref[0, rows, cols]
            inner = lax.dot_general(q, k, _NT, preferred_element_type=F32) * dm_ref[h]
            o = (jnp.dot(inner.astype(BF16), v, preferred_element_type=F32)
                 + xi_ref[h] * jnp.dot(q, st.astype(BF16), preferred_element_type=F32))
            kz = (k.astype(F32) * zeta_ref[h]).astype(BF16)
            st = cd_ref[h] * st + lax.dot_general(kz, v, (((0,), (0,)), ((), ())),
                                                  preferred_element_type=F32)
            mu = jnp.mean(o, axis=-1, keepdims=True)
            d = o - mu
            var = jnp.mean(d * d, axis=-1, keepdims=True)
            y = d * lax.rsqrt(var + GN_EPS) * gn_ref[h:h + 1, :]
            g = g_ref[0, rows, cols]
            o_ref[0, rows, cols] = (g * jax.nn.sigmoid(g) * y).astype(o_ref.dtype)
        state_ref[h] = st


def _retention(rq, rk, rv, rg, dm, xi, zeta, cd, gn):
    b, s, _ = rq.shape
    t = RET_TILE
    tok = pl.BlockSpec((1, t, RET_WIDTH), lambda bi, si: (bi, si, 0))
    tab = pl.BlockSpec((RET_HEADS, RET_CHUNK, RET_HEAD_DIM), lambda bi, si: (0, 0, 0))
    return pl.pallas_call(
        _ret_kernel,
        grid=(b, s // t),
        in_specs=[tok, tok, tok, tok, tab, tab, tab, tab,
                  pl.BlockSpec((RET_HEADS, RET_HEAD_DIM), lambda bi, si: (0, 0))],
        out_specs=tok,
        out_shape=jax.ShapeDtypeStruct((b, s, RET_WIDTH), BF16),
        scratch_shapes=[pltpu.VMEM((RET_HEADS, RET_HEAD_DIM, RET_HEAD_DIM), F32)],
        compiler_params=_params("parallel", "arbitrary"),
        name="retention",
    )(rq, rk, rv, rg, dm, xi, zeta, cd, gn)


def _layernorm(y, g, b):
    mu = jnp.mean(y, axis=-1, keepdims=True)
    d = y - mu
    var = jnp.mean(d * d, axis=-1, keepdims=True)
    return d * lax.rsqrt(var + LN_EPS) * g + b


def _outproj_kernel(alpha, fox_ref, ret_ref, x_ref, wa_ref, wb_ref, g_ref, b_ref, o_ref):
    mix = (jnp.dot(fox_ref[...], wa_ref[...], preferred_element_type=F32)
           + jnp.dot(ret_ref[...], wb_ref[...], preferred_element_type=F32))
    o_ref[...] = _layernorm(alpha * x_ref[...] + mix, g_ref[...], b_ref[...])


def _outproj(alpha, fox, ret, x2, wa, wb, g, b):
    n = x2.shape[0]
    tm = TOKEN_TILE
    row = lambda i: (i, 0)
    const = lambda i: (0, 0)
    return pl.pallas_call(
        functools.partial(_outproj_kernel, alpha),
        grid=(n // tm,),
        in_specs=[pl.BlockSpec((tm, FOX_WIDTH), row),
                  pl.BlockSpec((tm, RET_WIDTH), row),
                  pl.BlockSpec((tm, D_MODEL), row),
                  pl.BlockSpec((FOX_WIDTH, D_MODEL), const),
                  pl.BlockSpec((RET_WIDTH, D_MODEL), const),
                  pl.BlockSpec((1, D_MODEL), const),
                  pl.BlockSpec((1, D_MODEL), const)],
        out_specs=pl.BlockSpec((tm, D_MODEL), row),
        out_shape=jax.ShapeDtypeStruct((n, D_MODEL), F32),
        compiler_params=_params("parallel"),
        name="out_proj_ln",
    )(fox, ret, x2, wa, wb, g, b)


def _ffn_kernel(alpha, x_ref, w1_ref, w2_ref, g_ref, b_ref, o_ref):
    x = x_ref[...]
    xb = x.astype(BF16)
    acc = alpha * x
    for c in range(D_FF // FF_CHUNK):
        cols = slice(c * FF_CHUNK, (c + 1) * FF_CHUNK)
        h = jnp.maximum(jnp.dot(xb, w1_ref[:, cols], preferred_element_type=F32), 0.0)
        acc = acc + jnp.dot((h * h).astype(BF16), w2_ref[cols, :], preferred_element_type=F32)
    o_ref[...] = _layernorm(acc, g_ref[...], b_ref[...])


def _ffn(alpha, x2, w1, w2, g, b):
    n = x2.shape[0]
    tm = TOKEN_TILE
    row = lambda i: (i, 0)
    const = lambda i: (0, 0)
    return pl.pallas_call(
        functools.partial(_ffn_kernel, alpha),
        grid=(n // tm,),
        in_specs=[pl.BlockSpec((tm, D_MODEL), row),
                  pl.BlockSpec((D_MODEL, D_FF), const),
                  pl.BlockSpec((D_FF, D_MODEL), const),
                  pl.BlockSpec((1, D_MODEL), const),
                  pl.BlockSpec((1, D_MODEL), const)],
        out_specs=pl.BlockSpec((tm, D_MODEL), row),
        out_shape=jax.ShapeDtypeStruct((n, D_MODEL), F32),
        compiler_params=_params("parallel"),
        name="ffn_ln",
    )(x2, w1, w2, g, b)


def _rotary_tables(seq):
    half = RET_HEAD_DIM // 2
    inv_freq = ROPE_BASE ** (-jnp.arange(half, dtype=F32) / half)
    ang = jnp.arange(seq, dtype=F32)[:, None] * inv_freq[None, :]
    cos, sin = jnp.cos(ang), jnp.sin(ang)
    return jnp.concatenate([cos, cos], axis=-1), jnp.concatenate([-sin, sin], axis=-1)


def _decay_tables():
    gamma = 1.0 - jnp.exp2(-5.0 - jnp.arange(RET_HEADS, dtype=F32))
    log_g = jnp.log(gamma)
    idx = jnp.arange(RET_CHUNK, dtype=F32)
    diff = idx[:, None] - idx[None, :]
    dm = jnp.where(diff >= 0, jnp.exp(log_g[:, None, None] * jnp.maximum(diff, 0.0)), 0.0)
    shape = (RET_HEADS, RET_CHUNK, RET_HEAD_DIM)
    xi = jnp.broadcast_to(jnp.exp(log_g[:, None] * (idx + 1.0))[..., None], shape)
    zeta = jnp.broadcast_to(jnp.exp(log_g[:, None] * (RET_CHUNK - 1.0 - idx))[..., None], shape)
    cd = jnp.broadcast_to(jnp.exp(log_g * RET_CHUNK)[:, None, None], shape)
    return dm, xi, zeta, cd


def _pack_w_in(w):
    d = w.shape[0]
    wq, wk, wv = (w[:, i * FOX_WIDTH:(i + 1) * FOX_WIDTH] for i in range(3))
    lo = 3 * FOX_WIDTH
    logits = w[:, lo:lo + FOX_HEADS]
    pad = jnp.zeros((d, LANES - FOX_HEADS), w.dtype)
    packed = jnp.concatenate([wk, w[:, lo + FOX_HEADS:], logits, pad], axis=1).astype(BF16)
    wv_t = wv.T.reshape(FOX_HEADS, FOX_HEAD_DIM, d)
    wv_t = jnp.pad(wv_t, ((0, 0), (0, V_ROWS - FOX_HEAD_DIM), (0, 0))).reshape(FOX_HEADS * V_ROWS, d)
    packed_t = jnp.concatenate([wq.T, wv_t], axis=0).astype(BF16)
    return packed, packed_t


def _bias_operand(pieces, batch, seq):
    p = pieces.reshape(batch, BIAS_PIECES, FOX_PAIRS, 2, seq)
    p = p.transpose(0, 2, 4, 3, 1).reshape(batch, FOX_PAIRS, seq, 2 * BIAS_PIECES)
    return jnp.pad(p, ((0, 0), (0, 0), (0, 0), (0, LANES - 2 * BIAS_PIECES)))


@jax.jit
def kernel(x, w_in, w_out, w_ff1, w_ff2, ln1_g, ln1_b, ln2_g, ln2_b, b_forget, ret_gn_g):
    batch, seq, _ = x.shape
    depth = w_in.shape[0]
    alpha = (2 * depth) ** 0.25
    cs, sn = _rotary_tables(seq)
    dm, xi, zeta, cd = _decay_tables()
    x2 = x.reshape(batch * seq, D_MODEL)
    for layer in range(depth):
        bf_pad = jnp.pad(b_forget[layer], (0, LANES - FOX_HEADS)).reshape(1, LANES)
        w_packed, wt_packed = _pack_w_in(w_in[layer])
        fqt, fk, fvt, lf, kn2, rq, rk, rv, rg = _inproj(x2, w_packed, wt_packed, bf_pad, cs, sn, seq)
        seq3 = lambda t: t.reshape(batch, seq, t.shape[-1])
        lf = seq3(lf)[:, :, :FOX_HEADS].transpose(0, 2, 1)
        pieces = _neg_cumsum_pieces(lf)
        nb, kn = _score_bounds(pieces, kn2, batch, seq)
        fox = _fox_attention(nb, kn, fqt, seq3(fk), _bias_operand(pieces, batch, seq), fvt, batch, seq)
        ret = _retention(seq3(rq), seq3(rk), seq3(rv), seq3(rg), dm, xi, zeta, cd, ret_gn_g[layer])
        wo = w_out[layer].astype(BF16)
        x2 = _outproj(alpha, fox.reshape(batch * seq, FOX_WIDTH), ret.reshape(batch * seq, RET_WIDTH),
                      x2, wo[:FOX_WIDTH], wo[FOX_WIDTH:],
                      ln1_g[layer].reshape(1, D_MODEL), ln1_b[layer].reshape(1, D_MODEL))
        x2 = _ffn(alpha, x2, w_ff1[layer].astype(BF16), w_ff2[layer].astype(BF16),
                  ln2_g[layer].reshape(1, D_MODEL), ln2_b[layer].reshape(1, D_MODEL))
    return x2.reshape(batch, seq, D_MODEL)
```

```python
import functools
import math

import jax
import jax.numpy as jnp
from jax import lax
from jax.experimental import pallas as pl
from jax.experimental.pallas import tpu as pltpu

F32 = jnp.float32
BF16 = jnp.bfloat16

D_MODEL = 1024
FOX_HEADS = 8
FOX_HEAD_DIM = 64
FOX_WIDTH = FOX_HEADS * FOX_HEAD_DIM
RET_HEADS = 4
RET_HEAD_DIM = 128
RET_WIDTH = RET_HEADS * RET_HEAD_DIM
D_FF = 4 * D_MODEL
RET_CHUNK = 128
ROPE_BASE = 10000.0
LN_EPS = 1e-5
GN_EPS = 1e-5
LOG2E = math.log2(math.e)

LANES = 128
MXU_DEPTH = 256
VMEM_LIMIT = 56 * 1024 * 1024

TOKEN_TILE = 512
ATTN_TILE = 512
ATTN_UNROLL = 4
ZERO_WEIGHT_LOG2 = 135.0
BOUND_SLACK = 2.0
NORM_INFLATE = 1.01
RET_TILE = 512
FF_CHUNK = 1024

FOX_PAIRS = FOX_HEADS // 2
V_ROWS = LANES
BIAS_PIECES = 3

_OFF_FK = 0
_OFF_RQ = FOX_WIDTH
_OFF_RK, _OFF_RV, _OFF_RG = _OFF_RQ + RET_WIDTH, _OFF_RQ + 2 * RET_WIDTH, _OFF_RQ + 3 * RET_WIDTH
_OFF_FL = _OFF_RQ + 4 * RET_WIDTH
PACKED_COLS = _OFF_FL + LANES
_ROW_FV = FOX_WIDTH
PACKED_ROWS = FOX_WIDTH + FOX_HEADS * V_ROWS

_NT = (((1,), (1,)), ((), ()))


def _params(*semantics):
    return pltpu.CompilerParams(dimension_semantics=semantics, vmem_limit_bytes=VMEM_LIMIT)


def _inproj_kernel(x_ref, w_ref, wt_ref, bf_ref, cs_ref, sn_ref,
                   fqt_ref, fk_ref, fvt_ref, lf_ref, kn2_ref, rq_ref, rk_ref, rv_ref, rg_ref):
    xb = x_ref[...].astype(BF16)

    def proj(lo, width):
        return jnp.dot(xb, w_ref[:, lo:lo + width], preferred_element_type=F32)

    def proj_t(lo, rows):
        return lax.dot_general(wt_ref[lo:lo + rows, :], xb, _NT, preferred_element_type=F32)

    fqt_ref[...] = (proj_t(0, FOX_WIDTH) * (FOX_HEAD_DIM ** -0.5 * LOG2E)).astype(BF16)
    kb = proj(_OFF_FK, FOX_WIDTH).astype(BF16)
    fk_ref[...] = kb
    kf = kb.astype(F32)
    col = lax.broadcasted_iota(jnp.int32, (FOX_WIDTH, LANES), 0)
    lane = lax.broadcasted_iota(jnp.int32, (FOX_WIDTH, LANES), 1)
    sel = (col // FOX_HEAD_DIM == lane).astype(BF16)
    kn2_ref[...] = jnp.dot((kf * kf).astype(BF16), sel, preferred_element_type=F32)
    vt = proj_t(_ROW_FV, FOX_HEADS * V_ROWS)
    row = lax.broadcasted_iota(jnp.int32, vt.shape, 0)
    fvt_ref[...] = jnp.where(row % V_ROWS == FOX_HEAD_DIM, 1.0, vt).astype(BF16)
    rv_ref[...] = proj(_OFF_RV, RET_WIDTH).astype(BF16)
    rg_ref[...] = proj(_OFF_RG, RET_WIDTH)

    z = proj(_OFF_FL, LANES) + bf_ref[...]
    lf_ref[...] = -(jnp.maximum(-z, 0.0) + jnp.log1p(jnp.exp(-jnp.abs(z))))

    cs = cs_ref[...]
    sn = sn_ref[...]

    def rotary(t, scale):
        for h in range(RET_HEADS):
            th = t[:, h * RET_HEAD_DIM:(h + 1) * RET_HEAD_DIM]
            r = th * cs + pltpu.roll(th, RET_HEAD_DIM // 2, 1) * sn
            if scale is not None:
                r = r * scale
            yield h, r.astype(BF16)

    for h, r in rotary(proj(_OFF_RQ, RET_WIDTH), None):
        rq_ref[:, h * RET_HEAD_DIM:(h + 1) * RET_HEAD_DIM] = r
    for h, r in rotary(proj(_OFF_RK, RET_WIDTH), RET_HEAD_DIM ** -0.5):
        rk_ref[:, h * RET_HEAD_DIM:(h + 1) * RET_HEAD_DIM] = r


def _inproj(x2, w_packed, wt_packed, bf_pad, cs, sn, seq):
    n = x2.shape[0]
    tm = TOKEN_TILE
    pos_blocks = seq // tm
    row = lambda i: (i, 0)
    col = lambda i: (0, i)
    const = lambda i: (0, 0)
    pos = lambda i: (i % pos_blocks, 0)
    wide = lambda w, dt: jax.ShapeDtypeStruct((n, w), dt)
    tall = lambda r: jax.ShapeDtypeStruct((r, n), BF16)
    return pl.pallas_call(
        _inproj_kernel,
        grid=(n // tm,),
        in_specs=[pl.BlockSpec((tm, D_MODEL), row),
                  pl.BlockSpec((D_MODEL, PACKED_COLS), const),
                  pl.BlockSpec((PACKED_ROWS, D_MODEL), const),
                  pl.BlockSpec((1, LANES), const),
                  pl.BlockSpec((tm, LANES), pos),
                  pl.BlockSpec((tm, LANES), pos)],
        out_specs=[pl.BlockSpec((FOX_WIDTH, tm), col),
                   pl.BlockSpec((tm, FOX_WIDTH), row),
                   pl.BlockSpec((FOX_HEADS * V_ROWS, tm), col),
                   pl.BlockSpec((tm, LANES), row),
                   pl.BlockSpec((tm, LANES), row)]
                  + [pl.BlockSpec((tm, RET_WIDTH), row)] * 4,
        out_shape=[tall(FOX_WIDTH), wide(FOX_WIDTH, BF16), tall(FOX_HEADS * V_ROWS), wide(LANES, F32),
                   wide(LANES, F32)]
                  + [wide(RET_WIDTH, BF16)] * 3 + [wide(RET_WIDTH, F32)],
        compiler_params=_params("parallel"),
        name="in_proj",
    )(x2, w_packed, wt_packed, bf_pad, cs, sn)


def _split3(t):
    hi = t.astype(BF16)
    r1 = t - hi.astype(F32)
    mid = r1.astype(BF16)
    lo = (r1 - mid.astype(F32)).astype(BF16)
    return hi, mid, lo


def _dot3(pieces, m):
    return sum(jnp.dot(p, m, preferred_element_type=F32) for p in pieces)


def _cumsum_kernel(x_ref, o_ref):
    heads, rows, lanes = x_ref.shape[1:]
    j = lax.broadcasted_iota(jnp.int32, (lanes, lanes), 0)
    l = lax.broadcasted_iota(jnp.int32, (lanes, lanes), 1)
    upper = (j <= l).astype(BF16)
    ones = jnp.ones((lanes, lanes), BF16)
    r_out = lax.broadcasted_iota(jnp.int32, (rows, rows), 0)
    r_in = lax.broadcasted_iota(jnp.int32, (rows, rows), 1)
    strict = (r_in < r_out).astype(BF16)
    for h in range(heads):
        pieces = _split3(x_ref[0, h] * (-LOG2E))
        within = _dot3(pieces, upper)
        totals = _dot3(pieces, ones)
        offs = sum(jnp.dot(strict, p, preferred_element_type=F32) for p in _split3(totals))
        for t, piece in enumerate(_split3(within + offs)):
            o_ref[0, t, h] = piece


def _neg_cumsum_pieces(lf):
    b, h, s = lf.shape
    rows = s // LANES
    return pl.pallas_call(
        _cumsum_kernel,
        grid=(b,),
        in_specs=[pl.BlockSpec((1, h, rows, LANES), lambda i: (i, 0, 0, 0))],
        out_specs=pl.BlockSpec((1, BIAS_PIECES, h, rows, LANES), lambda i: (i, 0, 0, 0, 0)),
        out_shape=jax.ShapeDtypeStruct((b, BIAS_PIECES, h, rows, LANES), BF16),
        compiler_params=_params("parallel"),
        name="forget_cumsum",
    )(lf.reshape(b, h, rows, LANES)).reshape(b, BIAS_PIECES, h, s)


def _fox_kernel(nb_ref, kn_ref, qt_ref, k_ref, cb_ref, vt_ref, o_ref,
                qaug_ref, m_ref, acc_ref, s_ref, cmax_ref):
    t = ATTN_TILE
    qi = pl.program_id(2)
    qt = qt_ref[...]
    q2 = qt.astype(F32) * qt.astype(F32)
    qnorm = [jnp.sqrt(jnp.sum(q2[a * FOX_HEAD_DIM:(a + 1) * FOX_HEAD_DIM], axis=0, keepdims=True))
             for a in range(2)]
    sub = lax.broadcasted_iota(jnp.int32, (LANES, t), 0)
    for a in range(2):
        mine = (sub >= a * FOX_HEAD_DIM) & (sub < (a + 1) * FOX_HEAD_DIM)
        qaug_ref[a, 0:LANES, :] = jnp.where(mine, qt, jnp.zeros_like(qt))
        pick = (sub >= a * BIAS_PIECES) & (sub < (a + 1) * BIAS_PIECES)
        qaug_ref[a, LANES:MXU_DEPTH, :] = jnp.where(pick, 1.0, 0.0).astype(BF16)
    m_ref[...] = jnp.full(m_ref.shape, -jnp.inf, F32)
    acc_ref[...] = jnp.zeros(acc_ref.shape, F32)

    def scores(ki, masked):
        start = pl.multiple_of(ki * t, t)
        kaug = jnp.concatenate([k_ref[0, pl.ds(start, t), :], cb_ref[0, 0, pl.ds(start, t), :]], axis=1)
        if masked:
            key = lax.broadcasted_iota(jnp.int32, (t, t), 0) + ki * t
            qry = lax.broadcasted_iota(jnp.int32, (t, t), 1) + qi * t
            keep = key <= qry
        for a in range(2):
            s = jnp.dot(kaug, qaug_ref[a], preferred_element_type=F32)
            if masked:
                s = jnp.where(keep, s, -jnp.inf)
            s_ref[a] = s
            cmax_ref[a] = jnp.max(s, axis=0, keepdims=True)

    def weights():
        out = []
        for a in range(2):
            m_prev = m_ref[a]
            m_new = jnp.maximum(m_prev, cmax_ref[a])
            out.append((jnp.exp2(s_ref[a] - m_new).astype(BF16), jnp.exp2(m_prev - m_new)))
            m_ref[a] = m_new
        return out

    def values(ki, pw):
        start = pl.multiple_of(ki * t, t)
        for a, (p, alpha) in enumerate(pw):
            vt = vt_ref[a * V_ROWS:(a + 1) * V_ROWS, pl.ds(start, t)]
            acc_ref[a] = alpha * acc_ref[a] + jnp.dot(vt, p, preferred_element_type=F32)

    def tiles_needed():
        need = None
        for a in range(2):
            bound = nb_ref[a] + qnorm[a] * kn_ref[a] - cmax_ref[a]
            hit = jnp.max(bound, axis=1, keepdims=True) >= -ZERO_WEIGHT_LOG2
            need = hit if need is None else need | hit
        older = lax.broadcasted_iota(jnp.int32, need.shape, 0) < qi
        return jnp.sum((need & older).astype(jnp.int32))

    def step(k):
        pw = weights()
        scores(k - 1, False)
        values(k, pw)

    def body(g, carry):
        for u in range(ATTN_UNROLL):
            step(qi - (g * ATTN_UNROLL + u))
        return carry

    scores(qi, True)
    n = tiles_needed()
    groups = n // ATTN_UNROLL
    lax.fori_loop(0, groups, body, 0)
    for u in range(ATTN_UNROLL - 1):
        @pl.when(groups * ATTN_UNROLL + u < n)
        def _():
            step(qi - (groups * ATTN_UNROLL + u))
    values(qi - n, weights())

    outs = [acc_ref[a, 0:FOX_HEAD_DIM, :] / acc_ref[a, FOX_HEAD_DIM:FOX_HEAD_DIM + 1, :] for a in range(2)]
    o_ref[0] = jnp.concatenate(outs, axis=0).T.astype(o_ref.dtype)


def _fox_attention(nb, kn, fqt, fk, cb, fvt, batch, seq):
    t = ATTN_TILE
    nq = seq // t
    bound = pl.BlockSpec((2, nq, t), lambda bi, hp, qi: (bi * FOX_PAIRS + hp, 0, 0))
    return pl.pallas_call(
        _fox_kernel,
        grid=(batch, FOX_PAIRS, nq),
        in_specs=[bound, bound,
                  pl.BlockSpec((LANES, t), lambda bi, hp, qi: (hp, bi * nq + qi)),
                  pl.BlockSpec((1, seq, LANES), lambda bi, hp, qi: (bi, 0, hp)),
                  pl.BlockSpec((1, 1, seq, LANES), lambda bi, hp, qi: (bi, hp, 0, 0)),
                  pl.BlockSpec((2 * V_ROWS, seq), lambda bi, hp, qi: (hp, bi))],
        out_specs=pl.BlockSpec((1, t, LANES), lambda bi, hp, qi: (bi, qi, hp)),
        out_shape=jax.ShapeDtypeStruct((batch, seq, FOX_WIDTH), BF16),
        scratch_shapes=[pltpu.VMEM((2, MXU_DEPTH, t), BF16),
                        pltpu.VMEM((2, 1, t), F32),
                        pltpu.VMEM((2, V_ROWS, t), F32),
                        pltpu.VMEM((2, t, t), F32),
                        pltpu.VMEM((2, 1, t), F32)],
        compiler_params=_params("parallel", "parallel", "arbitrary"),
        name="fox_attention",
    )(nb, kn, fqt, fk, cb, fvt)


def _score_bounds(pieces, kn2, batch, seq):
    nt = seq // ATTN_TILE
    negc = pieces.astype(F32).sum(axis=1)
    nb = lax.cummax(negc.reshape(batch, FOX_HEADS, nt, ATTN_TILE).max(axis=-1), axis=2) + BOUND_SLACK
    k2 = kn2[:, :FOX_HEADS].reshape(batch, nt, ATTN_TILE, FOX_HEADS).max(axis=2)
    kn = lax.cummax(jnp.sqrt(k2).transpose(0, 2, 1), axis=2) * NORM_INFLATE
    rep = lambda b: jnp.broadcast_to(b.reshape(batch * FOX_HEADS, nt, 1), (batch * FOX_HEADS, nt, ATTN_TILE))
    return rep(nb), rep(kn)


def _ret_kernel(q_ref, k_ref, v_ref, g_ref, dm_ref, xi_ref, zeta_ref, cd_ref, gn_ref,
                o_ref, state_ref):
    @pl.when(pl.program_id(1) == 0)
    def _():
        state_ref[...] = jnp.zeros(state_ref.shape, F32)

    c = RET_CHUNK
    blocks = [(n, h) for n in range(RET_TILE // c) for h in range(RET_HEADS)]
    rows = lambda n: slice(n * c, (n + 1) * c)
    cols = lambda h: slice(h * RET_HEAD_DIM, (h + 1) * RET_HEAD_DIM)

    inner, kv = {}, {}
    for n, h in blocks:
        q, k = q_ref[0, rows(n), cols(h)], k_ref[0, rows(n), cols(h)]
        inner[n, h] = (lax.dot_general(q, k, _NT, preferred_element_type=F32) * dm_ref[h]).astype(BF16)
    for n, h in blocks:
        kz = (k_ref[0, rows(n), cols(h)].astype(F32) * zeta_ref[h]).astype(BF16)
        kv[n, h] = lax.dot_general(kz, v_ref[0, rows(n), cols(h)], (((0,), (0,)), ((), ())),
                                   preferred_element_type=F32)
    before = {}
    for h in range(RET_HEADS):
        st = state_ref[h]
        for n in range(RET_TILE // c):
            before[n, h] = st.astype(BF16)
            st = cd_ref[h] * st + kv[n, h]
        state_ref[h] = st
    for n, h in blocks:
        q, v = q_ref[0, rows(n), cols(h)], v_ref[0, rows(n), cols(h)]
        o = (jnp.dot(inner[n, h], v, preferred_element_type=F32)
             + xi_ref[h] * jnp.dot(q, before[n, h], preferred_element_type=F32))
        mu = jnp.mean(o, axis=-1, keepdims=True)
        d = o - mu
        var = jnp.mean(d * d, axis=-1, keepdims=True)
        y = d * lax.rsqrt(var + GN_EPS) * gn_ref[h:h + 1, :]
        g = g_ref[0, rows(n), cols(h)]
        o_ref[0, rows(n), cols(h)] = (g * jax.nn.sigmoid(g) * y).astype(o_ref.dtype)


def _retention(rq, rk, rv, rg, dm, xi, zeta, cd, gn):
    b, s, _ = rq.shape
    t = RET_TILE
    tok = pl.BlockSpec((1, t, RET_WIDTH), lambda bi, si: (bi, si, 0))
    tab = pl.BlockSpec((RET_HEADS, RET_CHUNK, RET_HEAD_DIM), lambda bi, si: (0, 0, 0))
    return pl.pallas_call(
        _ret_kernel,
        grid=(b, s // t),
        in_specs=[tok, tok, tok, tok, tab, tab, tab, tab,
                  pl.BlockSpec((RET_HEADS, RET_HEAD_DIM), lambda bi, si: (0, 0))],
        out_specs=tok,
        out_shape=jax.ShapeDtypeStruct((b, s, RET_WIDTH), BF16),
        scratch_shapes=[pltpu.VMEM((RET_HEADS, RET_HEAD_DIM, RET_HEAD_DIM), F32)],
        compiler_params=_params("parallel", "arbitrary"),
        name="retention",
    )(rq, rk, rv, rg, dm, xi, zeta, cd, gn)


def _layernorm(y, g, b):
    mu = jnp.mean(y, axis=-1, keepdims=True)
    d = y - mu
    var = jnp.mean(d * d, axis=-1, keepdims=True)
    return d * lax.rsqrt(var + LN_EPS) * g + b


def _tail_kernel(alpha, fox_ref, ret_ref, x_ref, wa_ref, wb_ref, w1_ref, w2_ref,
                 g1_ref, b1_ref, g2_ref, b2_ref, o_ref):
    mix = (jnp.dot(fox_ref[...], wa_ref[...], preferred_element_type=F32)
           + jnp.dot(ret_ref[...], wb_ref[...], preferred_element_type=F32))
    x = _layernorm(alpha * x_ref[...] + mix, g1_ref[...], b1_ref[...])
    xb = x.astype(BF16)
    acc = alpha * x
    for c in range(D_FF // FF_CHUNK):
        cols = slice(c * FF_CHUNK, (c + 1) * FF_CHUNK)
        h = jnp.maximum(jnp.dot(xb, w1_ref[:, cols], preferred_element_type=F32), 0.0)
        acc = acc + jnp.dot((h * h).astype(BF16), w2_ref[cols, :], preferred_element_type=F32)
    o_ref[...] = _layernorm(acc, g2_ref[...], b2_ref[...])


def _tail(alpha, fox, ret, x2, wa, wb, w1, w2, g1, b1, g2, b2):
    n = x2.shape[0]
    tm = TOKEN_TILE
    row = lambda i: (i, 0)
    const = lambda i: (0, 0)
    resident = lambda shape: pl.BlockSpec(shape, const, pipeline_mode=pl.Buffered(1))
    vec = pl.BlockSpec((1, D_MODEL), const)
    return pl.pallas_call(
        functools.partial(_tail_kernel, alpha),
        grid=(n // tm,),
        in_specs=[pl.BlockSpec((tm, FOX_WIDTH), row),
                  pl.BlockSpec((tm, RET_WIDTH), row),
                  pl.BlockSpec((tm, D_MODEL), row),
                  resident((FOX_WIDTH, D_MODEL)),
                  resident((RET_WIDTH, D_MODEL)),
                  resident((D_MODEL, D_FF)),
                  resident((D_FF, D_MODEL)),
                  vec, vec, vec, vec],
        out_specs=pl.BlockSpec((tm, D_MODEL), row),
        out_shape=jax.ShapeDtypeStruct((n, D_MODEL), F32),
        compiler_params=_params("parallel"),
        name="out_proj_ffn",
    )(fox, ret, x2, wa, wb, w1, w2, g1, b1, g2, b2)


def _rotary_tables(seq):
    half = RET_HEAD_DIM // 2
    inv_freq = ROPE_BASE ** (-jnp.arange(half, dtype=F32) / half)
    ang = jnp.arange(seq, dtype=F32)[:, None] * inv_freq[None, :]
    cos, sin = jnp.cos(ang), jnp.sin(ang)
    return jnp.concatenate([cos, cos], axis=-1), jnp.concatenate([-sin, sin], axis=-1)


def _decay_tables():
    gamma = 1.0 - jnp.exp2(-5.0 - jnp.arange(RET_HEADS, dtype=F32))
    log_g = jnp.log(gamma)
    idx = jnp.arange(RET_CHUNK, dtype=F32)
    diff = idx[:, None] - idx[None, :]
    dm = jnp.where(diff >= 0, jnp.exp(log_g[:, None, None] * jnp.maximum(diff, 0.0)), 0.0)
    shape = (RET_HEADS, RET_CHUNK, RET_HEAD_DIM)
    xi = jnp.broadcast_to(jnp.exp(log_g[:, None] * (idx + 1.0))[..., None], shape)
    zeta = jnp.broadcast_to(jnp.exp(log_g[:, None] * (RET_CHUNK - 1.0 - idx))[..., None], shape)
    cd = jnp.broadcast_to(jnp.exp(log_g * RET_CHUNK)[:, None, None], shape)
    return dm, xi, zeta, cd


def _pack_w_in(w, order):
    d = w.shape[0]
    by_head = lambda m: m.reshape(d, FOX_HEADS, FOX_HEAD_DIM)[:, order].reshape(d, FOX_WIDTH)
    wq, wk, wv = (by_head(w[:, i * FOX_WIDTH:(i + 1) * FOX_WIDTH]) for i in range(3))
    lo = 3 * FOX_WIDTH
    logits = w[:, lo:lo + FOX_HEADS][:, order]
    pad = jnp.zeros((d, LANES - FOX_HEADS), w.dtype)
    packed = jnp.concatenate([wk, w[:, lo + FOX_HEADS:], logits, pad], axis=1).astype(BF16)
    wv_t = wv.T.reshape(FOX_HEADS, FOX_HEAD_DIM, d)
    wv_t = jnp.pad(wv_t, ((0, 0), (0, V_ROWS - FOX_HEAD_DIM), (0, 0))).reshape(FOX_HEADS * V_ROWS, d)
    packed_t = jnp.concatenate([wq.T, wv_t], axis=0).astype(BF16)
    return packed, packed_t


def _bias_operand(pieces, batch, seq):
    p = pieces.reshape(batch, BIAS_PIECES, FOX_PAIRS, 2, seq)
    p = p.transpose(0, 2, 4, 3, 1).reshape(batch, FOX_PAIRS, seq, 2 * BIAS_PIECES)
    return jnp.pad(p, ((0, 0), (0, 0), (0, 0), (0, LANES - 2 * BIAS_PIECES)))


@jax.jit
def kernel(x, w_in, w_out, w_ff1, w_ff2, ln1_g, ln1_b, ln2_g, ln2_b, b_forget, ret_gn_g):
    batch, seq, _ = x.shape
    depth = w_in.shape[0]
    alpha = (2 * depth) ** 0.25
    cs, sn = _rotary_tables(seq)
    dm, xi, zeta, cd = _decay_tables()
    x2 = x.reshape(batch * seq, D_MODEL)
    for layer in range(depth):
        order = jnp.argsort(b_forget[layer])
        bf_pad = jnp.pad(b_forget[layer][order], (0, LANES - FOX_HEADS)).reshape(1, LANES)
        w_packed, wt_packed = _pack_w_in(w_in[layer], order)
        fqt, fk, fvt, lf, kn2, rq, rk, rv, rg = _inproj(x2, w_packed, wt_packed, bf_pad, cs, sn, seq)
        seq3 = lambda t: t.reshape(batch, seq, t.shape[-1])
        lf = seq3(lf)[:, :, :FOX_HEADS].transpose(0, 2, 1)
        pieces = _neg_cumsum_pieces(lf)
        nb, kn = _score_bounds(pieces, kn2, batch, seq)
        fox = _fox_attention(nb, kn, fqt, seq3(fk), _bias_operand(pieces, batch, seq), fvt, batch, seq)
        ret = _retention(seq3(rq), seq3(rk), seq3(rv), seq3(rg), dm, xi, zeta, cd, ret_gn_g[layer])
        wo = w_out[layer].astype(BF16)
        wo_fox = wo[:FOX_WIDTH].reshape(FOX_HEADS, FOX_HEAD_DIM, D_MODEL)[order].reshape(FOX_WIDTH, D_MODEL)
        vec = lambda p: p[layer].reshape(1, D_MODEL)
        x2 = _tail(alpha, fox.reshape(batch * seq, FOX_WIDTH), ret.reshape(batch * seq, RET_WIDTH), x2,
                   wo_fox, wo[FOX_WIDTH:], w_ff1[layer].astype(BF16), w_ff2[layer].astype(BF16),
                   vec(ln1_g), vec(ln1_b), vec(ln2_g), vec(ln2_b))
    return x2.reshape(batch, seq, D_MODEL)
```

```python
import functools
import math

import jax
import jax.numpy as jnp
from jax import lax
from jax.experimental import pallas as pl
from jax.experimental.pallas import tpu as pltpu

F32 = jnp.float32
BF16 = jnp.bfloat16

D_MODEL = 1024
FOX_HEADS = 8
FOX_HEAD_DIM = 64
FOX_WIDTH = FOX_HEADS * FOX_HEAD_DIM
RET_HEADS = 4
RET_HEAD_DIM = 128
RET_WIDTH = RET_HEADS * RET_HEAD_DIM
D_FF = 4 * D_MODEL
RET_CHUNK = 128
ROPE_BASE = 10000.0
LN_EPS = 1e-5
GN_EPS = 1e-5
LOG2E = math.log2(math.e)

LANES = 128
MXU_DEPTH = 256
VMEM_LIMIT = 56 * 1024 * 1024

TOKEN_TILE = 512
ATTN_Q_TILE = 512
ATTN_K_TILE = 512
ATTN_UNROLL = 4
RUNNING_MAX_INIT = -1e30
ZERO_WEIGHT_LOG2 = 135.0
BOUND_SLACK = 2.0
NORM_INFLATE = 1.01
RET_TILE = 512
FF_CHUNK = 1024

FOX_PAIRS = FOX_HEADS // 2
V_ROWS = 80
BIAS_PIECES = 3

_OFF_FK = 0
_OFF_RQ = FOX_WIDTH
_OFF_RK, _OFF_RV, _OFF_RG = _OFF_RQ + RET_WIDTH, _OFF_RQ + 2 * RET_WIDTH, _OFF_RQ + 3 * RET_WIDTH
PACKED_COLS = _OFF_RQ + 4 * RET_WIDTH
LOGIT_ROWS = 16
_ROW_FV = FOX_WIDTH
_ROW_FL = _ROW_FV + FOX_HEADS * V_ROWS
PACKED_ROWS = _ROW_FL + LOGIT_ROWS

_NT = (((1,), (1,)), ((), ()))


def _params(*semantics):
    return pltpu.CompilerParams(dimension_semantics=semantics, vmem_limit_bytes=VMEM_LIMIT)


def _inproj_kernel(x_ref, w_ref, wt_ref, bf_ref, cs_ref, sn_ref, ones_ref,
                   fqt_ref, fk_ref, fvt_ref, lf_ref, kn2_ref, rq_ref, rk_ref, rv_ref, rg_ref):
    xb = x_ref[...].astype(BF16)

    def proj(lo, width):
        return jnp.dot(xb, w_ref[:, lo:lo + width], preferred_element_type=F32)

    def proj_t(lo, rows):
        return lax.dot_general(wt_ref[lo:lo + rows, :], xb, _NT, preferred_element_type=F32)

    fqt_ref[...] = (proj_t(0, FOX_WIDTH) * (FOX_HEAD_DIM ** -0.5 * LOG2E)).astype(BF16)
    kb = proj(_OFF_FK, FOX_WIDTH).astype(BF16)
    fk_ref[...] = kb
    kf = kb.astype(F32)
    col = lax.broadcasted_iota(jnp.int32, (FOX_WIDTH, LANES), 0)
    lane = lax.broadcasted_iota(jnp.int32, (FOX_WIDTH, LANES), 1)
    sel = (col // FOX_HEAD_DIM == lane).astype(BF16)
    kn2 = jnp.dot((kf * kf).astype(BF16), sel, preferred_element_type=F32)
    kn2_ref[0] = jnp.broadcast_to(jnp.max(kn2, axis=0, keepdims=True), kn2_ref.shape[1:])
    vt = proj_t(_ROW_FV, FOX_HEADS * V_ROWS)
    fvt_ref[...] = jnp.where(ones_ref[...] != 0.0, 1.0, vt).astype(BF16)
    rv_ref[...] = proj(_OFF_RV, RET_WIDTH).astype(BF16)
    rg_ref[...] = proj(_OFF_RG, RET_WIDTH)

    z = proj_t(_ROW_FL, LOGIT_ROWS) + bf_ref[...]
    lf_ref[...] = -(jnp.maximum(-z, 0.0) + jnp.log1p(jnp.exp(-jnp.abs(z))))

    cs = cs_ref[...]
    sn = sn_ref[...]

    def rotary(t, scale):
        for h in range(RET_HEADS):
            th = t[:, h * RET_HEAD_DIM:(h + 1) * RET_HEAD_DIM]
            r = th * cs + pltpu.roll(th, RET_HEAD_DIM // 2, 1) * sn
            if scale is not None:
                r = r * scale
            yield h, r.astype(BF16)

    for h, r in rotary(proj(_OFF_RQ, RET_WIDTH), None):
        rq_ref[:, h * RET_HEAD_DIM:(h + 1) * RET_HEAD_DIM] = r
    for h, r in rotary(proj(_OFF_RK, RET_WIDTH), RET_HEAD_DIM ** -0.5):
        rk_ref[:, h * RET_HEAD_DIM:(h + 1) * RET_HEAD_DIM] = r


def _inproj(x2, w_packed, wt_packed, b_forget, cs, sn, seq):
    n = x2.shape[0]
    tm = TOKEN_TILE
    assert tm == ATTN_K_TILE, "the key-norm output is one row per attention key tile"
    bf_rows = jnp.broadcast_to(jnp.pad(b_forget, (0, LOGIT_ROWS - FOX_HEADS))[:, None], (LOGIT_ROWS, tm))
    pos_blocks = seq // tm
    row = lambda i: (i, 0)
    col = lambda i: (0, i)
    const = lambda i: (0, 0)
    pos = lambda i: (i % pos_blocks, 0)
    wide = lambda w, dt: jax.ShapeDtypeStruct((n, w), dt)
    tall = lambda r: jax.ShapeDtypeStruct((r, n), BF16)
    return pl.pallas_call(
        _inproj_kernel,
        grid=(n // tm,),
        in_specs=[pl.BlockSpec((tm, D_MODEL), row),
                  pl.BlockSpec((D_MODEL, PACKED_COLS), const),
                  pl.BlockSpec((PACKED_ROWS, D_MODEL), const),
                  pl.BlockSpec((LOGIT_ROWS, tm), const),
                  pl.BlockSpec((tm, LANES), pos),
                  pl.BlockSpec((tm, LANES), pos),
                  pl.BlockSpec((FOX_HEADS * V_ROWS, tm), const)],
        out_specs=[pl.BlockSpec((FOX_WIDTH, tm), col),
                   pl.BlockSpec((tm, FOX_WIDTH), row),
                   pl.BlockSpec((FOX_HEADS * V_ROWS, tm), col),
                   pl.BlockSpec((LOGIT_ROWS, tm), col),
                   pl.BlockSpec((1, 8, LANES), lambda i: (i, 0, 0))]
                  + [pl.BlockSpec((tm, RET_WIDTH), row)] * 4,
        out_shape=[tall(FOX_WIDTH), wide(FOX_WIDTH, BF16), tall(FOX_HEADS * V_ROWS),
                   jax.ShapeDtypeStruct((LOGIT_ROWS, n), F32),
                   jax.ShapeDtypeStruct((n // tm, 8, LANES), F32)]
                  + [wide(RET_WIDTH, BF16)] * 3 + [wide(RET_WIDTH, F32)],
        compiler_params=_params("parallel"),
        name="in_proj",
    )(x2, w_packed, wt_packed, bf_rows, cs, sn, _ones_rows(tm))


def _ones_rows(width):
    row = jnp.arange(FOX_HEADS * V_ROWS) % V_ROWS
    return jnp.broadcast_to((row == FOX_HEAD_DIM).astype(F32)[:, None], (FOX_HEADS * V_ROWS, width))


def _split3(t):
    hi = t.astype(BF16)
    r1 = t - hi.astype(F32)
    mid = r1.astype(BF16)
    lo = (r1 - mid.astype(F32)).astype(BF16)
    return hi, mid, lo


def _dot3(pieces, m):
    return sum(jnp.dot(p, m, preferred_element_type=F32) for p in pieces)


def _cumsum_kernel(x_ref, o_ref):
    heads, rows, lanes = x_ref.shape[1:]
    j = lax.broadcasted_iota(jnp.int32, (lanes, lanes), 0)
    l = lax.broadcasted_iota(jnp.int32, (lanes, lanes), 1)
    upper = (j <= l).astype(BF16)
    ones = jnp.ones((lanes, lanes), BF16)
    r_out = lax.broadcasted_iota(jnp.int32, (rows, rows), 0)
    r_in = lax.broadcasted_iota(jnp.int32, (rows, rows), 1)
    strict = (r_in < r_out).astype(BF16)
    for h in range(heads):
        pieces = _split3(x_ref[0, h] * (-LOG2E))
        within = _dot3(pieces, upper)
        totals = _dot3(pieces, ones)
        offs = sum(jnp.dot(strict, p, preferred_element_type=F32) for p in _split3(totals))
        for t, piece in enumerate(_split3(within + offs)):
            o_ref[0, t, h] = piece


def _neg_cumsum_pieces(lf):
    b, h, s = lf.shape
    rows = s // LANES
    return pl.pallas_call(
        _cumsum_kernel,
        grid=(b,),
        in_specs=[pl.BlockSpec((1, h, rows, LANES), lambda i: (i, 0, 0, 0))],
        out_specs=pl.BlockSpec((1, BIAS_PIECES, h, rows, LANES), lambda i: (i, 0, 0, 0, 0)),
        out_shape=jax.ShapeDtypeStruct((b, BIAS_PIECES, h, rows, LANES), BF16),
        compiler_params=_params("parallel"),
        name="forget_cumsum",
    )(lf.reshape(b, h, rows, LANES)).reshape(b, BIAS_PIECES, h, s)


def _fox_kernel(nb_ref, kn_ref, qt_ref, k_ref, cb_ref, vt_ref, o_ref,
                qaug_ref, m_ref, acc_ref, s_ref, cmax_ref):
    tq, tk = ATTN_Q_TILE, ATTN_K_TILE
    diag = tq // tk
    qi = pl.program_id(2)
    qt = qt_ref[...]
    q2 = qt.astype(F32) * qt.astype(F32)
    qnorm = [jnp.sqrt(jnp.sum(q2[a * FOX_HEAD_DIM:(a + 1) * FOX_HEAD_DIM], axis=0, keepdims=True))
             for a in range(2)]
    sub = lax.broadcasted_iota(jnp.int32, (LANES, tq), 0)
    for a in range(2):
        mine = (sub >= a * FOX_HEAD_DIM) & (sub < (a + 1) * FOX_HEAD_DIM)
        qaug_ref[a, 0:LANES, :] = jnp.where(mine, qt, jnp.zeros_like(qt))
        pick = (sub >= a * BIAS_PIECES) & (sub < (a + 1) * BIAS_PIECES)
        qaug_ref[a, LANES:MXU_DEPTH, :] = jnp.where(pick, 1.0, 0.0).astype(BF16)
    m_ref[...] = jnp.full(m_ref.shape, RUNNING_MAX_INIT, F32)
    acc_ref[...] = jnp.zeros(acc_ref.shape, F32)

    def scores(ki, masked):
        start = pl.multiple_of(ki * tk, tk)
        kaug = jnp.concatenate([k_ref[0, pl.ds(start, tk), :], cb_ref[0, 0, pl.ds(start, tk), :]], axis=1)
        if masked:
            key = lax.broadcasted_iota(jnp.int32, (tk, tq), 0) + ki * tk
            qry = lax.broadcasted_iota(jnp.int32, (tk, tq), 1) + qi * tq
            keep = key <= qry
        for a in range(2):
            s = jnp.dot(kaug, qaug_ref[a], preferred_element_type=F32)
            if masked:
                s = jnp.where(keep, s, -jnp.inf)
            s_ref[a] = s
            cmax_ref[a] = jnp.max(s, axis=0, keepdims=True)

    def weights():
        out = []
        for a in range(2):
            m_prev = m_ref[a]
            m_new = jnp.maximum(m_prev, cmax_ref[a])
            out.append((jnp.exp2(s_ref[a] - m_new).astype(BF16), jnp.exp2(m_prev - m_new)))
            m_ref[a] = m_new
        return out

    def values(ki, pw):
        start = pl.multiple_of(ki * tk, tk)
        for a, (p, alpha) in enumerate(pw):
            vt = vt_ref[a * V_ROWS:(a + 1) * V_ROWS, pl.ds(start, tk)]
            acc_ref[a] = alpha * acc_ref[a] + jnp.dot(vt, p, preferred_element_type=F32)

    def tiles_needed():
        need = None
        for a in range(2):
            seen = jnp.maximum(m_ref[a], cmax_ref[a])
            bound = nb_ref[a] + qnorm[a] * kn_ref[a] - seen
            hit = jnp.max(bound, axis=1, keepdims=True) >= -ZERO_WEIGHT_LOG2
            need = hit if need is None else need | hit
        older = lax.broadcasted_iota(jnp.int32, need.shape, 0) < first
        return jnp.sum((need & older).astype(jnp.int32))

    def step(k, masked=False):
        pw = weights()
        scores(k - 1, masked)
        values(k, pw)

    def body(g, carry):
        for u in range(ATTN_UNROLL):
            step(first - (g * ATTN_UNROLL + u))
        return carry

    first = qi * diag
    scores(first + diag - 1, True)
    for j in range(diag - 1):
        step(first + diag - 1 - j, True)
    n = tiles_needed()
    groups = n // ATTN_UNROLL
    lax.fori_loop(0, groups, body, 0)
    for u in range(ATTN_UNROLL - 1):
        @pl.when(groups * ATTN_UNROLL + u < n)
        def _():
            step(first - (groups * ATTN_UNROLL + u))
    values(first - n, weights())

    outs = [acc_ref[a, 0:FOX_HEAD_DIM, :] / acc_ref[a, FOX_HEAD_DIM:FOX_HEAD_DIM + 1, :] for a in range(2)]
    o_ref[0] = jnp.concatenate(outs, axis=0).T.astype(o_ref.dtype)


def _fox_attention(nb, kn, fqt, fk, cb, fvt, batch, seq):
    tq, tk = ATTN_Q_TILE, ATTN_K_TILE
    nq = seq // tq
    bound = pl.BlockSpec((2, seq // tk, tq), lambda bi, hp, qi: (bi * FOX_PAIRS + hp, 0, 0))
    return pl.pallas_call(
        _fox_kernel,
        grid=(batch, FOX_PAIRS, nq),
        in_specs=[bound, bound,
                  pl.BlockSpec((LANES, tq), lambda bi, hp, qi: (hp, bi * nq + qi)),
                  pl.BlockSpec((1, seq, LANES), lambda bi, hp, qi: (bi, 0, hp)),
                  pl.BlockSpec((1, 1, seq, LANES), lambda bi, hp, qi: (bi, hp, 0, 0)),
                  pl.BlockSpec((2 * V_ROWS, seq), lambda bi, hp, qi: (hp, bi))],
        out_specs=pl.BlockSpec((1, tq, LANES), lambda bi, hp, qi: (bi, qi, hp)),
        out_shape=jax.ShapeDtypeStruct((batch, seq, FOX_WIDTH), BF16),
        scratch_shapes=[pltpu.VMEM((2, MXU_DEPTH, tq), BF16),
                        pltpu.VMEM((2, 1, tq), F32),
                        pltpu.VMEM((2, V_ROWS, tq), F32),
                        pltpu.VMEM((2, tk, tq), F32),
                        pltpu.VMEM((2, 1, tq), F32)],
        compiler_params=_params("parallel", "parallel", "arbitrary"),
        name="fox_attention",
    )(nb, kn, fqt, fk, cb, fvt)


def _score_bounds(pieces, kn2, batch, seq):
    nt = seq // ATTN_K_TILE
    negc = pieces.astype(F32).sum(axis=1)
    nb = lax.cummax(negc.reshape(batch, FOX_HEADS, nt, ATTN_K_TILE).max(axis=-1), axis=2) + BOUND_SLACK
    k2 = kn2[:, 0, :FOX_HEADS].reshape(batch, nt, FOX_HEADS)
    kn = lax.cummax(jnp.sqrt(k2).transpose(0, 2, 1), axis=2) * NORM_INFLATE
    rep = lambda b: jnp.broadcast_to(b.reshape(batch * FOX_HEADS, nt, 1), (batch * FOX_HEADS, nt, ATTN_Q_TILE))
    return rep(nb), rep(kn)


def _ret_kernel(q_ref, k_ref, v_ref, g_ref, dm_ref, xi_ref, zeta_ref, cd_ref, gn_ref,
                o_ref, state_ref):
    @pl.when(pl.program_id(1) == 0)
    def _():
        state_ref[...] = jnp.zeros(state_ref.shape, F32)

    c = RET_CHUNK
    blocks = [(n, h) for n in range(RET_TILE // c) for h in range(RET_HEADS)]
    rows = lambda n: slice(n * c, (n + 1) * c)
    cols = lambda h: slice(h * RET_HEAD_DIM, (h + 1) * RET_HEAD_DIM)

    inner, kv = {}, {}
    for n, h in blocks:
        q, k = q_ref[0, rows(n), cols(h)], k_ref[0, rows(n), cols(h)]
        inner[n, h] = (lax.dot_general(q, k, _NT, preferred_element_type=F32) * dm_ref[h]).astype(BF16)
    for n, h in blocks:
        kz = (k_ref[0, rows(n), cols(h)].astype(F32) * zeta_ref[h]).astype(BF16)
        kv[n, h] = lax.dot_general(kz, v_ref[0, rows(n), cols(h)], (((0,), (0,)), ((), ())),
                                   preferred_element_type=F32)
    before = {}
    for h in range(RET_HEADS):
        st = state_ref[h]
        for n in range(RET_TILE // c):
            before[n, h] = st.astype(BF16)
            st = cd_ref[h] * st + kv[n, h]
        state_ref[h] = st
    for n, h in blocks:
        q, v = q_ref[0, rows(n), cols(h)], v_ref[0, rows(n), cols(h)]
        o = (jnp.dot(inner[n, h], v, preferred_element_type=F32)
             + xi_ref[h] * jnp.dot(q, before[n, h], preferred_element_type=F32))
        mu = jnp.mean(o, axis=-1, keepdims=True)
        d = o - mu
        var = jnp.mean(d * d, axis=-1, keepdims=True)
        y = d * lax.rsqrt(var + GN_EPS) * gn_ref[h:h + 1, :]
        g = g_ref[0, rows(n), cols(h)]
        o_ref[0, rows(n), cols(h)] = (g * jax.nn.sigmoid(g) * y).astype(o_ref.dtype)


def _retention(rq, rk, rv, rg, dm, xi, zeta, cd, gn):
    b, s, _ = rq.shape
    t = RET_TILE
    tok = pl.BlockSpec((1, t, RET_WIDTH), lambda bi, si: (bi, si, 0))
    tab = pl.BlockSpec((RET_HEADS, RET_CHUNK, RET_HEAD_DIM), lambda bi, si: (0, 0, 0))
    return pl.pallas_call(
        _ret_kernel,
        grid=(b, s // t),
        in_specs=[tok, tok, tok, tok, tab, tab, tab, tab,
                  pl.BlockSpec((RET_HEADS, RET_HEAD_DIM), lambda bi, si: (0, 0))],
        out_specs=tok,
        out_shape=jax.ShapeDtypeStruct((b, s, RET_WIDTH), BF16),
        scratch_shapes=[pltpu.VMEM((RET_HEADS, RET_HEAD_DIM, RET_HEAD_DIM), F32)],
        compiler_params=_params("parallel", "arbitrary"),
        name="retention",
    )(rq, rk, rv, rg, dm, xi, zeta, cd, gn)


def _layernorm(y, g, b):
    mu = jnp.mean(y, axis=-1, keepdims=True)
    d = y - mu
    var = jnp.mean(d * d, axis=-1, keepdims=True)
    return d * lax.rsqrt(var + LN_EPS) * g + b


def _tail_kernel(alpha, fox_ref, ret_ref, x_ref, wa_ref, wb_ref, w1_ref, w2_ref,
                 g1_ref, b1_ref, g2_ref, b2_ref, o_ref):
    mix = (jnp.dot(fox_ref[...], wa_ref[...], preferred_element_type=F32)
           + jnp.dot(ret_ref[...], wb_ref[...], preferred_element_type=F32))
    x = _layernorm(alpha * x_ref[...] + mix, g1_ref[...], b1_ref[...])
    xb = x.astype(BF16)
    acc = alpha * x
    for c in range(D_FF // FF_CHUNK):
        cols = slice(c * FF_CHUNK, (c + 1) * FF_CHUNK)
        h = jnp.maximum(jnp.dot(xb, w1_ref[:, cols], preferred_element_type=F32), 0.0)
        acc = acc + jnp.dot((h * h).astype(BF16), w2_ref[cols, :], preferred_element_type=F32)
    o_ref[...] = _layernorm(acc, g2_ref[...], b2_ref[...])


def _tail(alpha, fox, ret, x2, wa, wb, w1, w2, g1, b1, g2, b2):
    n = x2.shape[0]
    tm = TOKEN_TILE
    row = lambda i: (i, 0)
    const = lambda i: (0, 0)
    resident = lambda shape: pl.BlockSpec(shape, const, pipeline_mode=pl.Buffered(1))
    vec = pl.BlockSpec((1, D_MODEL), const)
    return pl.pallas_call(
        functools.partial(_tail_kernel, alpha),
        grid=(n // tm,),
        in_specs=[pl.BlockSpec((tm, FOX_WIDTH), row),
                  pl.BlockSpec((tm, RET_WIDTH), row),
                  pl.BlockSpec((tm, D_MODEL), row),
                  resident((FOX_WIDTH, D_MODEL)),
                  resident((RET_WIDTH, D_MODEL)),
                  resident((D_MODEL, D_FF)),
                  resident((D_FF, D_MODEL)),
                  vec, vec, vec, vec],
        out_specs=pl.BlockSpec((tm, D_MODEL), row),
        out_shape=jax.ShapeDtypeStruct((n, D_MODEL), F32),
        compiler_params=_params("parallel"),
        name="out_proj_ffn",
    )(fox, ret, x2, wa, wb, w1, w2, g1, b1, g2, b2)


def _rotary_tables(seq):
    half = RET_HEAD_DIM // 2
    inv_freq = ROPE_BASE ** (-jnp.arange(half, dtype=F32) / half)
    ang = jnp.arange(seq, dtype=F32)[:, None] * inv_freq[None, :]
    cos, sin = jnp.cos(ang), jnp.sin(ang)
    return jnp.concatenate([cos, cos], axis=-1), jnp.concatenate([-sin, sin], axis=-1)


def _decay_tables():
    gamma = 1.0 - jnp.exp2(-5.0 - jnp.arange(RET_HEADS, dtype=F32))
    log_g = jnp.log(gamma)
    idx = jnp.arange(RET_CHUNK, dtype=F32)
    diff = idx[:, None] - idx[None, :]
    dm = jnp.where(diff >= 0, jnp.exp(log_g[:, None, None] * jnp.maximum(diff, 0.0)), 0.0)
    shape = (RET_HEADS, RET_CHUNK, RET_HEAD_DIM)
    xi = jnp.broadcast_to(jnp.exp(log_g[:, None] * (idx + 1.0))[..., None], shape)
    zeta = jnp.broadcast_to(jnp.exp(log_g[:, None] * (RET_CHUNK - 1.0 - idx))[..., None], shape)
    cd = jnp.broadcast_to(jnp.exp(log_g * RET_CHUNK)[:, None, None], shape)
    return dm, xi, zeta, cd


def _pack_w_in(w, order):
    d = w.shape[0]
    by_head = lambda m: m.reshape(d, FOX_HEADS, FOX_HEAD_DIM)[:, order].reshape(d, FOX_WIDTH)
    wq, wk, wv = (by_head(w[:, i * FOX_WIDTH:(i + 1) * FOX_WIDTH]) for i in range(3))
    lo = 3 * FOX_WIDTH
    logits_t = jnp.pad(w[:, lo:lo + FOX_HEADS][:, order].T, ((0, LOGIT_ROWS - FOX_HEADS), (0, 0)))
    packed = jnp.concatenate([wk, w[:, lo + FOX_HEADS:]], axis=1).astype(BF16)
    wv_t = wv.T.reshape(FOX_HEADS, FOX_HEAD_DIM, d)
    wv_t = jnp.pad(wv_t, ((0, 0), (0, V_ROWS - FOX_HEAD_DIM), (0, 0))).reshape(FOX_HEADS * V_ROWS, d)
    packed_t = jnp.concatenate([wq.T, wv_t, logits_t], axis=0).astype(BF16)
    return packed, packed_t


def _bias_operand(pieces, batch, seq):
    p = pieces.reshape(batch, BIAS_PIECES, FOX_PAIRS, 2, seq)
    p = p.transpose(0, 2, 4, 3, 1).reshape(batch, FOX_PAIRS, seq, 2 * BIAS_PIECES)
    return jnp.pad(p, ((0, 0), (0, 0), (0, 0), (0, LANES - 2 * BIAS_PIECES)))


@jax.jit
def kernel(x, w_in, w_out, w_ff1, w_ff2, ln1_g, ln1_b, ln2_g, ln2_b, b_forget, ret_gn_g):
    batch, seq, _ = x.shape
    depth = w_in.shape[0]
    alpha = (2 * depth) ** 0.25
    cs, sn = _rotary_tables(seq)
    dm, xi, zeta, cd = _decay_tables()
    x2 = x.reshape(batch * seq, D_MODEL)
    for layer in range(depth):
        order = jnp.argsort(b_forget[layer])
        w_packed, wt_packed = _pack_w_in(w_in[layer], order)
        fqt, fk, fvt, lf_t, kn2, rq, rk, rv, rg = _inproj(x2, w_packed, wt_packed, b_forget[layer][order],
                                                          cs, sn, seq)
        seq3 = lambda t: t.reshape(batch, seq, t.shape[-1])
        lf = lf_t[:FOX_HEADS].reshape(FOX_HEADS, batch, seq).transpose(1, 0, 2)
        pieces = _neg_cumsum_pieces(lf)
        nb, kn = _score_bounds(pieces, kn2, batch, seq)
        fox = _fox_attention(nb, kn, fqt, seq3(fk), _bias_operand(pieces, batch, seq), fvt, batch, seq)
        ret = _retention(seq3(rq), seq3(rk), seq3(rv), seq3(rg), dm, xi, zeta, cd, ret_gn_g[layer])
        wo = w_out[layer].astype(BF16)
        wo_fox = wo[:FOX_WIDTH].reshape(FOX_HEADS, FOX_HEAD_DIM, D_MODEL)[order].reshape(FOX_WIDTH, D_MODEL)
        vec = lambda p: p[layer].reshape(1, D_MODEL)
        x2 = _tail(alpha, fox.reshape(batch * seq, FOX_WIDTH), ret.reshape(batch * seq, RET_WIDTH), x2,
                   wo_fox, wo[FOX_WIDTH:], w_ff1[layer].astype(BF16), w_ff2[layer].astype(BF16),
                   vec(ln1_g), vec(ln1_b), vec(ln2_g), vec(ln2_b))
    return x2.reshape(batch, seq, D_MODEL)
```

```python
import functools
import math

import jax
import jax.numpy as jnp
from jax import lax
from jax.experimental import pallas as pl
from jax.experimental.pallas import tpu as pltpu

F32 = jnp.float32
BF16 = jnp.bfloat16

D_MODEL = 1024
FOX_HEADS = 8
FOX_HEAD_DIM = 64
FOX_WIDTH = FOX_HEADS * FOX_HEAD_DIM
RET_HEADS = 4
RET_HEAD_DIM = 128
RET_WIDTH = RET_HEADS * RET_HEAD_DIM
D_FF = 4 * D_MODEL
RET_CHUNK = 128
ROPE_BASE = 10000.0
LN_EPS = 1e-5
GN_EPS = 1e-5
LOG2E = math.log2(math.e)

LANES = 128
MXU_DEPTH = 256
VMEM_LIMIT = 56 * 1024 * 1024

TOKEN_TILE = 512
ATTN_Q_TILE = 512
ATTN_K_TILE = 512
ATTN_UNROLL = 4
RUNNING_MAX_INIT = -1e30
ZERO_WEIGHT_LOG2 = 135.0
BOUND_SLACK = 2.0
NORM_INFLATE = 1.01
RET_TILE = 512
FF_CHUNK = 1024

FOX_PAIRS = FOX_HEADS // 2
V_ROWS = 80
BIAS_PIECES = 3

_OFF_FK = 0
_OFF_RQ = FOX_WIDTH
_OFF_RK, _OFF_RV, _OFF_RG = _OFF_RQ + RET_WIDTH, _OFF_RQ + 2 * RET_WIDTH, _OFF_RQ + 3 * RET_WIDTH
PACKED_COLS = _OFF_RQ + 4 * RET_WIDTH
LOGIT_ROWS = 16
_ROW_FV = FOX_WIDTH
_ROW_FL = _ROW_FV + FOX_HEADS * V_ROWS
PACKED_ROWS = _ROW_FL + LOGIT_ROWS

_NT = (((1,), (1,)), ((), ()))


def _params(*semantics):
    return pltpu.CompilerParams(dimension_semantics=semantics, vmem_limit_bytes=VMEM_LIMIT)


def _inproj_kernel(x_ref, w_ref, wt_ref, bf_ref, cs_ref, sn_ref, ones_ref,
                   fqt_ref, fk_ref, fvt_ref, lf_ref, kn2_ref, rq_ref, rk_ref, rv_ref, rg_ref):
    xb = x_ref[...].astype(BF16)

    def proj(lo, width):
        return jnp.dot(xb, w_ref[:, lo:lo + width], preferred_element_type=F32)

    def proj_t(lo, rows):
        return lax.dot_general(wt_ref[lo:lo + rows, :], xb, _NT, preferred_element_type=F32)

    fqt_ref[...] = (proj_t(0, FOX_WIDTH) * (FOX_HEAD_DIM ** -0.5 * LOG2E)).astype(BF16)
    kb = proj(_OFF_FK, FOX_WIDTH).astype(BF16)
    fk_ref[...] = kb
    kf = kb.astype(F32)
    col = lax.broadcasted_iota(jnp.int32, (FOX_WIDTH, LANES), 0)
    lane = lax.broadcasted_iota(jnp.int32, (FOX_WIDTH, LANES), 1)
    sel = (col // FOX_HEAD_DIM == lane).astype(BF16)
    kn2 = jnp.dot((kf * kf).astype(BF16), sel, preferred_element_type=F32)
    kn2_ref[0] = jnp.broadcast_to(jnp.max(kn2, axis=0, keepdims=True), kn2_ref.shape[1:])
    vt = proj_t(_ROW_FV, FOX_HEADS * V_ROWS)
    fvt_ref[...] = jnp.where(ones_ref[...] != 0.0, 1.0, vt).astype(BF16)
    rv_ref[...] = proj(_OFF_RV, RET_WIDTH).astype(BF16)
    rg_ref[...] = proj(_OFF_RG, RET_WIDTH)

    z = proj_t(_ROW_FL, LOGIT_ROWS) + bf_ref[...]
    lf_ref[...] = -(jnp.maximum(-z, 0.0) + jnp.log1p(jnp.exp(-jnp.abs(z))))

    cs = cs_ref[...]
    sn = sn_ref[...]

    def rotary(t, scale):
        for h in range(RET_HEADS):
            th = t[:, h * RET_HEAD_DIM:(h + 1) * RET_HEAD_DIM]
            r = th * cs + pltpu.roll(th, RET_HEAD_DIM // 2, 1) * sn
            if scale is not None:
                r = r * scale
            yield h, r.astype(BF16)

    for h, r in rotary(proj(_OFF_RQ, RET_WIDTH), None):
        rq_ref[:, h * RET_HEAD_DIM:(h + 1) * RET_HEAD_DIM] = r
    for h, r in rotary(proj(_OFF_RK, RET_WIDTH), RET_HEAD_DIM ** -0.5):
        rk_ref[:, h * RET_HEAD_DIM:(h + 1) * RET_HEAD_DIM] = r


def _inproj(x2, w_packed, wt_packed, b_forget, cs, sn, seq):
    n = x2.shape[0]
    tm = TOKEN_TILE
    assert tm == ATTN_K_TILE, "the key-norm output is one row per attention key tile"
    bf_rows = jnp.broadcast_to(jnp.pad(b_forget, (0, LOGIT_ROWS - FOX_HEADS))[:, None], (LOGIT_ROWS, tm))
    pos_blocks = seq // tm
    row = lambda i: (i, 0)
    col = lambda i: (0, i)
    const = lambda i: (0, 0)
    pos = lambda i: (i % pos_blocks, 0)
    wide = lambda w, dt: jax.ShapeDtypeStruct((n, w), dt)
    tall = lambda r: jax.ShapeDtypeStruct((r, n), BF16)
    return pl.pallas_call(
        _inproj_kernel,
        grid=(n // tm,),
        in_specs=[pl.BlockSpec((tm, D_MODEL), row),
                  pl.BlockSpec((D_MODEL, PACKED_COLS), const),
                  pl.BlockSpec((PACKED_ROWS, D_MODEL), const),
                  pl.BlockSpec((LOGIT_ROWS, tm), const),
                  pl.BlockSpec((tm, LANES), pos),
                  pl.BlockSpec((tm, LANES), pos),
                  pl.BlockSpec((FOX_HEADS * V_ROWS, tm), const)],
        out_specs=[pl.BlockSpec((FOX_WIDTH, tm), col),
                   pl.BlockSpec((tm, FOX_WIDTH), row),
                   pl.BlockSpec((FOX_HEADS * V_ROWS, tm), col),
                   pl.BlockSpec((LOGIT_ROWS, tm), col),
                   pl.BlockSpec((1, 8, LANES), lambda i: (i, 0, 0))]
                  + [pl.BlockSpec((tm, RET_WIDTH), row)] * 4,
        out_shape=[tall(FOX_WIDTH), wide(FOX_WIDTH, BF16), tall(FOX_HEADS * V_ROWS),
                   jax.ShapeDtypeStruct((LOGIT_ROWS, n), F32),
                   jax.ShapeDtypeStruct((n // tm, 8, LANES), F32)]
                  + [wide(RET_WIDTH, BF16)] * 3 + [wide(RET_WIDTH, F32)],
        compiler_params=_params("parallel"),
        name="in_proj",
    )(x2, w_packed, wt_packed, bf_rows, cs, sn, _ones_rows(tm))


def _ones_rows(width):
    row = jnp.arange(FOX_HEADS * V_ROWS) % V_ROWS
    return jnp.broadcast_to((row == FOX_HEAD_DIM).astype(F32)[:, None], (FOX_HEADS * V_ROWS, width))


def _split3(t):
    hi = t.astype(BF16)
    r1 = t - hi.astype(F32)
    mid = r1.astype(BF16)
    lo = (r1 - mid.astype(F32)).astype(BF16)
    return hi, mid, lo


def _dot3(pieces, m):
    return sum(jnp.dot(p, m, preferred_element_type=F32) for p in pieces)


def _cumsum_kernel(x_ref, o_ref):
    heads, rows, lanes = x_ref.shape[1:]
    j = lax.broadcasted_iota(jnp.int32, (lanes, lanes), 0)
    l = lax.broadcasted_iota(jnp.int32, (lanes, lanes), 1)
    upper = (j <= l).astype(BF16)
    ones = jnp.ones((lanes, lanes), BF16)
    r_out = lax.broadcasted_iota(jnp.int32, (rows, rows), 0)
    r_in = lax.broadcasted_iota(jnp.int32, (rows, rows), 1)
    strict = (r_in < r_out).astype(BF16)
    for h in range(heads):
        pieces = _split3(x_ref[0, h] * (-LOG2E))
        within = _dot3(pieces, upper)
        totals = _dot3(pieces, ones)
        offs = sum(jnp.dot(strict, p, preferred_element_type=F32) for p in _split3(totals))
        for t, piece in enumerate(_split3(within + offs)):
            o_ref[0, t, h] = piece


def _neg_cumsum_pieces(lf):
    b, h, s = lf.shape
    rows = s // LANES
    return pl.pallas_call(
        _cumsum_kernel,
        grid=(b,),
        in_specs=[pl.BlockSpec((1, h, rows, LANES), lambda i: (i, 0, 0, 0))],
        out_specs=pl.BlockSpec((1, BIAS_PIECES, h, rows, LANES), lambda i: (i, 0, 0, 0, 0)),
        out_shape=jax.ShapeDtypeStruct((b, BIAS_PIECES, h, rows, LANES), BF16),
        compiler_params=_params("parallel"),
        name="forget_cumsum",
    )(lf.reshape(b, h, rows, LANES)).reshape(b, BIAS_PIECES, h, s)


def _fox_kernel(nb_ref, kn_ref, qt_ref, k_ref, cb_ref, vt_ref, o_ref,
                qaug_ref, m_ref, acc_ref, s_ref, cmax_ref):
    def query_tile(qi, carry):
        _fox_query_tile(qi, nb_ref, kn_ref, qt_ref, k_ref, cb_ref, vt_ref, o_ref,
                        qaug_ref, m_ref, acc_ref, s_ref, cmax_ref)
        return carry

    lax.fori_loop(0, qt_ref.shape[1] // ATTN_Q_TILE, query_tile, 0)


def _fox_query_tile(qi, nb_ref, kn_ref, qt_ref, k_ref, cb_ref, vt_ref, o_ref,
                    qaug_ref, m_ref, acc_ref, s_ref, cmax_ref):
    tq, tk = ATTN_Q_TILE, ATTN_K_TILE
    diag = tq // tk
    q_start = pl.multiple_of(qi * tq, tq)
    qt = qt_ref[:, pl.ds(q_start, tq)]
    q2 = qt.astype(F32) * qt.astype(F32)
    qnorm = [jnp.sqrt(jnp.sum(q2[a * FOX_HEAD_DIM:(a + 1) * FOX_HEAD_DIM], axis=0, keepdims=True))
             for a in range(2)]
    sub = lax.broadcasted_iota(jnp.int32, (LANES, tq), 0)
    for a in range(2):
        mine = (sub >= a * FOX_HEAD_DIM) & (sub < (a + 1) * FOX_HEAD_DIM)
        qaug_ref[a, 0:LANES, :] = jnp.where(mine, qt, jnp.zeros_like(qt))
        pick = (sub >= a * BIAS_PIECES) & (sub < (a + 1) * BIAS_PIECES)
        qaug_ref[a, LANES:MXU_DEPTH, :] = jnp.where(pick, 1.0, 0.0).astype(BF16)
    m_ref[...] = jnp.full(m_ref.shape, RUNNING_MAX_INIT, F32)
    acc_ref[...] = jnp.zeros(acc_ref.shape, F32)

    def scores(ki, masked):
        start = pl.multiple_of(ki * tk, tk)
        kaug = jnp.concatenate([k_ref[0, pl.ds(start, tk), :], cb_ref[0, 0, pl.ds(start, tk), :]], axis=1)
        if masked:
            key = lax.broadcasted_iota(jnp.int32, (tk, tq), 0) + ki * tk
            qry = lax.broadcasted_iota(jnp.int32, (tk, tq), 1) + qi * tq
            keep = key <= qry
        for a in range(2):
            s = jnp.dot(kaug, qaug_ref[a], preferred_element_type=F32)
            if masked:
                s = jnp.where(keep, s, -jnp.inf)
            s_ref[a] = s
            cmax_ref[a] = jnp.max(s, axis=0, keepdims=True)

    def weights():
        out = []
        for a in range(2):
            m_prev = m_ref[a]
            m_new = jnp.maximum(m_prev, cmax_ref[a])
            out.append((jnp.exp2(s_ref[a] - m_new).astype(BF16), jnp.exp2(m_prev - m_new)))
            m_ref[a] = m_new
        return out

    def values(ki, pw):
        start = pl.multiple_of(ki * tk, tk)
        for a, (p, alpha) in enumerate(pw):
            vt = vt_ref[a * V_ROWS:(a + 1) * V_ROWS, pl.ds(start, tk)]
            acc_ref[a] = alpha * acc_ref[a] + jnp.dot(vt, p, preferred_element_type=F32)

    def tiles_needed():
        need = None
        for a in range(2):
            seen = jnp.maximum(m_ref[a], cmax_ref[a])
            bound = nb_ref[a] + qnorm[a] * kn_ref[a] - seen
            hit = jnp.max(bound, axis=1, keepdims=True) >= -ZERO_WEIGHT_LOG2
            need = hit if need is None else need | hit
        older = lax.broadcasted_iota(jnp.int32, need.shape, 0) < first
        return jnp.sum((need & older).astype(jnp.int32))

    def step(k, masked=False):
        pw = weights()
        scores(k - 1, masked)
        values(k, pw)

    def body(g, carry):
        for u in range(ATTN_UNROLL):
            step(first - (g * ATTN_UNROLL + u))
        return carry

    first = qi * diag
    scores(first + diag - 1, True)
    for j in range(diag - 1):
        step(first + diag - 1 - j, True)
    n = tiles_needed()
    groups = n // ATTN_UNROLL
    lax.fori_loop(0, groups, body, 0)
    for u in range(ATTN_UNROLL - 1):
        @pl.when(groups * ATTN_UNROLL + u < n)
        def _():
            step(first - (groups * ATTN_UNROLL + u))
    values(first - n, weights())

    outs = [acc_ref[a, 0:FOX_HEAD_DIM, :] / acc_ref[a, FOX_HEAD_DIM:FOX_HEAD_DIM + 1, :] for a in range(2)]
    o_ref[0, pl.ds(q_start, tq), :] = jnp.concatenate(outs, axis=0).T.astype(o_ref.dtype)


def _fox_attention(nb, kn, fqt, fk, cb, fvt, batch, seq):
    tq, tk = ATTN_Q_TILE, ATTN_K_TILE
    bound = pl.BlockSpec((2, seq // tk, tq), lambda bi, hp: (bi * FOX_PAIRS + hp, 0, 0))
    return pl.pallas_call(
        _fox_kernel,
        grid=(batch, FOX_PAIRS),
        in_specs=[bound, bound,
                  pl.BlockSpec((LANES, seq), lambda bi, hp: (hp, bi)),
                  pl.BlockSpec((1, seq, LANES), lambda bi, hp: (bi, 0, hp)),
                  pl.BlockSpec((1, 1, seq, LANES), lambda bi, hp: (bi, hp, 0, 0)),
                  pl.BlockSpec((2 * V_ROWS, seq), lambda bi, hp: (hp, bi))],
        out_specs=pl.BlockSpec((1, seq, LANES), lambda bi, hp: (bi, 0, hp)),
        out_shape=jax.ShapeDtypeStruct((batch, seq, FOX_WIDTH), BF16),
        scratch_shapes=[pltpu.VMEM((2, MXU_DEPTH, tq), BF16),
                        pltpu.VMEM((2, 1, tq), F32),
                        pltpu.VMEM((2, V_ROWS, tq), F32),
                        pltpu.VMEM((2, tk, tq), F32),
                        pltpu.VMEM((2, 1, tq), F32)],
        compiler_params=_params("parallel", "parallel"),
        name="fox_attention",
    )(nb, kn, fqt, fk, cb, fvt)


def _score_bounds(pieces, kn2, batch, seq):
    nt = seq // ATTN_K_TILE
    negc = pieces.astype(F32).sum(axis=1)
    nb = lax.cummax(negc.reshape(batch, FOX_HEADS, nt, ATTN_K_TILE).max(axis=-1), axis=2) + BOUND_SLACK
    k2 = kn2[:, 0, :FOX_HEADS].reshape(batch, nt, FOX_HEADS)
    kn = lax.cummax(jnp.sqrt(k2).transpose(0, 2, 1), axis=2) * NORM_INFLATE
    rep = lambda b: jnp.broadcast_to(b.reshape(batch * FOX_HEADS, nt, 1), (batch * FOX_HEADS, nt, ATTN_Q_TILE))
    return rep(nb), rep(kn)


def _ret_kernel(q_ref, k_ref, v_ref, g_ref, dm_ref, xi_ref, zeta_ref, cd_ref, gn_ref,
                o_ref, state_ref):
    @pl.when(pl.program_id(1) == 0)
    def _():
        state_ref[...] = jnp.zeros(state_ref.shape, F32)

    c = RET_CHUNK
    blocks = [(n, h) for n in range(RET_TILE // c) for h in range(RET_HEADS)]
    rows = lambda n: slice(n * c, (n + 1) * c)
    cols = lambda h: slice(h * RET_HEAD_DIM, (h + 1) * RET_HEAD_DIM)

    inner, kv = {}, {}
    for n, h in blocks:
        q, k = q_ref[0, rows(n), cols(h)], k_ref[0, rows(n), cols(h)]
        inner[n, h] = (lax.dot_general(q, k, _NT, preferred_element_type=F32) * dm_ref[h]).astype(BF16)
    for n, h in blocks:
        kz = (k_ref[0, rows(n), cols(h)].astype(F32) * zeta_ref[h]).astype(BF16)
        kv[n, h] = lax.dot_general(kz, v_ref[0, rows(n), cols(h)], (((0,), (0,)), ((), ())),
                                   preferred_element_type=F32)
    before = {}
    for h in range(RET_HEADS):
        st = state_ref[h]
        for n in range(RET_TILE // c):
            before[n, h] = st.astype(BF16)
            st = cd_ref[h] * st + kv[n, h]
        state_ref[h] = st
    for n, h in blocks:
        q, v = q_ref[0, rows(n), cols(h)], v_ref[0, rows(n), cols(h)]
        o = (jnp.dot(inner[n, h], v, preferred_element_type=F32)
             + xi_ref[h] * jnp.dot(q, before[n, h], preferred_element_type=F32))
        mu = jnp.mean(o, axis=-1, keepdims=True)
        d = o - mu
        var = jnp.mean(d * d, axis=-1, keepdims=True)
        y = d * lax.rsqrt(var + GN_EPS) * gn_ref[h:h + 1, :]
        g = g_ref[0, rows(n), cols(h)]
        o_ref[0, rows(n), cols(h)] = (g * jax.nn.sigmoid(g) * y).astype(o_ref.dtype)


def _retention(rq, rk, rv, rg, dm, xi, zeta, cd, gn):
    b, s, _ = rq.shape
    t = RET_TILE
    tok = pl.BlockSpec((1, t, RET_WIDTH), lambda bi, si: (bi, si, 0))
    tab = pl.BlockSpec((RET_HEADS, RET_CHUNK, RET_HEAD_DIM), lambda bi, si: (0, 0, 0))
    return pl.pallas_call(
        _ret_kernel,
        grid=(b, s // t),
        in_specs=[tok, tok, tok, tok, tab, tab, tab, tab,
                  pl.BlockSpec((RET_HEADS, RET_HEAD_DIM), lambda bi, si: (0, 0))],
        out_specs=tok,
        out_shape=jax.ShapeDtypeStruct((b, s, RET_WIDTH), BF16),
        scratch_shapes=[pltpu.VMEM((RET_HEADS, RET_HEAD_DIM, RET_HEAD_DIM), F32)],
        compiler_params=_params("parallel", "arbitrary"),
        name="retention",
    )(rq, rk, rv, rg, dm, xi, zeta, cd, gn)


def _layernorm(y, g, b):
    mu = jnp.mean(y, axis=-1, keepdims=True)
    d = y - mu
    var = jnp.mean(d * d, axis=-1, keepdims=True)
    return d * lax.rsqrt(var + LN_EPS) * g + b


def _tail_kernel(alpha, fox_ref, ret_ref, x_ref, wa_ref, wb_ref, w1_ref, w2_ref,
                 g1_ref, b1_ref, g2_ref, b2_ref, o_ref):
    mix = (jnp.dot(fox_ref[...], wa_ref[...], preferred_element_type=F32)
           + jnp.dot(ret_ref[...], wb_ref[...], preferred_element_type=F32))
    x = _layernorm(alpha * x_ref[...] + mix, g1_ref[...], b1_ref[...])
    xb = x.astype(BF16)
    acc = alpha * x
    for c in range(D_FF // FF_CHUNK):
        cols = slice(c * FF_CHUNK, (c + 1) * FF_CHUNK)
        h = jnp.maximum(jnp.dot(xb, w1_ref[:, cols], preferred_element_type=F32), 0.0)
        acc = acc + jnp.dot((h * h).astype(BF16), w2_ref[cols, :], preferred_element_type=F32)
    o_ref[...] = _layernorm(acc, g2_ref[...], b2_ref[...])


def _tail(alpha, fox, ret, x2, wa, wb, w1, w2, g1, b1, g2, b2):
    n = x2.shape[0]
    tm = TOKEN_TILE
    row = lambda i: (i, 0)
    const = lambda i: (0, 0)
    resident = lambda shape: pl.BlockSpec(shape, const, pipeline_mode=pl.Buffered(1))
    vec = pl.BlockSpec((1, D_MODEL), const)
    return pl.pallas_call(
        functools.partial(_tail_kernel, alpha),
        grid=(n // tm,),
        in_specs=[pl.BlockSpec((tm, FOX_WIDTH), row),
                  pl.BlockSpec((tm, RET_WIDTH), row),
                  pl.BlockSpec((tm, D_MODEL), row),
                  resident((FOX_WIDTH, D_MODEL)),
                  resident((RET_WIDTH, D_MODEL)),
                  resident((D_MODEL, D_FF)),
                  resident((D_FF, D_MODEL)),
                  vec, vec, vec, vec],
        out_specs=pl.BlockSpec((tm, D_MODEL), row),
        out_shape=jax.ShapeDtypeStruct((n, D_MODEL), F32),
        compiler_params=_params("parallel"),
        name="out_proj_ffn",
    )(fox, ret, x2, wa, wb, w1, w2, g1, b1, g2, b2)


def _rotary_tables(seq):
    half = RET_HEAD_DIM // 2
    inv_freq = ROPE_BASE ** (-jnp.arange(half, dtype=F32) / half)
    ang = jnp.arange(seq, dtype=F32)[:, None] * inv_freq[None, :]
    cos, sin = jnp.cos(ang), jnp.sin(ang)
    return jnp.concatenate([cos, cos], axis=-1), jnp.concatenate([-sin, sin], axis=-1)


def _decay_tables():
    gamma = 1.0 - jnp.exp2(-5.0 - jnp.arange(RET_HEADS, dtype=F32))
    log_g = jnp.log(gamma)
    idx = jnp.arange(RET_CHUNK, dtype=F32)
    diff = idx[:, None] - idx[None, :]
    dm = jnp.where(diff >= 0, jnp.exp(log_g[:, None, None] * jnp.maximum(diff, 0.0)), 0.0)
    shape = (RET_HEADS, RET_CHUNK, RET_HEAD_DIM)
    xi = jnp.broadcast_to(jnp.exp(log_g[:, None] * (idx + 1.0))[..., None], shape)
    zeta = jnp.broadcast_to(jnp.exp(log_g[:, None] * (RET_CHUNK - 1.0 - idx))[..., None], shape)
    cd = jnp.broadcast_to(jnp.exp(log_g * RET_CHUNK)[:, None, None], shape)
    return dm, xi, zeta, cd


def _pack_w_in(w, order):
    d = w.shape[0]
    by_head = lambda m: m.reshape(d, FOX_HEADS, FOX_HEAD_DIM)[:, order].reshape(d, FOX_WIDTH)
    wq, wk, wv = (by_head(w[:, i * FOX_WIDTH:(i + 1) * FOX_WIDTH]) for i in range(3))
    lo = 3 * FOX_WIDTH
    logits_t = jnp.pad(w[:, lo:lo + FOX_HEADS][:, order].T, ((0, LOGIT_ROWS - FOX_HEADS), (0, 0)))
    packed = jnp.concatenate([wk, w[:, lo + FOX_HEADS:]], axis=1).astype(BF16)
    wv_t = wv.T.reshape(FOX_HEADS, FOX_HEAD_DIM, d)
    wv_t = jnp.pad(wv_t, ((0, 0), (0, V_ROWS - FOX_HEAD_DIM), (0, 0))).reshape(FOX_HEADS * V_ROWS, d)
    packed_t = jnp.concatenate([wq.T, wv_t, logits_t], axis=0).astype(BF16)
    return packed, packed_t


def _bias_operand(pieces, batch, seq):
    p = pieces.reshape(batch, BIAS_PIECES, FOX_PAIRS, 2, seq)
    p = p.transpose(0, 2, 4, 3, 1).reshape(batch, FOX_PAIRS, seq, 2 * BIAS_PIECES)
    return jnp.pad(p, ((0, 0), (0, 0), (0, 0), (0, LANES - 2 * BIAS_PIECES)))


@jax.jit
def kernel(x, w_in, w_out, w_ff1, w_ff2, ln1_g, ln1_b, ln2_g, ln2_b, b_forget, ret_gn_g):
    batch, seq, _ = x.shape
    depth = w_in.shape[0]
    alpha = (2 * depth) ** 0.25
    cs, sn = _rotary_tables(seq)
    dm, xi, zeta, cd = _decay_tables()
    x2 = x.reshape(batch * seq, D_MODEL)
    for layer in range(depth):
        order = jnp.argsort(b_forget[layer])
        w_packed, wt_packed = _pack_w_in(w_in[layer], order)
        fqt, fk, fvt, lf_t, kn2, rq, rk, rv, rg = _inproj(x2, w_packed, wt_packed, b_forget[layer][order],
                                                          cs, sn, seq)
        seq3 = lambda t: t.reshape(batch, seq, t.shape[-1])
        lf = lf_t[:FOX_HEADS].reshape(FOX_HEADS, batch, seq).transpose(1, 0, 2)
        pieces = _neg_cumsum_pieces(lf)
        nb, kn = _score_bounds(pieces, kn2, batch, seq)
        fox = _fox_attention(nb, kn, fqt, seq3(fk), _bias_operand(pieces, batch, seq), fvt, batch, seq)
        ret = _retention(seq3(rq), seq3(rk), seq3(rv), seq3(rg), dm, xi, zeta, cd, ret_gn_g[layer])
        wo = w_out[layer].astype(BF16)
        wo_fox = wo[:FOX_WIDTH].reshape(FOX_HEADS, FOX_HEAD_DIM, D_MODEL)[order].reshape(FOX_WIDTH, D_MODEL)
        vec = lambda p: p[layer].reshape(1, D_MODEL)
        x2 = _tail(alpha, fox.reshape(batch * seq, FOX_WIDTH), ret.reshape(batch * seq, RET_WIDTH), x2,
                   wo_fox, wo[FOX_WIDTH:], w_ff1[layer].astype(BF16), w_ff2[layer].astype(BF16),
                   vec(ln1_g), vec(ln1_b), vec(ln2_g), vec(ln2_b))
    return x2.reshape(batch, seq, D_MODEL)
```

```python
import functools
import math

import jax
import jax.numpy as jnp
from jax import lax
from jax.experimental import pallas as pl
from jax.experimental.pallas import tpu as pltpu

F32 = jnp.float32
BF16 = jnp.bfloat16

D_MODEL = 1024
FOX_HEADS = 8
FOX_HEAD_DIM = 64
FOX_WIDTH = FOX_HEADS * FOX_HEAD_DIM
RET_HEADS = 4
RET_HEAD_DIM = 128
RET_WIDTH = RET_HEADS * RET_HEAD_DIM
D_FF = 4 * D_MODEL
RET_CHUNK = 128
ROPE_BASE = 10000.0
LN_EPS = 1e-5
GN_EPS = 1e-5
LOG2E = math.log2(math.e)

LANES = 128
MXU_DEPTH = 256
VMEM_LIMIT = 56 * 1024 * 1024

TOKEN_TILE = 512
ATTN_Q_TILE = 512
ATTN_K_TILE = 512
ATTN_UNROLL = 4
RUNNING_MAX_INIT = -1e30
FIXED_REF_GAP = 100.0
ZERO_WEIGHT_LOG2 = 135.0
BOUND_SLACK = 2.0
NORM_INFLATE = 1.01
RET_TILE = 512
FF_CHUNK = 1024

FOX_PAIRS = FOX_HEADS // 2
V_ROWS = 80
BIAS_PIECES = 3

_OFF_FK = 0
_OFF_RQ = FOX_WIDTH
_OFF_RK, _OFF_RV, _OFF_RG = _OFF_RQ + RET_WIDTH, _OFF_RQ + 2 * RET_WIDTH, _OFF_RQ + 3 * RET_WIDTH
PACKED_COLS = _OFF_RQ + 4 * RET_WIDTH
LOGIT_ROWS = 16
_ROW_FV = FOX_WIDTH
_ROW_FL = _ROW_FV + FOX_HEADS * V_ROWS
PACKED_ROWS = _ROW_FL + LOGIT_ROWS

_NT = (((1,), (1,)), ((), ()))


def _params(*semantics):
    return pltpu.CompilerParams(dimension_semantics=semantics, vmem_limit_bytes=VMEM_LIMIT)


def _inproj_kernel(x_ref, w_ref, wt_ref, bf_ref, cs_ref, sn_ref, ones_ref,
                   fqt_ref, fk_ref, fvt_ref, lf_ref, kn2_ref, rq_ref, rk_ref, rv_ref, rg_ref):
    xb = x_ref[...].astype(BF16)

    def proj(lo, width):
        return jnp.dot(xb, w_ref[:, lo:lo + width], preferred_element_type=F32)

    def proj_t(lo, rows):
        return lax.dot_general(wt_ref[lo:lo + rows, :], xb, _NT, preferred_element_type=F32)

    fqt_ref[...] = (proj_t(0, FOX_WIDTH) * (FOX_HEAD_DIM ** -0.5 * LOG2E)).astype(BF16)
    kb = proj(_OFF_FK, FOX_WIDTH).astype(BF16)
    fk_ref[...] = kb
    kf = kb.astype(F32)
    col = lax.broadcasted_iota(jnp.int32, (FOX_WIDTH, LANES), 0)
    lane = lax.broadcasted_iota(jnp.int32, (FOX_WIDTH, LANES), 1)
    sel = (col // FOX_HEAD_DIM == lane).astype(BF16)
    kn2 = jnp.dot((kf * kf).astype(BF16), sel, preferred_element_type=F32)
    kn2_ref[0] = jnp.broadcast_to(jnp.max(kn2, axis=0, keepdims=True), kn2_ref.shape[1:])
    vt = proj_t(_ROW_FV, FOX_HEADS * V_ROWS)
    fvt_ref[...] = jnp.where(ones_ref[...] != 0.0, 1.0, vt).astype(BF16)
    rv_ref[...] = proj(_OFF_RV, RET_WIDTH).astype(BF16)
    rg_ref[...] = proj(_OFF_RG, RET_WIDTH)

    z = proj_t(_ROW_FL, LOGIT_ROWS) + bf_ref[...]
    lf_ref[...] = -(jnp.maximum(-z, 0.0) + jnp.log1p(jnp.exp(-jnp.abs(z))))

    cs = cs_ref[...]
    sn = sn_ref[...]

    def rotary(t, scale):
        for h in range(RET_HEADS):
            th = t[:, h * RET_HEAD_DIM:(h + 1) * RET_HEAD_DIM]
            r = th * cs + pltpu.roll(th, RET_HEAD_DIM // 2, 1) * sn
            if scale is not None:
                r = r * scale
            yield h, r.astype(BF16)

    for h, r in rotary(proj(_OFF_RQ, RET_WIDTH), None):
        rq_ref[:, h * RET_HEAD_DIM:(h + 1) * RET_HEAD_DIM] = r
    for h, r in rotary(proj(_OFF_RK, RET_WIDTH), RET_HEAD_DIM ** -0.5):
        rk_ref[:, h * RET_HEAD_DIM:(h + 1) * RET_HEAD_DIM] = r


def _inproj(x2, w_packed, wt_packed, b_forget, cs, sn, seq):
    n = x2.shape[0]
    tm = TOKEN_TILE
    assert tm == ATTN_K_TILE, "the key-norm output is one row per attention key tile"
    bf_rows = jnp.broadcast_to(jnp.pad(b_forget, (0, LOGIT_ROWS - FOX_HEADS))[:, None], (LOGIT_ROWS, tm))
    pos_blocks = seq // tm
    row = lambda i: (i, 0)
    col = lambda i: (0, i)
    const = lambda i: (0, 0)
    pos = lambda i: (i % pos_blocks, 0)
    wide = lambda w, dt: jax.ShapeDtypeStruct((n, w), dt)
    tall = lambda r: jax.ShapeDtypeStruct((r, n), BF16)
    return pl.pallas_call(
        _inproj_kernel,
        grid=(n // tm,),
        in_specs=[pl.BlockSpec((tm, D_MODEL), row),
                  pl.BlockSpec((D_MODEL, PACKED_COLS), const),
                  pl.BlockSpec((PACKED_ROWS, D_MODEL), const),
                  pl.BlockSpec((LOGIT_ROWS, tm), const),
                  pl.BlockSpec((tm, LANES), pos),
                  pl.BlockSpec((tm, LANES), pos),
                  pl.BlockSpec((FOX_HEADS * V_ROWS, tm), const)],
        out_specs=[pl.BlockSpec((FOX_WIDTH, tm), col),
                   pl.BlockSpec((tm, FOX_WIDTH), row),
                   pl.BlockSpec((FOX_HEADS * V_ROWS, tm), col),
                   pl.BlockSpec((LOGIT_ROWS, tm), col),
                   pl.BlockSpec((1, 8, LANES), lambda i: (i, 0, 0))]
                  + [pl.BlockSpec((tm, RET_WIDTH), row)] * 4,
        out_shape=[tall(FOX_WIDTH), wide(FOX_WIDTH, BF16), tall(FOX_HEADS * V_ROWS),
                   jax.ShapeDtypeStruct((LOGIT_ROWS, n), F32),
                   jax.ShapeDtypeStruct((n // tm, 8, LANES), F32)]
                  + [wide(RET_WIDTH, BF16)] * 3 + [wide(RET_WIDTH, F32)],
        compiler_params=_params("parallel"),
        name="in_proj",
    )(x2, w_packed, wt_packed, bf_rows, cs, sn, _ones_rows(tm))


def _ones_rows(width):
    row = jnp.arange(FOX_HEADS * V_ROWS) % V_ROWS
    return jnp.broadcast_to((row == FOX_HEAD_DIM).astype(F32)[:, None], (FOX_HEADS * V_ROWS, width))


def _split3(t):
    hi = t.astype(BF16)
    r1 = t - hi.astype(F32)
    mid = r1.astype(BF16)
    lo = (r1 - mid.astype(F32)).astype(BF16)
    return hi, mid, lo


def _dot3(pieces, m):
    return sum(jnp.dot(p, m, preferred_element_type=F32) for p in pieces)


def _cumsum_kernel(x_ref, o_ref):
    heads, rows, lanes = x_ref.shape[1:]
    j = lax.broadcasted_iota(jnp.int32, (lanes, lanes), 0)
    l = lax.broadcasted_iota(jnp.int32, (lanes, lanes), 1)
    upper = (j <= l).astype(BF16)
    ones = jnp.ones((lanes, lanes), BF16)
    r_out = lax.broadcasted_iota(jnp.int32, (rows, rows), 0)
    r_in = lax.broadcasted_iota(jnp.int32, (rows, rows), 1)
    strict = (r_in < r_out).astype(BF16)
    for h in range(heads):
        pieces = _split3(x_ref[0, h] * (-LOG2E))
        within = _dot3(pieces, upper)
        totals = _dot3(pieces, ones)
        offs = sum(jnp.dot(strict, p, preferred_element_type=F32) for p in _split3(totals))
        for t, piece in enumerate(_split3(within + offs)):
            o_ref[0, t, h] = piece


def _neg_cumsum_pieces(lf):
    b, h, s = lf.shape
    rows = s // LANES
    return pl.pallas_call(
        _cumsum_kernel,
        grid=(b,),
        in_specs=[pl.BlockSpec((1, h, rows, LANES), lambda i: (i, 0, 0, 0))],
        out_specs=pl.BlockSpec((1, BIAS_PIECES, h, rows, LANES), lambda i: (i, 0, 0, 0, 0)),
        out_shape=jax.ShapeDtypeStruct((b, BIAS_PIECES, h, rows, LANES), BF16),
        compiler_params=_params("parallel"),
        name="forget_cumsum",
    )(lf.reshape(b, h, rows, LANES)).reshape(b, BIAS_PIECES, h, s)


def _fox_kernel(nb_ref, kn_ref, qt_ref, k_ref, cb_ref, vt_ref, o_ref,
                qaug_ref, m_ref, acc_ref, s_ref, cmax_ref):
    def query_tile(qi, carry):
        _fox_query_tile(qi, nb_ref, kn_ref, qt_ref, k_ref, cb_ref, vt_ref, o_ref,
                        qaug_ref, m_ref, acc_ref, s_ref, cmax_ref)
        return carry

    lax.fori_loop(0, qt_ref.shape[1] // ATTN_Q_TILE, query_tile, 0)


def _fox_query_tile(qi, nb_ref, kn_ref, qt_ref, k_ref, cb_ref, vt_ref, o_ref,
                    qaug_ref, m_ref, acc_ref, s_ref, cmax_ref):
    tq, tk = ATTN_Q_TILE, ATTN_K_TILE
    diag = tq // tk
    q_start = pl.multiple_of(qi * tq, tq)
    qt = qt_ref[:, pl.ds(q_start, tq)]
    q2 = qt.astype(F32) * qt.astype(F32)
    qnorm = [jnp.sqrt(jnp.sum(q2[a * FOX_HEAD_DIM:(a + 1) * FOX_HEAD_DIM], axis=0, keepdims=True))
             for a in range(2)]
    sub = lax.broadcasted_iota(jnp.int32, (LANES, tq), 0)
    for a in range(2):
        mine = (sub >= a * FOX_HEAD_DIM) & (sub < (a + 1) * FOX_HEAD_DIM)
        qaug_ref[a, 0:LANES, :] = jnp.where(mine, qt, jnp.zeros_like(qt))
        pick = (sub >= a * BIAS_PIECES) & (sub < (a + 1) * BIAS_PIECES)
        qaug_ref[a, LANES:MXU_DEPTH, :] = jnp.where(pick, 1.0, 0.0).astype(BF16)
    m_ref[...] = jnp.full(m_ref.shape, RUNNING_MAX_INIT, F32)
    acc_ref[...] = jnp.zeros(acc_ref.shape, F32)

    def scores(ki, masked):
        start = pl.multiple_of(ki * tk, tk)
        kaug = jnp.concatenate([k_ref[0, pl.ds(start, tk), :], cb_ref[0, 0, pl.ds(start, tk), :]], axis=1)
        if masked:
            key = lax.broadcasted_iota(jnp.int32, (tk, tq), 0) + ki * tk
            qry = lax.broadcasted_iota(jnp.int32, (tk, tq), 1) + qi * tq
            keep = key <= qry
        for a in range(2):
            s = jnp.dot(kaug, qaug_ref[a], preferred_element_type=F32)
            if masked:
                s = jnp.where(keep, s, -jnp.inf)
            s_ref[a] = s
            cmax_ref[a] = jnp.max(s, axis=0, keepdims=True)

    def weights():
        out = []
        for a in range(2):
            m_prev = m_ref[a]
            m_new = jnp.maximum(m_prev, cmax_ref[a])
            out.append((jnp.exp2(s_ref[a] - m_new).astype(BF16), jnp.exp2(m_prev - m_new)))
            m_ref[a] = m_new
        return out

    def values(ki, pw):
        start = pl.multiple_of(ki * tk, tk)
        for a, (p, alpha) in enumerate(pw):
            vt = vt_ref[a * V_ROWS:(a + 1) * V_ROWS, pl.ds(start, tk)]
            acc_ref[a] = alpha * acc_ref[a] + jnp.dot(vt, p, preferred_element_type=F32)

    def plan():
        need, fixed = None, None
        newest_older = jnp.maximum(first - 1, 0)
        for a in range(2):
            seen = jnp.maximum(m_ref[a], cmax_ref[a])
            bound = nb_ref[a] + qnorm[a] * kn_ref[a] - seen
            hit = jnp.max(bound, axis=1, keepdims=True) >= -ZERO_WEIGHT_LOG2
            need = hit if need is None else need | hit
            top = (nb_ref[a, pl.ds(newest_older, 1), :]
                   + qnorm[a] * kn_ref[a, pl.ds(newest_older, 1), :] - seen)
            ok = jnp.max(top) <= FIXED_REF_GAP
            fixed = ok if fixed is None else fixed & ok
        older = lax.broadcasted_iota(jnp.int32, need.shape, 0) < first
        return jnp.sum((need & older).astype(jnp.int32)), fixed

    def step(k, masked=False):
        pw = weights()
        scores(k - 1, masked)
        values(k, pw)

    def body(g, carry):
        for u in range(ATTN_UNROLL):
            step(oldest_single - (g * ATTN_UNROLL + u))
        return carry

    def fixed_weights(ki):
        start = pl.multiple_of(ki * tk, tk)
        kaug = jnp.concatenate([k_ref[0, pl.ds(start, tk), :], cb_ref[0, 0, pl.ds(start, tk), :]], axis=1)
        return [jnp.exp2(jnp.dot(kaug, qaug_ref[a], preferred_element_type=F32) - m_ref[a]).astype(BF16)
                for a in range(2)]

    def fixed_values(ki, ps):
        start = pl.multiple_of(ki * tk, tk)
        return [jnp.dot(vt_ref[a * V_ROWS:(a + 1) * V_ROWS, pl.ds(start, tk)], ps[a],
                        preferred_element_type=F32) for a in range(2)]

    def fixed_body(g, carry):
        k0 = oldest_single - 1 - g * ATTN_UNROLL
        ps = fixed_weights(k0)
        sums = None
        for u in range(ATTN_UNROLL):
            nxt = fixed_weights(k0 - u - 1) if u + 1 < ATTN_UNROLL else None
            pv = fixed_values(k0 - u, ps)
            sums = pv if sums is None else [x + y for x, y in zip(sums, pv)]
            ps = nxt
        for a in range(2):
            acc_ref[a] = acc_ref[a] + sums[a]
        return carry

    first = qi * diag
    scores(first + diag - 1, True)
    for j in range(diag - 1):
        step(first + diag - 1 - j, True)
    n, fixed = plan()
    groups = n // ATTN_UNROLL
    singles = n - groups * ATTN_UNROLL
    for u in range(ATTN_UNROLL - 1):
        @pl.when(u < singles)
        def _():
            step(first - u)
    oldest_single = first - singles
    running_groups = jnp.where(fixed, 0, groups)
    lax.fori_loop(0, running_groups, body, 0)
    values(oldest_single - running_groups * ATTN_UNROLL, weights())
    lax.fori_loop(0, groups - running_groups, fixed_body, 0)

    outs = [acc_ref[a, 0:FOX_HEAD_DIM, :] / acc_ref[a, FOX_HEAD_DIM:FOX_HEAD_DIM + 1, :] for a in range(2)]
    o_ref[0, pl.ds(q_start, tq), :] = jnp.concatenate(outs, axis=0).T.astype(o_ref.dtype)


def _fox_attention(nb, kn, fqt, fk, cb, fvt, batch, seq):
    tq, tk = ATTN_Q_TILE, ATTN_K_TILE
    bound = pl.BlockSpec((2, seq // tk, tq), lambda bi, hp: (bi * FOX_PAIRS + hp, 0, 0))
    return pl.pallas_call(
        _fox_kernel,
        grid=(batch, FOX_PAIRS),
        in_specs=[bound, bound,
                  pl.BlockSpec((LANES, seq), lambda bi, hp: (hp, bi)),
                  pl.BlockSpec((1, seq, LANES), lambda bi, hp: (bi, 0, hp)),
                  pl.BlockSpec((1, 1, seq, LANES), lambda bi, hp: (bi, hp, 0, 0)),
                  pl.BlockSpec((2 * V_ROWS, seq), lambda bi, hp: (hp, bi))],
        out_specs=pl.BlockSpec((1, seq, LANES), lambda bi, hp: (bi, 0, hp)),
        out_shape=jax.ShapeDtypeStruct((batch, seq, FOX_WIDTH), BF16),
        scratch_shapes=[pltpu.VMEM((2, MXU_DEPTH, tq), BF16),
                        pltpu.VMEM((2, 1, tq), F32),
                        pltpu.VMEM((2, V_ROWS, tq), F32),
                        pltpu.VMEM((2, tk, tq), F32),
                        pltpu.VMEM((2, 1, tq), F32)],
        compiler_params=_params("parallel", "parallel"),
        name="fox_attention",
    )(nb, kn, fqt, fk, cb, fvt)


def _score_bounds(pieces, kn2, batch, seq):
    nt = seq // ATTN_K_TILE
    negc = pieces.astype(F32).sum(axis=1)
    nb = lax.cummax(negc.reshape(batch, FOX_HEADS, nt, ATTN_K_TILE).max(axis=-1), axis=2) + BOUND_SLACK
    k2 = kn2[:, 0, :FOX_HEADS].reshape(batch, nt, FOX_HEADS)
    kn = lax.cummax(jnp.sqrt(k2).transpose(0, 2, 1), axis=2) * NORM_INFLATE
    rep = lambda b: jnp.broadcast_to(b.reshape(batch * FOX_HEADS, nt, 1), (batch * FOX_HEADS, nt, ATTN_Q_TILE))
    return rep(nb), rep(kn)


def _ret_kernel(q_ref, k_ref, v_ref, g_ref, dm_ref, xi_ref, zeta_ref, cd_ref, gn_ref,
                o_ref, state_ref):
    @pl.when(pl.program_id(1) == 0)
    def _():
        state_ref[...] = jnp.zeros(state_ref.shape, F32)

    c = RET_CHUNK
    blocks = [(n, h) for n in range(RET_TILE // c) for h in range(RET_HEADS)]
    rows = lambda n: slice(n * c, (n + 1) * c)
    cols = lambda h: slice(h * RET_HEAD_DIM, (h + 1) * RET_HEAD_DIM)

    inner, kv = {}, {}
    for n, h in blocks:
        q, k = q_ref[0, rows(n), cols(h)], k_ref[0, rows(n), cols(h)]
        inner[n, h] = (lax.dot_general(q, k, _NT, preferred_element_type=F32) * dm_ref[h]).astype(BF16)
    for n, h in blocks:
        kz = (k_ref[0, rows(n), cols(h)].astype(F32) * zeta_ref[h]).astype(BF16)
        kv[n, h] = lax.dot_general(kz, v_ref[0, rows(n), cols(h)], (((0,), (0,)), ((), ())),
                                   preferred_element_type=F32)
    before = {}
    for h in range(RET_HEADS):
        st = state_ref[h]
        for n in range(RET_TILE // c):
            before[n, h] = st.astype(BF16)
            st = cd_ref[h] * st + kv[n, h]
        state_ref[h] = st
    for n, h in blocks:
        q, v = q_ref[0, rows(n), cols(h)], v_ref[0, rows(n), cols(h)]
        o = (jnp.dot(inner[n, h], v, preferred_element_type=F32)
             + xi_ref[h] * jnp.dot(q, before[n, h], preferred_element_type=F32))
        mu = jnp.mean(o, axis=-1, keepdims=True)
        d = o - mu
        var = jnp.mean(d * d, axis=-1, keepdims=True)
        y = d * lax.rsqrt(var + GN_EPS) * gn_ref[h:h + 1, :]
        g = g_ref[0, rows(n), cols(h)]
        o_ref[0, rows(n), cols(h)] = (g * jax.nn.sigmoid(g) * y).astype(o_ref.dtype)


def _retention(rq, rk, rv, rg, dm, xi, zeta, cd, gn):
    b, s, _ = rq.shape
    t = RET_TILE
    tok = pl.BlockSpec((1, t, RET_WIDTH), lambda bi, si: (bi, si, 0))
    tab = pl.BlockSpec((RET_HEADS, RET_CHUNK, RET_HEAD_DIM), lambda bi, si: (0, 0, 0))
    return pl.pallas_call(
        _ret_kernel,
        grid=(b, s // t),
        in_specs=[tok, tok, tok, tok, tab, tab, tab, tab,
                  pl.BlockSpec((RET_HEADS, RET_HEAD_DIM), lambda bi, si: (0, 0))],
        out_specs=tok,
        out_shape=jax.ShapeDtypeStruct((b, s, RET_WIDTH), BF16),
        scratch_shapes=[pltpu.VMEM((RET_HEADS, RET_HEAD_DIM, RET_HEAD_DIM), F32)],
        compiler_params=_params("parallel", "arbitrary"),
        name="retention",
    )(rq, rk, rv, rg, dm, xi, zeta, cd, gn)


def _layernorm(y, g, b):
    mu = jnp.mean(y, axis=-1, keepdims=True)
    d = y - mu
    var = jnp.mean(d * d, axis=-1, keepdims=True)
    return d * lax.rsqrt(var + LN_EPS) * g + b


def _tail_kernel(alpha, fox_ref, ret_ref, x_ref, wa_ref, wb_ref, w1_ref, w2_ref,
                 g1_ref, b1_ref, g2_ref, b2_ref, o_ref):
    mix = (jnp.dot(fox_ref[...], wa_ref[...], preferred_element_type=F32)
           + jnp.dot(ret_ref[...], wb_ref[...], preferred_element_type=F32))
    x = _layernorm(alpha * x_ref[...] + mix, g1_ref[...], b1_ref[...])
    xb = x.astype(BF16)
    acc = alpha * x
    for c in range(D_FF // FF_CHUNK):
        cols = slice(c * FF_CHUNK, (c + 1) * FF_CHUNK)
        h = jnp.maximum(jnp.dot(xb, w1_ref[:, cols], preferred_element_type=F32), 0.0)
        acc = acc + jnp.dot((h * h).astype(BF16), w2_ref[cols, :], preferred_element_type=F32)
    o_ref[...] = _layernorm(acc, g2_ref[...], b2_ref[...])


def _tail(alpha, fox, ret, x2, wa, wb, w1, w2, g1, b1, g2, b2):
    n = x2.shape[0]
    tm = TOKEN_TILE
    row = lambda i: (i, 0)
    const = lambda i: (0, 0)
    resident = lambda shape: pl.BlockSpec(shape, const, pipeline_mode=pl.Buffered(1))
    vec = pl.BlockSpec((1, D_MODEL), const)
    return pl.pallas_call(
        functools.partial(_tail_kernel, alpha),
        grid=(n // tm,),
        in_specs=[pl.BlockSpec((tm, FOX_WIDTH), row),
                  pl.BlockSpec((tm, RET_WIDTH), row),
                  pl.BlockSpec((tm, D_MODEL), row),
                  resident((FOX_WIDTH, D_MODEL)),
                  resident((RET_WIDTH, D_MODEL)),
                  resident((D_MODEL, D_FF)),
                  resident((D_FF, D_MODEL)),
                  vec, vec, vec, vec],
        out_specs=pl.BlockSpec((tm, D_MODEL), row),
        out_shape=jax.ShapeDtypeStruct((n, D_MODEL), F32),
        compiler_params=_params("parallel"),
        name="out_proj_ffn",
    )(fox, ret, x2, wa, wb, w1, w2, g1, b1, g2, b2)


def _rotary_tables(seq):
    half = RET_HEAD_DIM // 2
    inv_freq = ROPE_BASE ** (-jnp.arange(half, dtype=F32) / half)
    ang = jnp.arange(seq, dtype=F32)[:, None] * inv_freq[None, :]
    cos, sin = jnp.cos(ang), jnp.sin(ang)
    return jnp.concatenate([cos, cos], axis=-1), jnp.concatenate([-sin, sin], axis=-1)


def _decay_tables():
    gamma = 1.0 - jnp.exp2(-5.0 - jnp.arange(RET_HEADS, dtype=F32))
    log_g = jnp.log(gamma)
    idx = jnp.arange(RET_CHUNK, dtype=F32)
    diff = idx[:, None] - idx[None, :]
    dm = jnp.where(diff >= 0, jnp.exp(log_g[:, None, None] * jnp.maximum(diff, 0.0)), 0.0)
    shape = (RET_HEADS, RET_CHUNK, RET_HEAD_DIM)
    xi = jnp.broadcast_to(jnp.exp(log_g[:, None] * (idx + 1.0))[..., None], shape)
    zeta = jnp.broadcast_to(jnp.exp(log_g[:, None] * (RET_CHUNK - 1.0 - idx))[..., None], shape)
    cd = jnp.broadcast_to(jnp.exp(log_g * RET_CHUNK)[:, None, None], shape)
    return dm, xi, zeta, cd


def _pack_w_in(w, order):
    d = w.shape[0]
    by_head = lambda m: m.reshape(d, FOX_HEADS, FOX_HEAD_DIM)[:, order].reshape(d, FOX_WIDTH)
    wq, wk, wv = (by_head(w[:, i * FOX_WIDTH:(i + 1) * FOX_WIDTH]) for i in range(3))
    lo = 3 * FOX_WIDTH
    logits_t = jnp.pad(w[:, lo:lo + FOX_HEADS][:, order].T, ((0, LOGIT_ROWS - FOX_HEADS), (0, 0)))
    packed = jnp.concatenate([wk, w[:, lo + FOX_HEADS:]], axis=1).astype(BF16)
    wv_t = wv.T.reshape(FOX_HEADS, FOX_HEAD_DIM, d)
    wv_t = jnp.pad(wv_t, ((0, 0), (0, V_ROWS - FOX_HEAD_DIM), (0, 0))).reshape(FOX_HEADS * V_ROWS, d)
    packed_t = jnp.concatenate([wq.T, wv_t, logits_t], axis=0).astype(BF16)
    return packed, packed_t


def _bias_operand(pieces, batch, seq):
    p = pieces.reshape(batch, BIAS_PIECES, FOX_PAIRS, 2, seq)
    p = p.transpose(0, 2, 4, 3, 1).reshape(batch, FOX_PAIRS, seq, 2 * BIAS_PIECES)
    return jnp.pad(p, ((0, 0), (0, 0), (0, 0), (0, LANES - 2 * BIAS_PIECES)))


@jax.jit
def kernel(x, w_in, w_out, w_ff1, w_ff2, ln1_g, ln1_b, ln2_g, ln2_b, b_forget, ret_gn_g):
    batch, seq, _ = x.shape
    depth = w_in.shape[0]
    alpha = (2 * depth) ** 0.25
    cs, sn = _rotary_tables(seq)
    dm, xi, zeta, cd = _decay_tables()
    x2 = x.reshape(batch * seq, D_MODEL)
    for layer in range(depth):
        order = jnp.argsort(b_forget[layer])
        w_packed, wt_packed = _pack_w_in(w_in[layer], order)
        fqt, fk, fvt, lf_t, kn2, rq, rk, rv, rg = _inproj(x2, w_packed, wt_packed, b_forget[layer][order],
                                                          cs, sn, seq)
        seq3 = lambda t: t.reshape(batch, seq, t.shape[-1])
        lf = lf_t[:FOX_HEADS].reshape(FOX_HEADS, batch, seq).transpose(1, 0, 2)
        pieces = _neg_cumsum_pieces(lf)
        nb, kn = _score_bounds(pieces, kn2, batch, seq)
        fox = _fox_attention(nb, kn, fqt, seq3(fk), _bias_operand(pieces, batch, seq), fvt, batch, seq)
        ret = _retention(seq3(rq), seq3(rk), seq3(rv), seq3(rg), dm, xi, zeta, cd, ret_gn_g[layer])
        wo = w_out[layer].astype(BF16)
        wo_fox = wo[:FOX_WIDTH].reshape(FOX_HEADS, FOX_HEAD_DIM, D_MODEL)[order].reshape(FOX_WIDTH, D_MODEL)
        vec = lambda p: p[layer].reshape(1, D_MODEL)
        x2 = _tail(alpha, fox.reshape(batch * seq, FOX_WIDTH), ret.reshape(batch * seq, RET_WIDTH), x2,
                   wo_fox, wo[FOX_WIDTH:], w_ff1[layer].astype(BF16), w_ff2[layer].astype(BF16),
                   vec(ln1_g), vec(ln1_b), vec(ln2_g), vec(ln2_b))
    return x2.reshape(batch, seq, D_MODEL)
```

```python
import functools
import math

import jax
import jax.numpy as jnp
from jax import lax
from jax.experimental import pallas as pl
from jax.experimental.pallas import tpu as pltpu

F32 = jnp.float32
BF16 = jnp.bfloat16

D_MODEL = 1024
FOX_HEADS = 8
FOX_HEAD_DIM = 64
FOX_WIDTH = FOX_HEADS * FOX_HEAD_DIM
RET_HEADS = 4
RET_HEAD_DIM = 128
RET_WIDTH = RET_HEADS * RET_HEAD_DIM
D_FF = 4 * D_MODEL
RET_CHUNK = 128
ROPE_BASE = 10000.0
LN_EPS = 1e-5
GN_EPS = 1e-5
LOG2E = math.log2(math.e)

LANES = 128
MXU_DEPTH = 256
VMEM_LIMIT = 56 * 1024 * 1024

TOKEN_TILE = 512
ATTN_TILE = 512
ATTN_UNROLL = 4
FIXED_REF_GAP = 100.0
ZERO_WEIGHT_LOG2 = 135.0
BOUND_SLACK = 2.0
NORM_INFLATE = 1.01
RET_TILE = 512
FF_CHUNK = 1024

FOX_PAIRS = FOX_HEADS // 2
V_ROWS = 80
BIAS_PIECES = 3

_OFF_FK = 0
_OFF_RQ = FOX_WIDTH
_OFF_RK, _OFF_RV, _OFF_RG = _OFF_RQ + RET_WIDTH, _OFF_RQ + 2 * RET_WIDTH, _OFF_RQ + 3 * RET_WIDTH
PACKED_COLS = _OFF_RQ + 4 * RET_WIDTH
LOGIT_ROWS = 16
_ROW_FV = FOX_WIDTH
_ROW_FL = _ROW_FV + FOX_HEADS * V_ROWS
PACKED_ROWS = _ROW_FL + LOGIT_ROWS

_NT = (((1,), (1,)), ((), ()))


def _params(*semantics):
    return pltpu.CompilerParams(dimension_semantics=semantics, vmem_limit_bytes=VMEM_LIMIT)


def _inproj_kernel(x_ref, w_ref, wt_ref, bf_ref, cs_ref, sn_ref, ones_ref,
                   fqt_ref, fk_ref, fvt_ref, lf_ref, kn2_ref, rq_ref, rk_ref, rv_ref, rg_ref):
    xb = x_ref[...].astype(BF16)

    def proj(lo, width):
        return jnp.dot(xb, w_ref[:, lo:lo + width], preferred_element_type=F32)

    def proj_t(lo, rows):
        return lax.dot_general(wt_ref[lo:lo + rows, :], xb, _NT, preferred_element_type=F32)

    fqt_ref[...] = (proj_t(0, FOX_WIDTH) * (FOX_HEAD_DIM ** -0.5 * LOG2E)).astype(BF16)
    kb = proj(_OFF_FK, FOX_WIDTH).astype(BF16)
    fk_ref[...] = kb
    kf = kb.astype(F32)
    col = lax.broadcasted_iota(jnp.int32, (FOX_WIDTH, LANES), 0)
    lane = lax.broadcasted_iota(jnp.int32, (FOX_WIDTH, LANES), 1)
    sel = (col // FOX_HEAD_DIM == lane).astype(BF16)
    kn2 = jnp.dot((kf * kf).astype(BF16), sel, preferred_element_type=F32)
    kn2_ref[0] = jnp.broadcast_to(jnp.max(kn2, axis=0, keepdims=True), kn2_ref.shape[1:])
    vt = proj_t(_ROW_FV, FOX_HEADS * V_ROWS)
    fvt_ref[...] = jnp.where(ones_ref[...] != 0.0, 1.0, vt).astype(BF16)
    rv_ref[...] = proj(_OFF_RV, RET_WIDTH).astype(BF16)
    rg_ref[...] = proj(_OFF_RG, RET_WIDTH)

    z = proj_t(_ROW_FL, LOGIT_ROWS) + bf_ref[...]
    lf_ref[...] = -(jnp.maximum(-z, 0.0) + jnp.log1p(jnp.exp(-jnp.abs(z))))

    cs = cs_ref[...]
    sn = sn_ref[...]

    def rotary(t, scale):
        for h in range(RET_HEADS):
            th = t[:, h * RET_HEAD_DIM:(h + 1) * RET_HEAD_DIM]
            r = th * cs + pltpu.roll(th, RET_HEAD_DIM // 2, 1) * sn
            if scale is not None:
                r = r * scale
            yield h, r.astype(BF16)

    for h, r in rotary(proj(_OFF_RQ, RET_WIDTH), None):
        rq_ref[:, h * RET_HEAD_DIM:(h + 1) * RET_HEAD_DIM] = r
    for h, r in rotary(proj(_OFF_RK, RET_WIDTH), RET_HEAD_DIM ** -0.5):
        rk_ref[:, h * RET_HEAD_DIM:(h + 1) * RET_HEAD_DIM] = r


def _inproj(x2, w_packed, wt_packed, b_forget, cs, sn, seq):
    n = x2.shape[0]
    tm = TOKEN_TILE
    assert tm == ATTN_TILE, "the key-norm output is one row per attention key tile"
    bf_rows = jnp.broadcast_to(jnp.pad(b_forget, (0, LOGIT_ROWS - FOX_HEADS))[:, None], (LOGIT_ROWS, tm))
    pos_blocks = seq // tm
    row = lambda i: (i, 0)
    col = lambda i: (0, i)
    const = lambda i: (0, 0)
    pos = lambda i: (i % pos_blocks, 0)
    wide = lambda w, dt: jax.ShapeDtypeStruct((n, w), dt)
    tall = lambda r: jax.ShapeDtypeStruct((r, n), BF16)
    return pl.pallas_call(
        _inproj_kernel,
        grid=(n // tm,),
        in_specs=[pl.BlockSpec((tm, D_MODEL), row),
                  pl.BlockSpec((D_MODEL, PACKED_COLS), const),
                  pl.BlockSpec((PACKED_ROWS, D_MODEL), const),
                  pl.BlockSpec((LOGIT_ROWS, tm), const),
                  pl.BlockSpec((tm, LANES), pos),
                  pl.BlockSpec((tm, LANES), pos),
                  pl.BlockSpec((FOX_HEADS * V_ROWS, tm), const)],
        out_specs=[pl.BlockSpec((FOX_WIDTH, tm), col),
                   pl.BlockSpec((tm, FOX_WIDTH), row),
                   pl.BlockSpec((FOX_HEADS * V_ROWS, tm), col),
                   pl.BlockSpec((LOGIT_ROWS, tm), col),
                   pl.BlockSpec((1, 8, LANES), lambda i: (i, 0, 0))]
                  + [pl.BlockSpec((tm, RET_WIDTH), row)] * 4,
        out_shape=[tall(FOX_WIDTH), wide(FOX_WIDTH, BF16), tall(FOX_HEADS * V_ROWS),
                   jax.ShapeDtypeStruct((LOGIT_ROWS, n), F32),
                   jax.ShapeDtypeStruct((n // tm, 8, LANES), F32)]
                  + [wide(RET_WIDTH, BF16)] * 3 + [wide(RET_WIDTH, F32)],
        compiler_params=_params("parallel"),
        name="in_proj",
    )(x2, w_packed, wt_packed, bf_rows, cs, sn, _ones_rows(tm))


def _ones_rows(width):
    row = jnp.arange(FOX_HEADS * V_ROWS) % V_ROWS
    return jnp.broadcast_to((row == FOX_HEAD_DIM).astype(F32)[:, None], (FOX_HEADS * V_ROWS, width))


def _split3(t):
    hi = t.astype(BF16)
    r1 = t - hi.astype(F32)
    mid = r1.astype(BF16)
    lo = (r1 - mid.astype(F32)).astype(BF16)
    return hi, mid, lo


def _dot3(pieces, m):
    return sum(jnp.dot(p, m, preferred_element_type=F32) for p in pieces)


def _cumsum_kernel(x_ref, o_ref):
    heads, rows, lanes = x_ref.shape[1:]
    j = lax.broadcasted_iota(jnp.int32, (lanes, lanes), 0)
    l = lax.broadcasted_iota(jnp.int32, (lanes, lanes), 1)
    upper = (j <= l).astype(BF16)
    ones = jnp.ones((lanes, lanes), BF16)
    r_out = lax.broadcasted_iota(jnp.int32, (rows, rows), 0)
    r_in = lax.broadcasted_iota(jnp.int32, (rows, rows), 1)
    strict = (r_in < r_out).astype(BF16)
    for h in range(heads):
        pieces = _split3(x_ref[0, h] * (-LOG2E))
        within = _dot3(pieces, upper)
        totals = _dot3(pieces, ones)
        offs = sum(jnp.dot(strict, p, preferred_element_type=F32) for p in _split3(totals))
        for t, piece in enumerate(_split3(within + offs)):
            o_ref[0, t, h] = piece


def _neg_cumsum_pieces(lf):
    b, h, s = lf.shape
    rows = s // LANES
    return pl.pallas_call(
        _cumsum_kernel,
        grid=(b,),
        in_specs=[pl.BlockSpec((1, h, rows, LANES), lambda i: (i, 0, 0, 0))],
        out_specs=pl.BlockSpec((1, BIAS_PIECES, h, rows, LANES), lambda i: (i, 0, 0, 0, 0)),
        out_shape=jax.ShapeDtypeStruct((b, BIAS_PIECES, h, rows, LANES), BF16),
        compiler_params=_params("parallel"),
        name="forget_cumsum",
    )(lf.reshape(b, h, rows, LANES)).reshape(b, BIAS_PIECES, h, s)


def _fox_kernel(nb_ref, kn_ref, qt_ref, k_ref, cb_ref, vt_ref, o_ref,
                qaug_ref, m_ref, acc_ref, s_ref, cmax_ref):
    t = ATTN_TILE
    n_tiles = qt_ref.shape[1] // t
    refs = (nb_ref, kn_ref, qt_ref, k_ref, cb_ref, vt_ref, o_ref, qaug_ref, m_ref, acc_ref, s_ref, cmax_ref)
    _fox_stage_diagonal(0, 0, *refs)

    def query_tile(qi, carry):
        _fox_query_tile(qi, n_tiles, *refs)
        return carry

    lax.fori_loop(0, n_tiles, query_tile, 0)


def _fox_query_operand(qi, slot, qt_ref, qaug_ref):
    t = ATTN_TILE
    qt = qt_ref[:, pl.ds(pl.multiple_of(qi * t, t), t)]
    sub = lax.broadcasted_iota(jnp.int32, (LANES, t), 0)
    for a in range(2):
        mine = (sub >= a * FOX_HEAD_DIM) & (sub < (a + 1) * FOX_HEAD_DIM)
        qaug_ref[slot, a, 0:LANES, :] = jnp.where(mine, qt, jnp.zeros_like(qt))
        pick = (sub >= a * BIAS_PIECES) & (sub < (a + 1) * BIAS_PIECES)
        qaug_ref[slot, a, LANES:MXU_DEPTH, :] = jnp.where(pick, 1.0, 0.0).astype(BF16)


def _fox_scores(qi, slot, ki, masked, k_ref, cb_ref, qaug_ref, s_ref, cmax_ref):
    t = ATTN_TILE
    start = pl.multiple_of(ki * t, t)
    kaug = jnp.concatenate([k_ref[0, pl.ds(start, t), :], cb_ref[0, 0, pl.ds(start, t), :]], axis=1)
    if masked:
        key = lax.broadcasted_iota(jnp.int32, (t, t), 0) + ki * t
        qry = lax.broadcasted_iota(jnp.int32, (t, t), 1) + qi * t
        keep = key <= qry
    for a in range(2):
        s = jnp.dot(kaug, qaug_ref[slot, a], preferred_element_type=F32)
        if masked:
            s = jnp.where(keep, s, -jnp.inf)
        s_ref[a] = s
        cmax_ref[a] = jnp.max(s, axis=0, keepdims=True)


def _fox_stage_diagonal(qi, slot, nb_ref, kn_ref, qt_ref, k_ref, cb_ref, vt_ref, o_ref,
                        qaug_ref, m_ref, acc_ref, s_ref, cmax_ref):
    _fox_query_operand(qi, slot, qt_ref, qaug_ref)
    _fox_scores(qi, slot, qi, True, k_ref, cb_ref, qaug_ref, s_ref, cmax_ref)


def _fox_query_tile(qi, n_tiles, nb_ref, kn_ref, qt_ref, k_ref, cb_ref, vt_ref, o_ref,
                    qaug_ref, m_ref, acc_ref, s_ref, cmax_ref):
    t = tq = tk = ATTN_TILE
    slot = qi % 2
    q_start = pl.multiple_of(qi * t, t)
    qt = qt_ref[:, pl.ds(q_start, t)]
    q2 = qt.astype(F32) * qt.astype(F32)
    qnorm = [jnp.sqrt(jnp.sum(q2[a * FOX_HEAD_DIM:(a + 1) * FOX_HEAD_DIM], axis=0, keepdims=True))
             for a in range(2)]
    m_ref[...] = jnp.full(m_ref.shape, -jnp.inf, F32)
    acc_ref[...] = jnp.zeros(acc_ref.shape, F32)

    def scores(ki, masked):
        _fox_scores(qi, slot, ki, masked, k_ref, cb_ref, qaug_ref, s_ref, cmax_ref)

    def weights():
        out = []
        for a in range(2):
            m_prev = m_ref[a]
            m_new = jnp.maximum(m_prev, cmax_ref[a])
            out.append((jnp.exp2(s_ref[a] - m_new).astype(BF16), jnp.exp2(m_prev - m_new)))
            m_ref[a] = m_new
        return out

    def values(ki, pw):
        start = pl.multiple_of(ki * tk, tk)
        for a, (p, alpha) in enumerate(pw):
            vt = vt_ref[a * V_ROWS:(a + 1) * V_ROWS, pl.ds(start, tk)]
            acc_ref[a] = alpha * acc_ref[a] + jnp.dot(vt, p, preferred_element_type=F32)

    def plan():
        need, fixed = None, None
        newest_older = jnp.maximum(first - 1, 0)
        for a in range(2):
            seen = jnp.maximum(m_ref[a], cmax_ref[a])
            bound = nb_ref[a] + qnorm[a] * kn_ref[a] - seen
            hit = jnp.max(bound, axis=1, keepdims=True) >= -ZERO_WEIGHT_LOG2
            need = hit if need is None else need | hit
            top = (nb_ref[a, pl.ds(newest_older, 1), :]
                   + qnorm[a] * kn_ref[a, pl.ds(newest_older, 1), :] - seen)
            ok = jnp.max(top) <= FIXED_REF_GAP
            fixed = ok if fixed is None else fixed & ok
        older = lax.broadcasted_iota(jnp.int32, need.shape, 0) < first
        return jnp.sum((need & older).astype(jnp.int32)), fixed

    def step(k, masked=False):
        pw = weights()
        scores(k - 1, masked)
        values(k, pw)

    def body(g, carry):
        for u in range(ATTN_UNROLL):
            step(oldest_single - (g * ATTN_UNROLL + u))
        return carry

    def fixed_weights(ki):
        start = pl.multiple_of(ki * tk, tk)
        kaug = jnp.concatenate([k_ref[0, pl.ds(start, tk), :], cb_ref[0, 0, pl.ds(start, tk), :]], axis=1)
        return [jnp.exp2(jnp.dot(kaug, qaug_ref[slot, a], preferred_element_type=F32) - m_ref[a]).astype(BF16)
                for a in range(2)]

    def fixed_values(ki, ps):
        start = pl.multiple_of(ki * tk, tk)
        return [jnp.dot(vt_ref[a * V_ROWS:(a + 1) * V_ROWS, pl.ds(start, tk)], ps[a],
                        preferred_element_type=F32) for a in range(2)]

    def fixed_body(g, carry):
        k0 = oldest_single - 1 - g * ATTN_UNROLL
        ps = fixed_weights(k0)
        sums = None
        for u in range(ATTN_UNROLL):
            nxt = fixed_weights(k0 - u - 1) if u + 1 < ATTN_UNROLL else None
            pv = fixed_values(k0 - u, ps)
            sums = pv if sums is None else [x + y for x, y in zip(sums, pv)]
            ps = nxt
        for a in range(2):
            acc_ref[a] = acc_ref[a] + sums[a]
        return carry

    first = qi
    n, fixed = plan()
    groups = n // ATTN_UNROLL
    singles = n - groups * ATTN_UNROLL
    for u in range(ATTN_UNROLL - 1):
        @pl.when(u < singles)
        def _():
            step(first - u)
    oldest_single = first - singles
    running_groups = jnp.where(fixed, 0, groups)
    lax.fori_loop(0, running_groups, body, 0)
    pw = weights()
    upcoming = jnp.minimum(qi + 1, n_tiles - 1)
    _fox_stage_diagonal(upcoming, 1 - slot, nb_ref, kn_ref, qt_ref, k_ref, cb_ref, vt_ref, o_ref,
                        qaug_ref, m_ref, acc_ref, s_ref, cmax_ref)
    values(oldest_single - running_groups * ATTN_UNROLL, pw)
    lax.fori_loop(0, groups - running_groups, fixed_body, 0)

    outs = [acc_ref[a, 0:FOX_HEAD_DIM, :] / acc_ref[a, FOX_HEAD_DIM:FOX_HEAD_DIM + 1, :] for a in range(2)]
    o_ref[0, pl.ds(q_start, tq), :] = jnp.concatenate(outs, axis=0).T.astype(o_ref.dtype)


def _fox_attention(nb, kn, fqt, fk, cb, fvt, batch, seq):
    tq = tk = ATTN_TILE
    bound = pl.BlockSpec((2, seq // tk, tq), lambda bi, hp: (bi * FOX_PAIRS + hp, 0, 0))
    return pl.pallas_call(
        _fox_kernel,
        grid=(batch, FOX_PAIRS),
        in_specs=[bound, bound,
                  pl.BlockSpec((LANES, seq), lambda bi, hp: (hp, bi)),
                  pl.BlockSpec((1, seq, LANES), lambda bi, hp: (bi, 0, hp)),
                  pl.BlockSpec((1, 1, seq, LANES), lambda bi, hp: (bi, hp, 0, 0)),
                  pl.BlockSpec((2 * V_ROWS, seq), lambda bi, hp: (hp, bi))],
        out_specs=pl.BlockSpec((1, seq, LANES), lambda bi, hp: (bi, 0, hp)),
        out_shape=jax.ShapeDtypeStruct((batch, seq, FOX_WIDTH), BF16),
        scratch_shapes=[pltpu.VMEM((2, 2, MXU_DEPTH, tq), BF16),
                        pltpu.VMEM((2, 1, tq), F32),
                        pltpu.VMEM((2, V_ROWS, tq), F32),
                        pltpu.VMEM((2, tk, tq), F32),
                        pltpu.VMEM((2, 1, tq), F32)],
        compiler_params=_params("parallel", "parallel"),
        name="fox_attention",
    )(nb, kn, fqt, fk, cb, fvt)


def _score_bounds(pieces, kn2, batch, seq):
    nt = seq // ATTN_TILE
    negc = pieces.astype(F32).sum(axis=1)
    nb = lax.cummax(negc.reshape(batch, FOX_HEADS, nt, ATTN_TILE).max(axis=-1), axis=2) + BOUND_SLACK
    k2 = kn2[:, 0, :FOX_HEADS].reshape(batch, nt, FOX_HEADS)
    kn = lax.cummax(jnp.sqrt(k2).transpose(0, 2, 1), axis=2) * NORM_INFLATE
    rep = lambda b: jnp.broadcast_to(b.reshape(batch * FOX_HEADS, nt, 1), (batch * FOX_HEADS, nt, ATTN_TILE))
    return rep(nb), rep(kn)


def _ret_kernel(q_ref, k_ref, v_ref, g_ref, dm_ref, xi_ref, zeta_ref, cd_ref, gn_ref,
                o_ref, state_ref):
    @pl.when(pl.program_id(1) == 0)
    def _():
        state_ref[...] = jnp.zeros(state_ref.shape, F32)

    c = RET_CHUNK
    blocks = [(n, h) for n in range(RET_TILE // c) for h in range(RET_HEADS)]
    rows = lambda n: slice(n * c, (n + 1) * c)
    cols = lambda h: slice(h * RET_HEAD_DIM, (h + 1) * RET_HEAD_DIM)

    inner, kv = {}, {}
    for n, h in blocks:
        q, k = q_ref[0, rows(n), cols(h)], k_ref[0, rows(n), cols(h)]
        inner[n, h] = (lax.dot_general(q, k, _NT, preferred_element_type=F32) * dm_ref[h]).astype(BF16)
    for n, h in blocks:
        kz = (k_ref[0, rows(n), cols(h)].astype(F32) * zeta_ref[h]).astype(BF16)
        kv[n, h] = lax.dot_general(kz, v_ref[0, rows(n), cols(h)], (((0,), (0,)), ((), ())),
                                   preferred_element_type=F32)
    before = {}
    for h in range(RET_HEADS):
        st = state_ref[h]
        for n in range(RET_TILE // c):
            before[n, h] = st.astype(BF16)
            st = cd_ref[h] * st + kv[n, h]
        state_ref[h] = st
    for n, h in blocks:
        q, v = q_ref[0, rows(n), cols(h)], v_ref[0, rows(n), cols(h)]
        o = (jnp.dot(inner[n, h], v, preferred_element_type=F32)
             + xi_ref[h] * jnp.dot(q, before[n, h], preferred_element_type=F32))
        mu = jnp.mean(o, axis=-1, keepdims=True)
        d = o - mu
        var = jnp.mean(d * d, axis=-1, keepdims=True)
        y = d * lax.rsqrt(var + GN_EPS) * gn_ref[h:h + 1, :]
        g = g_ref[0, rows(n), cols(h)]
        o_ref[0, rows(n), cols(h)] = (g * jax.nn.sigmoid(g) * y).astype(o_ref.dtype)


def _retention(rq, rk, rv, rg, dm, xi, zeta, cd, gn):
    b, s, _ = rq.shape
    t = RET_TILE
    tok = pl.BlockSpec((1, t, RET_WIDTH), lambda bi, si: (bi, si, 0))
    tab = pl.BlockSpec((RET_HEADS, RET_CHUNK, RET_HEAD_DIM), lambda bi, si: (0, 0, 0))
    return pl.pallas_call(
        _ret_kernel,
        grid=(b, s // t),
        in_specs=[tok, tok, tok, tok, tab, tab, tab, tab,
                  pl.BlockSpec((RET_HEADS, RET_HEAD_DIM), lambda bi, si: (0, 0))],
        out_specs=tok,
        out_shape=jax.ShapeDtypeStruct((b, s, RET_WIDTH), BF16),
        scratch_shapes=[pltpu.VMEM((RET_HEADS, RET_HEAD_DIM, RET_HEAD_DIM), F32)],
        compiler_params=_params("parallel", "arbitrary"),
        name="retention",
    )(rq, rk, rv, rg, dm, xi, zeta, cd, gn)


def _layernorm(y, g, b):
    mu = jnp.mean(y, axis=-1, keepdims=True)
    d = y - mu
    var = jnp.mean(d * d, axis=-1, keepdims=True)
    return d * lax.rsqrt(var + LN_EPS) * g + b


def _tail_kernel(alpha, fox_ref, ret_ref, x_ref, wa_ref, wb_ref, w1_ref, w2_ref,
                 g1_ref, b1_ref, g2_ref, b2_ref, o_ref):
    mix = (jnp.dot(fox_ref[...], wa_ref[...], preferred_element_type=F32)
           + jnp.dot(ret_ref[...], wb_ref[...], preferred_element_type=F32))
    x = _layernorm(alpha * x_ref[...] + mix, g1_ref[...], b1_ref[...])
    xb = x.astype(BF16)
    acc = alpha * x
    for c in range(D_FF // FF_CHUNK):
        cols = slice(c * FF_CHUNK, (c + 1) * FF_CHUNK)
        h = jnp.maximum(jnp.dot(xb, w1_ref[:, cols], preferred_element_type=F32), 0.0)
        acc = acc + jnp.dot((h * h).astype(BF16), w2_ref[cols, :], preferred_element_type=F32)
    o_ref[...] = _layernorm(acc, g2_ref[...], b2_ref[...])


def _tail(alpha, fox, ret, x2, wa, wb, w1, w2, g1, b1, g2, b2):
    n = x2.shape[0]
    tm = TOKEN_TILE
    row = lambda i: (i, 0)
    const = lambda i: (0, 0)
    resident = lambda shape: pl.BlockSpec(shape, const, pipeline_mode=pl.Buffered(1))
    vec = pl.BlockSpec((1, D_MODEL), const)
    return pl.pallas_call(
        functools.partial(_tail_kernel, alpha),
        grid=(n // tm,),
        in_specs=[pl.BlockSpec((tm, FOX_WIDTH), row),
                  pl.BlockSpec((tm, RET_WIDTH), row),
                  pl.BlockSpec((tm, D_MODEL), row),
                  resident((FOX_WIDTH, D_MODEL)),
                  resident((RET_WIDTH, D_MODEL)),
                  resident((D_MODEL, D_FF)),
                  resident((D_FF, D_MODEL)),
                  vec, vec, vec, vec],
        out_specs=pl.BlockSpec((tm, D_MODEL), row),
        out_shape=jax.ShapeDtypeStruct((n, D_MODEL), F32),
        compiler_params=_params("parallel"),
        name="out_proj_ffn",
    )(fox, ret, x2, wa, wb, w1, w2, g1, b1, g2, b2)


def _rotary_tables(seq):
    half = RET_HEAD_DIM // 2
    inv_freq = ROPE_BASE ** (-jnp.arange(half, dtype=F32) / half)
    ang = jnp.arange(seq, dtype=F32)[:, None] * inv_freq[None, :]
    cos, sin = jnp.cos(ang), jnp.sin(ang)
    return jnp.concatenate([cos, cos], axis=-1), jnp.concatenate([-sin, sin], axis=-1)


def _decay_tables():
    gamma = 1.0 - jnp.exp2(-5.0 - jnp.arange(RET_HEADS, dtype=F32))
    log_g = jnp.log(gamma)
    idx = jnp.arange(RET_CHUNK, dtype=F32)
    diff = idx[:, None] - idx[None, :]
    dm = jnp.where(diff >= 0, jnp.exp(log_g[:, None, None] * jnp.maximum(diff, 0.0)), 0.0)
    shape = (RET_HEADS, RET_CHUNK, RET_HEAD_DIM)
    xi = jnp.broadcast_to(jnp.exp(log_g[:, None] * (idx + 1.0))[..., None], shape)
    zeta = jnp.broadcast_to(jnp.exp(log_g[:, None] * (RET_CHUNK - 1.0 - idx))[..., None], shape)
    cd = jnp.broadcast_to(jnp.exp(log_g * RET_CHUNK)[:, None, None], shape)
    return dm, xi, zeta, cd


def _pack_w_in(w, order):
    d = w.shape[0]
    by_head = lambda m: m.reshape(d, FOX_HEADS, FOX_HEAD_DIM)[:, order].reshape(d, FOX_WIDTH)
    wq, wk, wv = (by_head(w[:, i * FOX_WIDTH:(i + 1) * FOX_WIDTH]) for i in range(3))
    lo = 3 * FOX_WIDTH
    logits_t = jnp.pad(w[:, lo:lo + FOX_HEADS][:, order].T, ((0, LOGIT_ROWS - FOX_HEADS), (0, 0)))
    packed = jnp.concatenate([wk, w[:, lo + FOX_HEADS:]], axis=1).astype(BF16)
    wv_t = wv.T.reshape(FOX_HEADS, FOX_HEAD_DIM, d)
    wv_t = jnp.pad(wv_t, ((0, 0), (0, V_ROWS - FOX_HEAD_DIM), (0, 0))).reshape(FOX_HEADS * V_ROWS, d)
    packed_t = jnp.concatenate([wq.T, wv_t, logits_t], axis=0).astype(BF16)
    return packed, packed_t


def _bias_operand(pieces, batch, seq):
    p = pieces.reshape(batch, BIAS_PIECES, FOX_PAIRS, 2, seq)
    p = p.transpose(0, 2, 4, 3, 1).reshape(batch, FOX_PAIRS, seq, 2 * BIAS_PIECES)
    return jnp.pad(p, ((0, 0), (0, 0), (0, 0), (0, LANES - 2 * BIAS_PIECES)))


@jax.jit
def kernel(x, w_in, w_out, w_ff1, w_ff2, ln1_g, ln1_b, ln2_g, ln2_b, b_forget, ret_gn_g):
    batch, seq, _ = x.shape
    depth = w_in.shape[0]
    alpha = (2 * depth) ** 0.25
    cs, sn = _rotary_tables(seq)
    dm, xi, zeta, cd = _decay_tables()
    x2 = x.reshape(batch * seq, D_MODEL)
    for layer in range(depth):
        order = jnp.argsort(b_forget[layer])
        w_packed, wt_packed = _pack_w_in(w_in[layer], order)
        fqt, fk, fvt, lf_t, kn2, rq, rk, rv, rg = _inproj(x2, w_packed, wt_packed, b_forget[layer][order],
                                                          cs, sn, seq)
        seq3 = lambda t: t.reshape(batch, seq, t.shape[-1])
        lf = lf_t[:FOX_HEADS].reshape(FOX_HEADS, batch, seq).transpose(1, 0, 2)
        pieces = _neg_cumsum_pieces(lf)
        nb, kn = _score_bounds(pieces, kn2, batch, seq)
        fox = _fox_attention(nb, kn, fqt, seq3(fk), _bias_operand(pieces, batch, seq), fvt, batch, seq)
        ret = _retention(seq3(rq), seq3(rk), seq3(rv), seq3(rg), dm, xi, zeta, cd, ret_gn_g[layer])
        wo = w_out[layer].astype(BF16)
        wo_fox = wo[:FOX_WIDTH].reshape(FOX_HEADS, FOX_HEAD_DIM, D_MODEL)[order].reshape(FOX_WIDTH, D_MODEL)
        vec = lambda p: p[layer].reshape(1, D_MODEL)
        x2 = _tail(alpha, fox.reshape(batch * seq, FOX_WIDTH), ret.reshape(batch * seq, RET_WIDTH), x2,
                   wo_fox, wo[FOX_WIDTH:], w_ff1[layer].astype(BF16), w_ff2[layer].astype(BF16),
                   vec(ln1_g), vec(ln1_b), vec(ln2_g), vec(ln2_b))
    return x2.reshape(batch, seq, D_MODEL)
```

```python
import functools
import math

import jax
import jax.numpy as jnp
from jax import lax
from jax.experimental import pallas as pl
from jax.experimental.pallas import tpu as pltpu

F32 = jnp.float32
BF16 = jnp.bfloat16

D_MODEL = 1024
FOX_HEADS = 8
FOX_HEAD_DIM = 64
FOX_WIDTH = FOX_HEADS * FOX_HEAD_DIM
RET_HEADS = 4
RET_HEAD_DIM = 128
RET_WIDTH = RET_HEADS * RET_HEAD_DIM
D_FF = 4 * D_MODEL
RET_CHUNK = 128
ROPE_BASE = 10000.0
LN_EPS = 1e-5
GN_EPS = 1e-5
LOG2E = math.log2(math.e)

LANES = 128
MXU_DEPTH = 256
VMEM_LIMIT = 56 * 1024 * 1024

TOKEN_TILE = 512
ATTN_TILE = 512
ATTN_UNROLL = 4
FIXED_REF_GAP = 100.0
ZERO_WEIGHT_LOG2 = 135.0
BOUND_SLACK = 2.0
NORM_INFLATE = 1.01
RET_TILE = 512
FF_CHUNK = 1024

FOX_PAIRS = FOX_HEADS // 2
V_ROWS = 80
BIAS_PIECES = 3

_OFF_FK = 0
_OFF_RQ = FOX_WIDTH
_OFF_RK, _OFF_RV, _OFF_RG = _OFF_RQ + RET_WIDTH, _OFF_RQ + 2 * RET_WIDTH, _OFF_RQ + 3 * RET_WIDTH
PACKED_COLS = _OFF_RQ + 4 * RET_WIDTH
LOGIT_ROWS = 16
_ROW_FV = FOX_WIDTH
_ROW_FL = _ROW_FV + FOX_HEADS * V_ROWS
PACKED_ROWS = _ROW_FL + LOGIT_ROWS

_NT = (((1,), (1,)), ((), ()))


def _params(*semantics):
    return pltpu.CompilerParams(dimension_semantics=semantics, vmem_limit_bytes=VMEM_LIMIT)


def _inproj_kernel(x_ref, w_ref, wt_ref, bf_ref, cs_ref, sn_ref, ones_ref,
                   fqt_ref, fk_ref, fvt_ref, lf_ref, kn2_ref, rq_ref, rk_ref, rv_ref, rg_ref):
    xb = x_ref[...].astype(BF16)

    def proj(lo, width):
        return jnp.dot(xb, w_ref[:, lo:lo + width], preferred_element_type=F32)

    def proj_t(lo, rows):
        return lax.dot_general(wt_ref[lo:lo + rows, :], xb, _NT, preferred_element_type=F32)

    fqt_ref[...] = (proj_t(0, FOX_WIDTH) * (FOX_HEAD_DIM ** -0.5 * LOG2E)).astype(BF16)
    kb = proj(_OFF_FK, FOX_WIDTH).astype(BF16)
    fk_ref[...] = kb
    kf = kb.astype(F32)
    col = lax.broadcasted_iota(jnp.int32, (FOX_WIDTH, LANES), 0)
    lane = lax.broadcasted_iota(jnp.int32, (FOX_WIDTH, LANES), 1)
    sel = (col // FOX_HEAD_DIM == lane).astype(BF16)
    kn2 = jnp.dot((kf * kf).astype(BF16), sel, preferred_element_type=F32)
    kn2_ref[0] = jnp.broadcast_to(jnp.max(kn2, axis=0, keepdims=True), kn2_ref.shape[1:])
    vt = proj_t(_ROW_FV, FOX_HEADS * V_ROWS)
    fvt_ref[...] = jnp.where(ones_ref[...] != 0.0, 1.0, vt).astype(BF16)
    rv_ref[...] = proj(_OFF_RV, RET_WIDTH).astype(BF16)
    rg_ref[...] = proj(_OFF_RG, RET_WIDTH)

    z = proj_t(_ROW_FL, LOGIT_ROWS) + bf_ref[...]
    lf_ref[...] = -(jnp.maximum(-z, 0.0) + jnp.log1p(jnp.exp(-jnp.abs(z))))

    cs = cs_ref[...]
    sn = sn_ref[...]

    def rotary(t, scale):
        for h in range(RET_HEADS):
            th = t[:, h * RET_HEAD_DIM:(h + 1) * RET_HEAD_DIM]
            r = th * cs + pltpu.roll(th, RET_HEAD_DIM // 2, 1) * sn
            if scale is not None:
                r = r * scale
            yield h, r.astype(BF16)

    for h, r in rotary(proj(_OFF_RQ, RET_WIDTH), None):
        rq_ref[:, h * RET_HEAD_DIM:(h + 1) * RET_HEAD_DIM] = r
    for h, r in rotary(proj(_OFF_RK, RET_WIDTH), RET_HEAD_DIM ** -0.5):
        rk_ref[:, h * RET_HEAD_DIM:(h + 1) * RET_HEAD_DIM] = r


def _inproj(x2, w_packed, wt_packed, b_forget, cs, sn, seq):
    n = x2.shape[0]
    tm = TOKEN_TILE
    assert tm == ATTN_TILE, "the key-norm output is one row per attention key tile"
    bf_rows = jnp.broadcast_to(jnp.pad(b_forget, (0, LOGIT_ROWS - FOX_HEADS))[:, None], (LOGIT_ROWS, tm))
    pos_blocks = seq // tm
    row = lambda i: (i, 0)
    col = lambda i: (0, i)
    const = lambda i: (0, 0)
    pos = lambda i: (i % pos_blocks, 0)
    wide = lambda w, dt: jax.ShapeDtypeStruct((n, w), dt)
    tall = lambda r: jax.ShapeDtypeStruct((r, n), BF16)
    return pl.pallas_call(
        _inproj_kernel,
        grid=(n // tm,),
        in_specs=[pl.BlockSpec((tm, D_MODEL), row),
                  pl.BlockSpec((D_MODEL, PACKED_COLS), const),
                  pl.BlockSpec((PACKED_ROWS, D_MODEL), const),
                  pl.BlockSpec((LOGIT_ROWS, tm), const),
                  pl.BlockSpec((tm, LANES), pos),
                  pl.BlockSpec((tm, LANES), pos),
                  pl.BlockSpec((FOX_HEADS * V_ROWS, tm), const)],
        out_specs=[pl.BlockSpec((FOX_WIDTH, tm), col),
                   pl.BlockSpec((tm, FOX_WIDTH), row),
                   pl.BlockSpec((FOX_HEADS * V_ROWS, tm), col),
                   pl.BlockSpec((LOGIT_ROWS, tm), col),
                   pl.BlockSpec((1, 8, LANES), lambda i: (i, 0, 0))]
                  + [pl.BlockSpec((tm, RET_WIDTH), row)] * 4,
        out_shape=[tall(FOX_WIDTH), wide(FOX_WIDTH, BF16), tall(FOX_HEADS * V_ROWS),
                   jax.ShapeDtypeStruct((LOGIT_ROWS, n), F32),
                   jax.ShapeDtypeStruct((n // tm, 8, LANES), F32)]
                  + [wide(RET_WIDTH, BF16)] * 3 + [wide(RET_WIDTH, F32)],
        compiler_params=_params("parallel"),
        name="in_proj",
    )(x2, w_packed, wt_packed, bf_rows, cs, sn, _ones_rows(tm))


def _ones_rows(width):
    row = jnp.arange(FOX_HEADS * V_ROWS) % V_ROWS
    return jnp.broadcast_to((row == FOX_HEAD_DIM).astype(F32)[:, None], (FOX_HEADS * V_ROWS, width))


def _split3(t):
    hi = t.astype(BF16)
    r1 = t - hi.astype(F32)
    mid = r1.astype(BF16)
    lo = (r1 - mid.astype(F32)).astype(BF16)
    return hi, mid, lo


def _dot3(pieces, m):
    return sum(jnp.dot(p, m, preferred_element_type=F32) for p in pieces)


def _cumsum_kernel(x_ref, o_ref):
    heads, rows, lanes = x_ref.shape[1:]
    j = lax.broadcasted_iota(jnp.int32, (lanes, lanes), 0)
    l = lax.broadcasted_iota(jnp.int32, (lanes, lanes), 1)
    upper = (j <= l).astype(BF16)
    ones = jnp.ones((lanes, lanes), BF16)
    r_out = lax.broadcasted_iota(jnp.int32, (rows, rows), 0)
    r_in = lax.broadcasted_iota(jnp.int32, (rows, rows), 1)
    strict = (r_in < r_out).astype(BF16)
    for h in range(heads):
        pieces = _split3(x_ref[0, h] * (-LOG2E))
        within = _dot3(pieces, upper)
        totals = _dot3(pieces, ones)
        offs = sum(jnp.dot(strict, p, preferred_element_type=F32) for p in _split3(totals))
        for t, piece in enumerate(_split3(within + offs)):
            o_ref[0, t, h] = piece


def _neg_cumsum_pieces(lf):
    b, h, s = lf.shape
    rows = s // LANES
    return pl.pallas_call(
        _cumsum_kernel,
        grid=(b,),
        in_specs=[pl.BlockSpec((1, h, rows, LANES), lambda i: (i, 0, 0, 0))],
        out_specs=pl.BlockSpec((1, BIAS_PIECES, h, rows, LANES), lambda i: (i, 0, 0, 0, 0)),
        out_shape=jax.ShapeDtypeStruct((b, BIAS_PIECES, h, rows, LANES), BF16),
        compiler_params=_params("parallel"),
        name="forget_cumsum",
    )(lf.reshape(b, h, rows, LANES)).reshape(b, BIAS_PIECES, h, s)


def _fox_kernel(nb_ref, kn_ref, qt_ref, k_ref, cb_ref, vt_ref, o_ref,
                qaug_ref, m_ref, acc_ref, s_ref, cmax_ref, plan_ref):
    t = ATTN_TILE
    n_tiles = qt_ref.shape[1] // t
    refs = (nb_ref, kn_ref, qt_ref, k_ref, cb_ref, vt_ref, o_ref, qaug_ref, m_ref, acc_ref, s_ref, cmax_ref,
            plan_ref)
    _fox_stage_diagonal(0, 0, *refs)

    def query_tile(qi, carry):
        _fox_query_tile(qi, n_tiles, *refs)
        return carry

    lax.fori_loop(0, n_tiles, query_tile, 0)


def _fox_query_operand(qi, slot, qt_ref, qaug_ref):
    t = ATTN_TILE
    qt = qt_ref[:, pl.ds(pl.multiple_of(qi * t, t), t)]
    sub = lax.broadcasted_iota(jnp.int32, (LANES, t), 0)
    for a in range(2):
        mine = (sub >= a * FOX_HEAD_DIM) & (sub < (a + 1) * FOX_HEAD_DIM)
        qaug_ref[slot, a, 0:LANES, :] = jnp.where(mine, qt, jnp.zeros_like(qt))
        pick = (sub >= a * BIAS_PIECES) & (sub < (a + 1) * BIAS_PIECES)
        qaug_ref[slot, a, LANES:MXU_DEPTH, :] = jnp.where(pick, 1.0, 0.0).astype(BF16)


def _fox_scores(qi, slot, ki, masked, k_ref, cb_ref, qaug_ref, s_ref, cmax_ref):
    t = ATTN_TILE
    start = pl.multiple_of(ki * t, t)
    kaug = jnp.concatenate([k_ref[0, pl.ds(start, t), :], cb_ref[0, 0, pl.ds(start, t), :]], axis=1)
    if masked:
        key = lax.broadcasted_iota(jnp.int32, (t, t), 0) + ki * t
        qry = lax.broadcasted_iota(jnp.int32, (t, t), 1) + qi * t
        keep = key <= qry
    for a in range(2):
        s = jnp.dot(kaug, qaug_ref[slot, a], preferred_element_type=F32)
        if masked:
            s = jnp.where(keep, s, -jnp.inf)
        s_ref[a] = s
        cmax_ref[a] = jnp.max(s, axis=0, keepdims=True)


def _fox_stage_diagonal(qi, slot, nb_ref, kn_ref, qt_ref, k_ref, cb_ref, vt_ref, o_ref,
                        qaug_ref, m_ref, acc_ref, s_ref, cmax_ref, plan_ref):
    t = ATTN_TILE
    _fox_query_operand(qi, slot, qt_ref, qaug_ref)
    _fox_scores(qi, slot, qi, True, k_ref, cb_ref, qaug_ref, s_ref, cmax_ref)
    qt = qt_ref[:, pl.ds(pl.multiple_of(qi * t, t), t)].astype(F32)
    q2 = qt * qt
    need, fixed = None, None
    newest_older = jnp.maximum(qi - 1, 0)
    for a in range(2):
        qnorm = jnp.sqrt(jnp.sum(q2[a * FOX_HEAD_DIM:(a + 1) * FOX_HEAD_DIM], axis=0, keepdims=True))
        seen = cmax_ref[a]
        bound = nb_ref[a] + qnorm * kn_ref[a] - seen
        hit = jnp.max(bound, axis=1, keepdims=True) >= -ZERO_WEIGHT_LOG2
        need = hit if need is None else need | hit
        top = (nb_ref[a, pl.ds(newest_older, 1), :]
               + qnorm * kn_ref[a, pl.ds(newest_older, 1), :] - seen)
        ok = jnp.max(top) <= FIXED_REF_GAP
        fixed = ok if fixed is None else fixed & ok
    older = lax.broadcasted_iota(jnp.int32, need.shape, 0) < qi
    plan_ref[0] = jnp.sum((need & older).astype(jnp.int32))
    plan_ref[1] = fixed.astype(jnp.int32)


def _fox_query_tile(qi, n_tiles, nb_ref, kn_ref, qt_ref, k_ref, cb_ref, vt_ref, o_ref,
                    qaug_ref, m_ref, acc_ref, s_ref, cmax_ref, plan_ref):
    t = tq = tk = ATTN_TILE
    slot = qi % 2
    q_start = pl.multiple_of(qi * t, t)
    n = plan_ref[0]
    fixed = plan_ref[1] != 0
    m_ref[...] = jnp.full(m_ref.shape, -jnp.inf, F32)
    acc_ref[...] = jnp.zeros(acc_ref.shape, F32)

    def scores(ki, masked):
        _fox_scores(qi, slot, ki, masked, k_ref, cb_ref, qaug_ref, s_ref, cmax_ref)

    def weights():
        out = []
        for a in range(2):
            m_prev = m_ref[a]
            m_new = jnp.maximum(m_prev, cmax_ref[a])
            out.append((jnp.exp2(s_ref[a] - m_new).astype(BF16), jnp.exp2(m_prev - m_new)))
            m_ref[a] = m_new
        return out

    def values(ki, pw):
        start = pl.multiple_of(ki * tk, tk)
        for a, (p, alpha) in enumerate(pw):
            vt = vt_ref[a * V_ROWS:(a + 1) * V_ROWS, pl.ds(start, tk)]
            acc_ref[a] = alpha * acc_ref[a] + jnp.dot(vt, p, preferred_element_type=F32)

    def step(k, masked=False):
        pw = weights()
        scores(k - 1, masked)
        values(k, pw)

    def body(g, carry):
        for u in range(ATTN_UNROLL):
            step(oldest_single - (g * ATTN_UNROLL + u))
        return carry

    def fixed_weights(ki):
        start = pl.multiple_of(ki * tk, tk)
        kaug = jnp.concatenate([k_ref[0, pl.ds(start, tk), :], cb_ref[0, 0, pl.ds(start, tk), :]], axis=1)
        return [jnp.exp2(jnp.dot(kaug, qaug_ref[slot, a], preferred_element_type=F32) - m_ref[a]).astype(BF16)
                for a in range(2)]

    def fixed_values(ki, ps):
        start = pl.multiple_of(ki * tk, tk)
        return [jnp.dot(vt_ref[a * V_ROWS:(a + 1) * V_ROWS, pl.ds(start, tk)], ps[a],
                        preferred_element_type=F32) for a in range(2)]

    def fixed_body(g, carry):
        k0 = oldest_single - 1 - g * ATTN_UNROLL
        ps = fixed_weights(k0)
        sums = None
        for u in range(ATTN_UNROLL):
            nxt = fixed_weights(k0 - u - 1) if u + 1 < ATTN_UNROLL else None
            pv = fixed_values(k0 - u, ps)
            sums = pv if sums is None else [x + y for x, y in zip(sums, pv)]
            ps = nxt
        for a in range(2):
            acc_ref[a] = acc_ref[a] + sums[a]
        return carry

    first = qi
    groups = n // ATTN_UNROLL
    singles = n - groups * ATTN_UNROLL
    for u in range(ATTN_UNROLL - 1):
        @pl.when(u < singles)
        def _():
            step(first - u)
    oldest_single = first - singles
    running_groups = jnp.where(fixed, 0, groups)
    lax.fori_loop(0, running_groups, body, 0)
    pw = weights()
    upcoming = jnp.minimum(qi + 1, n_tiles - 1)
    _fox_stage_diagonal(upcoming, 1 - slot, nb_ref, kn_ref, qt_ref, k_ref, cb_ref, vt_ref, o_ref,
                        qaug_ref, m_ref, acc_ref, s_ref, cmax_ref, plan_ref)
    values(oldest_single - running_groups * ATTN_UNROLL, pw)
    lax.fori_loop(0, groups - running_groups, fixed_body, 0)

    outs = [acc_ref[a, 0:FOX_HEAD_DIM, :] / acc_ref[a, FOX_HEAD_DIM:FOX_HEAD_DIM + 1, :] for a in range(2)]
    o_ref[0, pl.ds(q_start, tq), :] = jnp.concatenate(outs, axis=0).T.astype(o_ref.dtype)


def _fox_attention(nb, kn, fqt, fk, cb, fvt, batch, seq):
    tq = tk = ATTN_TILE
    bound = pl.BlockSpec((2, seq // tk, tq), lambda bi, hp: (bi * FOX_PAIRS + hp, 0, 0))
    return pl.pallas_call(
        _fox_kernel,
        grid=(batch, FOX_PAIRS),
        in_specs=[bound, bound,
                  pl.BlockSpec((LANES, seq), lambda bi, hp: (hp, bi)),
                  pl.BlockSpec((1, seq, LANES), lambda bi, hp: (bi, 0, hp)),
                  pl.BlockSpec((1, 1, seq, LANES), lambda bi, hp: (bi, hp, 0, 0)),
                  pl.BlockSpec((2 * V_ROWS, seq), lambda bi, hp: (hp, bi))],
        out_specs=pl.BlockSpec((1, seq, LANES), lambda bi, hp: (bi, 0, hp)),
        out_shape=jax.ShapeDtypeStruct((batch, seq, FOX_WIDTH), BF16),
        scratch_shapes=[pltpu.VMEM((2, 2, MXU_DEPTH, tq), BF16),
                        pltpu.VMEM((2, 1, tq), F32),
                        pltpu.VMEM((2, V_ROWS, tq), F32),
                        pltpu.VMEM((2, tk, tq), F32),
                        pltpu.VMEM((2, 1, tq), F32),
                        pltpu.SMEM((2,), jnp.int32)],
        compiler_params=_params("parallel", "parallel"),
        name="fox_attention",
    )(nb, kn, fqt, fk, cb, fvt)


def _score_bounds(pieces, kn2, batch, seq):
    nt = seq // ATTN_TILE
    negc = pieces.astype(F32).sum(axis=1)
    nb = lax.cummax(negc.reshape(batch, FOX_HEADS, nt, ATTN_TILE).max(axis=-1), axis=2) + BOUND_SLACK
    k2 = kn2[:, 0, :FOX_HEADS].reshape(batch, nt, FOX_HEADS)
    kn = lax.cummax(jnp.sqrt(k2).transpose(0, 2, 1), axis=2) * NORM_INFLATE
    rep = lambda b: jnp.broadcast_to(b.reshape(batch * FOX_HEADS, nt, 1), (batch * FOX_HEADS, nt, ATTN_TILE))
    return rep(nb), rep(kn)


def _ret_kernel(q_ref, k_ref, v_ref, g_ref, dm_ref, xi_ref, zeta_ref, cd_ref, gn_ref,
                o_ref, state_ref):
    @pl.when(pl.program_id(1) == 0)
    def _():
        state_ref[...] = jnp.zeros(state_ref.shape, F32)

    c = RET_CHUNK
    blocks = [(n, h) for n in range(RET_TILE // c) for h in range(RET_HEADS)]
    rows = lambda n: slice(n * c, (n + 1) * c)
    cols = lambda h: slice(h * RET_HEAD_DIM, (h + 1) * RET_HEAD_DIM)

    inner, kv = {}, {}
    for n, h in blocks:
        q, k = q_ref[0, rows(n), cols(h)], k_ref[0, rows(n), cols(h)]
        inner[n, h] = (lax.dot_general(q, k, _NT, preferred_element_type=F32) * dm_ref[h]).astype(BF16)
    for n, h in blocks:
        kz = (k_ref[0, rows(n), cols(h)].astype(F32) * zeta_ref[h]).astype(BF16)
        kv[n, h] = lax.dot_general(kz, v_ref[0, rows(n), cols(h)], (((0,), (0,)), ((), ())),
                                   preferred_element_type=F32)
    before = {}
    for h in range(RET_HEADS):
        st = state_ref[h]
        for n in range(RET_TILE // c):
            before[n, h] = st.astype(BF16)
            st = cd_ref[h] * st + kv[n, h]
        state_ref[h] = st
    for n, h in blocks:
        q, v = q_ref[0, rows(n), cols(h)], v_ref[0, rows(n), cols(h)]
        o = (jnp.dot(inner[n, h], v, preferred_element_type=F32)
             + xi_ref[h] * jnp.dot(q, before[n, h], preferred_element_type=F32))
        mu = jnp.mean(o, axis=-1, keepdims=True)
        d = o - mu
        var = jnp.mean(d * d, axis=-1, keepdims=True)
        y = d * lax.rsqrt(var + GN_EPS) * gn_ref[h:h + 1, :]
        g = g_ref[0, rows(n), cols(h)]
        o_ref[0, rows(n), cols(h)] = (g * jax.nn.sigmoid(g) * y).astype(o_ref.dtype)


def _retention(rq, rk, rv, rg, dm, xi, zeta, cd, gn):
    b, s, _ = rq.shape
    t = RET_TILE
    tok = pl.BlockSpec((1, t, RET_WIDTH), lambda bi, si: (bi, si, 0))
    tab = pl.BlockSpec((RET_HEADS, RET_CHUNK, RET_HEAD_DIM), lambda bi, si: (0, 0, 0))
    return pl.pallas_call(
        _ret_kernel,
        grid=(b, s // t),
        in_specs=[tok, tok, tok, tok, tab, tab, tab, tab,
                  pl.BlockSpec((RET_HEADS, RET_HEAD_DIM), lambda bi, si: (0, 0))],
        out_specs=tok,
        out_shape=jax.ShapeDtypeStruct((b, s, RET_WIDTH), BF16),
        scratch_shapes=[pltpu.VMEM((RET_HEADS, RET_HEAD_DIM, RET_HEAD_DIM), F32)],
        compiler_params=_params("parallel", "arbitrary"),
        name="retention",
    )(rq, rk, rv, rg, dm, xi, zeta, cd, gn)


def _layernorm(y, g, b):
    mu = jnp.mean(y, axis=-1, keepdims=True)
    d = y - mu
    var = jnp.mean(d * d, axis=-1, keepdims=True)
    return d * lax.rsqrt(var + LN_EPS) * g + b


def _tail_kernel(alpha, fox_ref, ret_ref, x_ref, wa_ref, wb_ref, w1_ref, w2_ref,
                 g1_ref, b1_ref, g2_ref, b2_ref, o_ref):
    mix = (jnp.dot(fox_ref[...], wa_ref[...], preferred_element_type=F32)
           + jnp.dot(ret_ref[...], wb_ref[...], preferred_element_type=F32))
    x = _layernorm(alpha * x_ref[...] + mix, g1_ref[...], b1_ref[...])
    xb = x.astype(BF16)
    acc = alpha * x
    for c in range(D_FF // FF_CHUNK):
        cols = slice(c * FF_CHUNK, (c + 1) * FF_CHUNK)
        h = jnp.maximum(jnp.dot(xb, w1_ref[:, cols], preferred_element_type=F32), 0.0)
        acc = acc + jnp.dot((h * h).astype(BF16), w2_ref[cols, :], preferred_element_type=F32)
    o_ref[...] = _layernorm(acc, g2_ref[...], b2_ref[...])


def _tail(alpha, fox, ret, x2, wa, wb, w1, w2, g1, b1, g2, b2):
    n = x2.shape[0]
    tm = TOKEN_TILE
    row = lambda i: (i, 0)
    const = lambda i: (0, 0)
    resident = lambda shape: pl.BlockSpec(shape, const, pipeline_mode=pl.Buffered(1))
    vec = pl.BlockSpec((1, D_MODEL), const)
    return pl.pallas_call(
        functools.partial(_tail_kernel, alpha),
        grid=(n // tm,),
        in_specs=[pl.BlockSpec((tm, FOX_WIDTH), row),
                  pl.BlockSpec((tm, RET_WIDTH), row),
                  pl.BlockSpec((tm, D_MODEL), row),
                  resident((FOX_WIDTH, D_MODEL)),
                  resident((RET_WIDTH, D_MODEL)),
                  resident((D_MODEL, D_FF)),
                  resident((D_FF, D_MODEL)),
                  vec, vec, vec, vec],
        out_specs=pl.BlockSpec((tm, D_MODEL), row),
        out_shape=jax.ShapeDtypeStruct((n, D_MODEL), F32),
        compiler_params=_params("parallel"),
        name="out_proj_ffn",
    )(fox, ret, x2, wa, wb, w1, w2, g1, b1, g2, b2)


def _rotary_tables(seq):
    half = RET_HEAD_DIM // 2
    inv_freq = ROPE_BASE ** (-jnp.arange(half, dtype=F32) / half)
    ang = jnp.arange(seq, dtype=F32)[:, None] * inv_freq[None, :]
    cos, sin = jnp.cos(ang), jnp.sin(ang)
    return jnp.concatenate([cos, cos], axis=-1), jnp.concatenate([-sin, sin], axis=-1)


def _decay_tables():
    gamma = 1.0 - jnp.exp2(-5.0 - jnp.arange(RET_HEADS, dtype=F32))
    log_g = jnp.log(gamma)
    idx = jnp.arange(RET_CHUNK, dtype=F32)
    diff = idx[:, None] - idx[None, :]
    dm = jnp.where(diff >= 0, jnp.exp(log_g[:, None, None] * jnp.maximum(diff, 0.0)), 0.0)
    shape = (RET_HEADS, RET_CHUNK, RET_HEAD_DIM)
    xi = jnp.broadcast_to(jnp.exp(log_g[:, None] * (idx + 1.0))[..., None], shape)
    zeta = jnp.broadcast_to(jnp.exp(log_g[:, None] * (RET_CHUNK - 1.0 - idx))[..., None], shape)
    cd = jnp.broadcast_to(jnp.exp(log_g * RET_CHUNK)[:, None, None], shape)
    return dm, xi, zeta, cd


def _pack_w_in(w, order):
    d = w.shape[0]
    by_head = lambda m: m.reshape(d, FOX_HEADS, FOX_HEAD_DIM)[:, order].reshape(d, FOX_WIDTH)
    wq, wk, wv = (by_head(w[:, i * FOX_WIDTH:(i + 1) * FOX_WIDTH]) for i in range(3))
    lo = 3 * FOX_WIDTH
    logits_t = jnp.pad(w[:, lo:lo + FOX_HEADS][:, order].T, ((0, LOGIT_ROWS - FOX_HEADS), (0, 0)))
    packed = jnp.concatenate([wk, w[:, lo + FOX_HEADS:]], axis=1).astype(BF16)
    wv_t = wv.T.reshape(FOX_HEADS, FOX_HEAD_DIM, d)
    wv_t = jnp.pad(wv_t, ((0, 0), (0, V_ROWS - FOX_HEAD_DIM), (0, 0))).reshape(FOX_HEADS * V_ROWS, d)
    packed_t = jnp.concatenate([wq.T, wv_t, logits_t], axis=0).astype(BF16)
    return packed, packed_t


def _bias_operand(pieces, batch, seq):
    p = pieces.reshape(batch, BIAS_PIECES, FOX_PAIRS, 2, seq)
    p = p.transpose(0, 2, 4, 3, 1).reshape(batch, FOX_PAIRS, seq, 2 * BIAS_PIECES)
    return jnp.pad(p, ((0, 0), (0, 0), (0, 0), (0, LANES - 2 * BIAS_PIECES)))


@jax.jit
def kernel(x, w_in, w_out, w_ff1, w_ff2, ln1_g, ln1_b, ln2_g, ln2_b, b_forget, ret_gn_g):
    batch, seq, _ = x.shape
    depth = w_in.shape[0]
    alpha = (2 * depth) ** 0.25
    cs, sn = _rotary_tables(seq)
    dm, xi, zeta, cd = _decay_tables()
    x2 = x.reshape(batch * seq, D_MODEL)
    for layer in range(depth):
        order = jnp.argsort(b_forget[layer])
        w_packed, wt_packed = _pack_w_in(w_in[layer], order)
        fqt, fk, fvt, lf_t, kn2, rq, rk, rv, rg = _inproj(x2, w_packed, wt_packed, b_forget[layer][order],
                                                          cs, sn, seq)
        seq3 = lambda t: t.reshape(batch, seq, t.shape[-1])
        lf = lf_t[:FOX_HEADS].reshape(FOX_HEADS, batch, seq).transpose(1, 0, 2)
        pieces = _neg_cumsum_pieces(lf)
        nb, kn = _score_bounds(pieces, kn2, batch, seq)
        fox = _fox_attention(nb, kn, fqt, seq3(fk), _bias_operand(pieces, batch, seq), fvt, batch, seq)
        ret = _retention(seq3(rq), seq3(rk), seq3(rv), seq3(rg), dm, xi, zeta, cd, ret_gn_g[layer])
        wo = w_out[layer].astype(BF16)
        wo_fox = wo[:FOX_WIDTH].reshape(FOX_HEADS, FOX_HEAD_DIM, D_MODEL)[order].reshape(FOX_WIDTH, D_MODEL)
        vec = lambda p: p[layer].reshape(1, D_MODEL)
        x2 = _tail(alpha, fox.reshape(batch * seq, FOX_WIDTH), ret.reshape(batch * seq, RET_WIDTH), x2,
                   wo_fox, wo[FOX_WIDTH:], w_ff1[layer].astype(BF16), w_ff2[layer].astype(BF16),
                   vec(ln1_g), vec(ln1_b), vec(ln2_g), vec(ln2_b))
    return x2.reshape(batch, seq, D_MODEL)
```

```python
import functools
import math

import jax
import jax.numpy as jnp
from jax import lax
from jax.experimental import pallas as pl
from jax.experimental.pallas import tpu as pltpu

F32 = jnp.float32
BF16 = jnp.bfloat16

D_MODEL = 1024
FOX_HEADS = 8
FOX_HEAD_DIM = 64
FOX_WIDTH = FOX_HEADS * FOX_HEAD_DIM
RET_HEADS = 4
RET_HEAD_DIM = 128
RET_WIDTH = RET_HEADS * RET_HEAD_DIM
D_FF = 4 * D_MODEL
RET_CHUNK = 128
ROPE_BASE = 10000.0
LN_EPS = 1e-5
GN_EPS = 1e-5
LOG2E = math.log2(math.e)

LANES = 128
MXU_DEPTH = 256
VMEM_LIMIT = 56 * 1024 * 1024

TOKEN_TILE = 512
ATTN_TILE = 512
ATTN_UNROLL = 4
FIXED_REF_GAP = 100.0
ZERO_WEIGHT_LOG2 = 135.0
BOUND_SLACK = 2.0
NORM_INFLATE = 1.01
RET_TILE = 512
FF_CHUNK = 1024

FOX_PAIRS = FOX_HEADS // 2
V_ROWS = 80
BIAS_PIECES = 3

_OFF_FK = 0
_OFF_RQ = FOX_WIDTH
_OFF_RK, _OFF_RV, _OFF_RG = _OFF_RQ + RET_WIDTH, _OFF_RQ + 2 * RET_WIDTH, _OFF_RQ + 3 * RET_WIDTH
PACKED_COLS = _OFF_RQ + 4 * RET_WIDTH
LOGIT_ROWS = 16
_ROW_FV = FOX_WIDTH
_ROW_FL = _ROW_FV + FOX_HEADS * V_ROWS
PACKED_ROWS = _ROW_FL + LOGIT_ROWS

_NT = (((1,), (1,)), ((), ()))


def _params(*semantics):
    return pltpu.CompilerParams(dimension_semantics=semantics, vmem_limit_bytes=VMEM_LIMIT)


def _inproj_kernel(x_ref, w_ref, wt_ref, bf_ref, cs_ref, sn_ref, ones_ref,
                   fqt_ref, fk_ref, fvt_ref, lf_ref, kn2_ref, rq_ref, rk_ref, rv_ref, rg_ref):
    xb = x_ref[...].astype(BF16)

    def proj(lo, width):
        return jnp.dot(xb, w_ref[:, lo:lo + width], preferred_element_type=F32)

    def proj_t(lo, rows):
        return lax.dot_general(wt_ref[lo:lo + rows, :], xb, _NT, preferred_element_type=F32)

    fqt_ref[...] = (proj_t(0, FOX_WIDTH) * (FOX_HEAD_DIM ** -0.5 * LOG2E)).astype(BF16)
    kb = proj(_OFF_FK, FOX_WIDTH).astype(BF16)
    fk_ref[...] = kb
    kf = kb.astype(F32)
    col = lax.broadcasted_iota(jnp.int32, (FOX_WIDTH, LANES), 0)
    lane = lax.broadcasted_iota(jnp.int32, (FOX_WIDTH, LANES), 1)
    sel = (col // FOX_HEAD_DIM == lane).astype(BF16)
    kn2 = jnp.dot((kf * kf).astype(BF16), sel, preferred_element_type=F32)
    kn2_ref[0] = jnp.broadcast_to(jnp.max(kn2, axis=0, keepdims=True), kn2_ref.shape[1:])
    vt = proj_t(_ROW_FV, FOX_HEADS * V_ROWS)
    fvt_ref[...] = jnp.where(ones_ref[...] != 0.0, 1.0, vt).astype(BF16)
    rv_ref[...] = proj(_OFF_RV, RET_WIDTH).astype(BF16)
    rg_ref[...] = proj(_OFF_RG, RET_WIDTH)

    z = proj_t(_ROW_FL, LOGIT_ROWS) + bf_ref[...]
    lf_ref[...] = -(jnp.maximum(-z, 0.0) + jnp.log1p(jnp.exp(-jnp.abs(z))))

    cs = cs_ref[...]
    sn = sn_ref[...]

    def rotary(t, scale):
        for h in range(RET_HEADS):
            th = t[:, h * RET_HEAD_DIM:(h + 1) * RET_HEAD_DIM]
            r = th * cs + pltpu.roll(th, RET_HEAD_DIM // 2, 1) * sn
            if scale is not None:
                r = r * scale
            yield h, r.astype(BF16)

    for h, r in rotary(proj(_OFF_RQ, RET_WIDTH), None):
        rq_ref[:, h * RET_HEAD_DIM:(h + 1) * RET_HEAD_DIM] = r
    for h, r in rotary(proj(_OFF_RK, RET_WIDTH), RET_HEAD_DIM ** -0.5):
        rk_ref[:, h * RET_HEAD_DIM:(h + 1) * RET_HEAD_DIM] = r


def _inproj(x2, w_packed, wt_packed, b_forget, cs, sn, seq):
    n = x2.shape[0]
    tm = TOKEN_TILE
    assert tm == ATTN_TILE, "the key-norm output is one row per attention key tile"
    bf_rows = jnp.broadcast_to(jnp.pad(b_forget, (0, LOGIT_ROWS - FOX_HEADS))[:, None], (LOGIT_ROWS, tm))
    pos_blocks = seq // tm
    row = lambda i: (i, 0)
    col = lambda i: (0, i)
    const = lambda i: (0, 0)
    pos = lambda i: (i % pos_blocks, 0)
    wide = lambda w, dt: jax.ShapeDtypeStruct((n, w), dt)
    tall = lambda r: jax.ShapeDtypeStruct((r, n), BF16)
    return pl.pallas_call(
        _inproj_kernel,
        grid=(n // tm,),
        in_specs=[pl.BlockSpec((tm, D_MODEL), row),
                  pl.BlockSpec((D_MODEL, PACKED_COLS), const),
                  pl.BlockSpec((PACKED_ROWS, D_MODEL), const),
                  pl.BlockSpec((LOGIT_ROWS, tm), const),
                  pl.BlockSpec((tm, LANES), pos),
                  pl.BlockSpec((tm, LANES), pos),
                  pl.BlockSpec((FOX_HEADS * V_ROWS, tm), const)],
        out_specs=[pl.BlockSpec((FOX_WIDTH, tm), col),
                   pl.BlockSpec((tm, FOX_WIDTH), row),
                   pl.BlockSpec((FOX_HEADS * V_ROWS, tm), col),
                   pl.BlockSpec((LOGIT_ROWS, tm), col),
                   pl.BlockSpec((1, 8, LANES), lambda i: (i, 0, 0))]
                  + [pl.BlockSpec((tm, RET_WIDTH), row)] * 4,
        out_shape=[tall(FOX_WIDTH), wide(FOX_WIDTH, BF16), tall(FOX_HEADS * V_ROWS),
                   jax.ShapeDtypeStruct((LOGIT_ROWS, n), F32),
                   jax.ShapeDtypeStruct((n // tm, 8, LANES), F32)]
                  + [wide(RET_WIDTH, BF16)] * 3 + [wide(RET_WIDTH, F32)],
        compiler_params=_params("parallel"),
        name="in_proj",
    )(x2, w_packed, wt_packed, bf_rows, cs, sn, _ones_rows(tm))


def _ones_rows(width):
    row = jnp.arange(FOX_HEADS * V_ROWS) % V_ROWS
    return jnp.broadcast_to((row == FOX_HEAD_DIM).astype(F32)[:, None], (FOX_HEADS * V_ROWS, width))


def _split3(t):
    hi = t.astype(BF16)
    r1 = t - hi.astype(F32)
    mid = r1.astype(BF16)
    lo = (r1 - mid.astype(F32)).astype(BF16)
    return hi, mid, lo


def _dot3(pieces, m):
    return sum(jnp.dot(p, m, preferred_element_type=F32) for p in pieces)


def _cumsum_kernel(x_ref, o_ref):
    heads, rows, lanes = x_ref.shape[1:]
    j = lax.broadcasted_iota(jnp.int32, (lanes, lanes), 0)
    l = lax.broadcasted_iota(jnp.int32, (lanes, lanes), 1)
    upper = (j <= l).astype(BF16)
    ones = jnp.ones((lanes, lanes), BF16)
    r_out = lax.broadcasted_iota(jnp.int32, (rows, rows), 0)
    r_in = lax.broadcasted_iota(jnp.int32, (rows, rows), 1)
    strict = (r_in < r_out).astype(BF16)
    for h in range(heads):
        pieces = _split3(x_ref[0, h] * (-LOG2E))
        within = _dot3(pieces, upper)
        totals = _dot3(pieces, ones)
        offs = sum(jnp.dot(strict, p, preferred_element_type=F32) for p in _split3(totals))
        for t, piece in enumerate(_split3(within + offs)):
            o_ref[0, t, h] = piece


def _neg_cumsum_pieces(lf):
    b, h, s = lf.shape
    rows = s // LANES
    return pl.pallas_call(
        _cumsum_kernel,
        grid=(b,),
        in_specs=[pl.BlockSpec((1, h, rows, LANES), lambda i: (i, 0, 0, 0))],
        out_specs=pl.BlockSpec((1, BIAS_PIECES, h, rows, LANES), lambda i: (i, 0, 0, 0, 0)),
        out_shape=jax.ShapeDtypeStruct((b, BIAS_PIECES, h, rows, LANES), BF16),
        compiler_params=_params("parallel"),
        name="forget_cumsum",
    )(lf.reshape(b, h, rows, LANES)).reshape(b, BIAS_PIECES, h, s)


def _fox_kernel(nb_ref, kn_ref, qt_ref, k_ref, cb_ref, vt_ref, o_ref,
                qaug_ref, m_ref, acc_ref, s_ref, cmax_ref, plan_ref):
    t = ATTN_TILE
    n_tiles = qt_ref.shape[1] // t
    refs = (nb_ref, kn_ref, qt_ref, k_ref, cb_ref, vt_ref, o_ref, qaug_ref, m_ref, acc_ref, s_ref, cmax_ref,
            plan_ref)
    _fox_stage_diagonal(0, 0, *refs)

    def query_tile(qi, carry):
        _fox_query_tile(qi, n_tiles, *refs)
        return carry

    lax.fori_loop(0, n_tiles, query_tile, 0)


def _fox_query_operand(qi, slot, qt_ref, qaug_ref):
    t = ATTN_TILE
    qt = qt_ref[:, pl.ds(pl.multiple_of(qi * t, t), t)]
    sub = lax.broadcasted_iota(jnp.int32, (LANES, t), 0)
    for a in range(2):
        mine = (sub >= a * FOX_HEAD_DIM) & (sub < (a + 1) * FOX_HEAD_DIM)
        qaug_ref[slot, a, 0:LANES, :] = jnp.where(mine, qt, jnp.zeros_like(qt))
        pick = (sub >= a * BIAS_PIECES) & (sub < (a + 1) * BIAS_PIECES)
        qaug_ref[slot, a, LANES:MXU_DEPTH, :] = jnp.where(pick, 1.0, 0.0).astype(BF16)


def _fox_scores(qi, slot, ki, masked, k_ref, cb_ref, qaug_ref, s_ref, cmax_ref):
    t = ATTN_TILE
    start = pl.multiple_of(ki * t, t)
    kaug = jnp.concatenate([k_ref[0, pl.ds(start, t), :], cb_ref[0, 0, pl.ds(start, t), :]], axis=1)
    if masked:
        key = lax.broadcasted_iota(jnp.int32, (t, t), 0) + ki * t
        qry = lax.broadcasted_iota(jnp.int32, (t, t), 1) + qi * t
        keep = key <= qry
    for a in range(2):
        s = jnp.dot(kaug, qaug_ref[slot, a], preferred_element_type=F32)
        if masked:
            s = jnp.where(keep, s, -jnp.inf)
        s_ref[a] = s
        cmax_ref[a] = jnp.max(s, axis=0, keepdims=True)


def _fox_stage_diagonal(qi, slot, nb_ref, kn_ref, qt_ref, k_ref, cb_ref, vt_ref, o_ref,
                        qaug_ref, m_ref, acc_ref, s_ref, cmax_ref, plan_ref):
    t = ATTN_TILE
    _fox_query_operand(qi, slot, qt_ref, qaug_ref)
    _fox_scores(qi, slot, qi, True, k_ref, cb_ref, qaug_ref, s_ref, cmax_ref)
    qt = qt_ref[:, pl.ds(pl.multiple_of(qi * t, t), t)].astype(F32)
    q2 = qt * qt
    need, fixed = None, None
    newest_older = jnp.maximum(qi - 1, 0)
    for a in range(2):
        qnorm = jnp.sqrt(jnp.sum(q2[a * FOX_HEAD_DIM:(a + 1) * FOX_HEAD_DIM], axis=0, keepdims=True))
        seen = cmax_ref[a]
        bound = nb_ref[a] + qnorm * kn_ref[a] - seen
        hit = jnp.max(bound, axis=1, keepdims=True) >= -ZERO_WEIGHT_LOG2
        need = hit if need is None else need | hit
        top = (nb_ref[a, pl.ds(newest_older, 1), :]
               + qnorm * kn_ref[a, pl.ds(newest_older, 1), :] - seen)
        ok = jnp.max(top) <= FIXED_REF_GAP
        fixed = ok if fixed is None else fixed & ok
    older = lax.broadcasted_iota(jnp.int32, need.shape, 0) < qi
    plan_ref[0] = jnp.sum((need & older).astype(jnp.int32))
    plan_ref[1] = fixed.astype(jnp.int32)


def _fox_query_tile(qi, n_tiles, nb_ref, kn_ref, qt_ref, k_ref, cb_ref, vt_ref, o_ref,
                    qaug_ref, m_ref, acc_ref, s_ref, cmax_ref, plan_ref):
    t = tq = tk = ATTN_TILE
    slot = qi % 2
    q_start = pl.multiple_of(qi * t, t)
    n = plan_ref[0]
    fixed = plan_ref[1] != 0
    m_ref[...] = jnp.full(m_ref.shape, -jnp.inf, F32)
    acc_ref[...] = jnp.zeros(acc_ref.shape, F32)

    def scores(ki, masked):
        _fox_scores(qi, slot, ki, masked, k_ref, cb_ref, qaug_ref, s_ref, cmax_ref)

    def weights():
        out = []
        for a in range(2):
            m_prev = m_ref[a]
            m_new = jnp.maximum(m_prev, cmax_ref[a])
            out.append((jnp.exp2(s_ref[a] - m_new).astype(BF16), jnp.exp2(m_prev - m_new)))
            m_ref[a] = m_new
        return out

    def values(ki, pw):
        start = pl.multiple_of(ki * tk, tk)
        for a, (p, alpha) in enumerate(pw):
            vt = vt_ref[a * V_ROWS:(a + 1) * V_ROWS, pl.ds(start, tk)]
            acc_ref[a] = alpha * acc_ref[a] + jnp.dot(vt, p, preferred_element_type=F32)

    def step(k, masked=False):
        pw = weights()
        scores(k - 1, masked)
        values(k, pw)

    def body(g, carry):
        for u in range(ATTN_UNROLL):
            step(oldest_single - (g * ATTN_UNROLL + u))
        return carry

    def fixed_weights(ki):
        start = pl.multiple_of(ki * tk, tk)
        kaug = jnp.concatenate([k_ref[0, pl.ds(start, tk), :], cb_ref[0, 0, pl.ds(start, tk), :]], axis=1)
        return [jnp.exp2(jnp.dot(kaug, qaug_ref[slot, a], preferred_element_type=F32) - m_ref[a]).astype(BF16)
                for a in range(2)]

    def fixed_values(ki, ps):
        start = pl.multiple_of(ki * tk, tk)
        return [jnp.dot(vt_ref[a * V_ROWS:(a + 1) * V_ROWS, pl.ds(start, tk)], ps[a],
                        preferred_element_type=F32) for a in range(2)]

    def fixed_tiles(k0, count):
        ps = fixed_weights(k0)
        sums = None
        for u in range(count):
            nxt = fixed_weights(k0 - u - 1) if u + 1 < count else None
            pv = fixed_values(k0 - u, ps)
            sums = pv if sums is None else [x + y for x, y in zip(sums, pv)]
            ps = nxt
        for a in range(2):
            acc_ref[a] = acc_ref[a] + sums[a]

    def fixed_body(g, carry):
        fixed_tiles(first - 1 - singles - g * ATTN_UNROLL, ATTN_UNROLL)
        return carry

    first = qi
    groups = n // ATTN_UNROLL
    singles = n - groups * ATTN_UNROLL
    running_singles = jnp.where(fixed, 0, singles)
    for u in range(ATTN_UNROLL - 1):
        @pl.when(u < running_singles)
        def _():
            step(first - u)
    oldest_single = first - running_singles
    running_groups = jnp.where(fixed, 0, groups)
    lax.fori_loop(0, running_groups, body, 0)
    pw = weights()
    upcoming = jnp.minimum(qi + 1, n_tiles - 1)
    _fox_stage_diagonal(upcoming, 1 - slot, nb_ref, kn_ref, qt_ref, k_ref, cb_ref, vt_ref, o_ref,
                        qaug_ref, m_ref, acc_ref, s_ref, cmax_ref, plan_ref)
    values(oldest_single - running_groups * ATTN_UNROLL, pw)
    for count in range(1, ATTN_UNROLL):
        @pl.when(fixed & (singles == count))
        def _():
            fixed_tiles(first - 1, count)
    lax.fori_loop(0, groups - running_groups, fixed_body, 0)

    outs = [acc_ref[a, 0:FOX_HEAD_DIM, :] / acc_ref[a, FOX_HEAD_DIM:FOX_HEAD_DIM + 1, :] for a in range(2)]
    o_ref[0, pl.ds(q_start, tq), :] = jnp.concatenate(outs, axis=0).T.astype(o_ref.dtype)


def _fox_attention(nb, kn, fqt, fk, cb, fvt, batch, seq):
    tq = tk = ATTN_TILE
    bound = pl.BlockSpec((2, seq // tk, tq), lambda bi, hp: (bi * FOX_PAIRS + hp, 0, 0))
    return pl.pallas_call(
        _fox_kernel,
        grid=(batch, FOX_PAIRS),
        in_specs=[bound, bound,
                  pl.BlockSpec((LANES, seq), lambda bi, hp: (hp, bi)),
                  pl.BlockSpec((1, seq, LANES), lambda bi, hp: (bi, 0, hp)),
                  pl.BlockSpec((1, 1, seq, LANES), lambda bi, hp: (bi, hp, 0, 0)),
                  pl.BlockSpec((2 * V_ROWS, seq), lambda bi, hp: (hp, bi))],
        out_specs=pl.BlockSpec((1, seq, LANES), lambda bi, hp: (bi, 0, hp)),
        out_shape=jax.ShapeDtypeStruct((batch, seq, FOX_WIDTH), BF16),
        scratch_shapes=[pltpu.VMEM((2, 2, MXU_DEPTH, tq), BF16),
                        pltpu.VMEM((2, 1, tq), F32),
                        pltpu.VMEM((2, V_ROWS, tq), F32),
                        pltpu.VMEM((2, tk, tq), F32),
                        pltpu.VMEM((2, 1, tq), F32),
                        pltpu.SMEM((2,), jnp.int32)],
        compiler_params=_params("parallel", "parallel"),
        name="fox_attention",
    )(nb, kn, fqt, fk, cb, fvt)


def _score_bounds(pieces, kn2, batch, seq):
    nt = seq // ATTN_TILE
    negc = pieces.astype(F32).sum(axis=1)
    nb = lax.cummax(negc.reshape(batch, FOX_HEADS, nt, ATTN_TILE).max(axis=-1), axis=2) + BOUND_SLACK
    k2 = kn2[:, 0, :FOX_HEADS].reshape(batch, nt, FOX_HEADS)
    kn = lax.cummax(jnp.sqrt(k2).transpose(0, 2, 1), axis=2) * NORM_INFLATE
    rep = lambda b: jnp.broadcast_to(b.reshape(batch * FOX_HEADS, nt, 1), (batch * FOX_HEADS, nt, ATTN_TILE))
    return rep(nb), rep(kn)


def _ret_kernel(q_ref, k_ref, v_ref, g_ref, dm_ref, xi_ref, zeta_ref, cd_ref, gn_ref,
                o_ref, state_ref):
    @pl.when(pl.program_id(1) == 0)
    def _():
        state_ref[...] = jnp.zeros(state_ref.shape, F32)

    c = RET_CHUNK
    blocks = [(n, h) for n in range(RET_TILE // c) for h in range(RET_HEADS)]
    rows = lambda n: slice(n * c, (n + 1) * c)
    cols = lambda h: slice(h * RET_HEAD_DIM, (h + 1) * RET_HEAD_DIM)

    inner, kv = {}, {}
    for n, h in blocks:
        q, k = q_ref[0, rows(n), cols(h)], k_ref[0, rows(n), cols(h)]
        inner[n, h] = (lax.dot_general(q, k, _NT, preferred_element_type=F32) * dm_ref[h]).astype(BF16)
    for n, h in blocks:
        kz = (k_ref[0, rows(n), cols(h)].astype(F32) * zeta_ref[h]).astype(BF16)
        kv[n, h] = lax.dot_general(kz, v_ref[0, rows(n), cols(h)], (((0,), (0,)), ((), ())),
                                   preferred_element_type=F32)
    before = {}
    for h in range(RET_HEADS):
        st = state_ref[h]
        for n in range(RET_TILE // c):
            before[n, h] = st.astype(BF16)
            st = cd_ref[h] * st + kv[n, h]
        state_ref[h] = st
    for n, h in blocks:
        q, v = q_ref[0, rows(n), cols(h)], v_ref[0, rows(n), cols(h)]
        o = (jnp.dot(inner[n, h], v, preferred_element_type=F32)
             + xi_ref[h] * jnp.dot(q, before[n, h], preferred_element_type=F32))
        mu = jnp.mean(o, axis=-1, keepdims=True)
        d = o - mu
        var = jnp.mean(d * d, axis=-1, keepdims=True)
        y = d * lax.rsqrt(var + GN_EPS) * gn_ref[h:h + 1, :]
        g = g_ref[0, rows(n), cols(h)]
        o_ref[0, rows(n), cols(h)] = (g * jax.nn.sigmoid(g) * y).astype(o_ref.dtype)


def _retention(rq, rk, rv, rg, dm, xi, zeta, cd, gn):
    b, s, _ = rq.shape
    t = RET_TILE
    tok = pl.BlockSpec((1, t, RET_WIDTH), lambda bi, si: (bi, si, 0))
    tab = pl.BlockSpec((RET_HEADS, RET_CHUNK, RET_HEAD_DIM), lambda bi, si: (0, 0, 0))
    return pl.pallas_call(
        _ret_kernel,
        grid=(b, s // t),
        in_specs=[tok, tok, tok, tok, tab, tab, tab, tab,
                  pl.BlockSpec((RET_HEADS, RET_HEAD_DIM), lambda bi, si: (0, 0))],
        out_specs=tok,
        out_shape=jax.ShapeDtypeStruct((b, s, RET_WIDTH), BF16),
        scratch_shapes=[pltpu.VMEM((RET_HEADS, RET_HEAD_DIM, RET_HEAD_DIM), F32)],
        compiler_params=_params("parallel", "arbitrary"),
        name="retention",
    )(rq, rk, rv, rg, dm, xi, zeta, cd, gn)


def _layernorm(y, g, b):
    mu = jnp.mean(y, axis=-1, keepdims=True)
    d = y - mu
    var = jnp.mean(d * d, axis=-1, keepdims=True)
    return d * lax.rsqrt(var + LN_EPS) * g + b


def _tail_kernel(alpha, fox_ref, ret_ref, x_ref, wa_ref, wb_ref, w1_ref, w2_ref,
                 g1_ref, b1_ref, g2_ref, b2_ref, o_ref):
    mix = (jnp.dot(fox_ref[...], wa_ref[...], preferred_element_type=F32)
           + jnp.dot(ret_ref[...], wb_ref[...], preferred_element_type=F32))
    x = _layernorm(alpha * x_ref[...] + mix, g1_ref[...], b1_ref[...])
    xb = x.astype(BF16)
    acc = alpha * x
    for c in range(D_FF // FF_CHUNK):
        cols = slice(c * FF_CHUNK, (c + 1) * FF_CHUNK)
        h = jnp.maximum(jnp.dot(xb, w1_ref[:, cols], preferred_element_type=F32), 0.0)
        acc = acc + jnp.dot((h * h).astype(BF16), w2_ref[cols, :], preferred_element_type=F32)
    o_ref[...] = _layernorm(acc, g2_ref[...], b2_ref[...])


def _tail(alpha, fox, ret, x2, wa, wb, w1, w2, g1, b1, g2, b2):
    n = x2.shape[0]
    tm = TOKEN_TILE
    row = lambda i: (i, 0)
    const = lambda i: (0, 0)
    resident = lambda shape: pl.BlockSpec(shape, const, pipeline_mode=pl.Buffered(1))
    vec = pl.BlockSpec((1, D_MODEL), const)
    return pl.pallas_call(
        functools.partial(_tail_kernel, alpha),
        grid=(n // tm,),
        in_specs=[pl.BlockSpec((tm, FOX_WIDTH), row),
                  pl.BlockSpec((tm, RET_WIDTH), row),
                  pl.BlockSpec((tm, D_MODEL), row),
                  resident((FOX_WIDTH, D_MODEL)),
                  resident((RET_WIDTH, D_MODEL)),
                  resident((D_MODEL, D_FF)),
                  resident((D_FF, D_MODEL)),
                  vec, vec, vec, vec],
        out_specs=pl.BlockSpec((tm, D_MODEL), row),
        out_shape=jax.ShapeDtypeStruct((n, D_MODEL), F32),
        compiler_params=_params("parallel"),
        name="out_proj_ffn",
    )(fox, ret, x2, wa, wb, w1, w2, g1, b1, g2, b2)


def _rotary_tables(seq):
    half = RET_HEAD_DIM // 2
    inv_freq = ROPE_BASE ** (-jnp.arange(half, dtype=F32) / half)
    ang = jnp.arange(seq, dtype=F32)[:, None] * inv_freq[None, :]
    cos, sin = jnp.cos(ang), jnp.sin(ang)
    return jnp.concatenate([cos, cos], axis=-1), jnp.concatenate([-sin, sin], axis=-1)


def _decay_tables():
    gamma = 1.0 - jnp.exp2(-5.0 - jnp.arange(RET_HEADS, dtype=F32))
    log_g = jnp.log(gamma)
    idx = jnp.arange(RET_CHUNK, dtype=F32)
    diff = idx[:, None] - idx[None, :]
    dm = jnp.where(diff >= 0, jnp.exp(log_g[:, None, None] * jnp.maximum(diff, 0.0)), 0.0)
    shape = (RET_HEADS, RET_CHUNK, RET_HEAD_DIM)
    xi = jnp.broadcast_to(jnp.exp(log_g[:, None] * (idx + 1.0))[..., None], shape)
    zeta = jnp.broadcast_to(jnp.exp(log_g[:, None] * (RET_CHUNK - 1.0 - idx))[..., None], shape)
    cd = jnp.broadcast_to(jnp.exp(log_g * RET_CHUNK)[:, None, None], shape)
    return dm, xi, zeta, cd


def _pack_w_in(w, order):
    d = w.shape[0]
    by_head = lambda m: m.reshape(d, FOX_HEADS, FOX_HEAD_DIM)[:, order].reshape(d, FOX_WIDTH)
    wq, wk, wv = (by_head(w[:, i * FOX_WIDTH:(i + 1) * FOX_WIDTH]) for i in range(3))
    lo = 3 * FOX_WIDTH
    logits_t = jnp.pad(w[:, lo:lo + FOX_HEADS][:, order].T, ((0, LOGIT_ROWS - FOX_HEADS), (0, 0)))
    packed = jnp.concatenate([wk, w[:, lo + FOX_HEADS:]], axis=1).astype(BF16)
    wv_t = wv.T.reshape(FOX_HEADS, FOX_HEAD_DIM, d)
    wv_t = jnp.pad(wv_t, ((0, 0), (0, V_ROWS - FOX_HEAD_DIM), (0, 0))).reshape(FOX_HEADS * V_ROWS, d)
    packed_t = jnp.concatenate([wq.T, wv_t, logits_t], axis=0).astype(BF16)
    return packed, packed_t


def _bias_operand(pieces, batch, seq):
    p = pieces.reshape(batch, BIAS_PIECES, FOX_PAIRS, 2, seq)
    p = p.transpose(0, 2, 4, 3, 1).reshape(batch, FOX_PAIRS, seq, 2 * BIAS_PIECES)
    return jnp.pad(p, ((0, 0), (0, 0), (0, 0), (0, LANES - 2 * BIAS_PIECES)))


@jax.jit
def kernel(x, w_in, w_out, w_ff1, w_ff2, ln1_g, ln1_b, ln2_g, ln2_b, b_forget, ret_gn_g):
    batch, seq, _ = x.shape
    depth = w_in.shape[0]
    alpha = (2 * depth) ** 0.25
    cs, sn = _rotary_tables(seq)
    dm, xi, zeta, cd = _decay_tables()
    x2 = x.reshape(batch * seq, D_MODEL)
    for layer in range(depth):
        order = jnp.argsort(b_forget[layer])
        w_packed, wt_packed = _pack_w_in(w_in[layer], order)
        fqt, fk, fvt, lf_t, kn2, rq, rk, rv, rg = _inproj(x2, w_packed, wt_packed, b_forget[layer][order],
                                                          cs, sn, seq)
        seq3 = lambda t: t.reshape(batch, seq, t.shape[-1])
        lf = lf_t[:FOX_HEADS].reshape(FOX_HEADS, batch, seq).transpose(1, 0, 2)
        pieces = _neg_cumsum_pieces(lf)
        nb, kn = _score_bounds(pieces, kn2, batch, seq)
        fox = _fox_attention(nb, kn, fqt, seq3(fk), _bias_operand(pieces, batch, seq), fvt, batch, seq)
        ret = _retention(seq3(rq), seq3(rk), seq3(rv), seq3(rg), dm, xi, zeta, cd, ret_gn_g[layer])
        wo = w_out[layer].astype(BF16)
        wo_fox = wo[:FOX_WIDTH].reshape(FOX_HEADS, FOX_HEAD_DIM, D_MODEL)[order].reshape(FOX_WIDTH, D_MODEL)
        vec = lambda p: p[layer].reshape(1, D_MODEL)
        x2 = _tail(alpha, fox.reshape(batch * seq, FOX_WIDTH), ret.reshape(batch * seq, RET_WIDTH), x2,
                   wo_fox, wo[FOX_WIDTH:], w_ff1[layer].astype(BF16), w_ff2[layer].astype(BF16),
                   vec(ln1_g), vec(ln1_b), vec(ln2_g), vec(ln2_b))
    return x2.reshape(batch, seq, D_MODEL)
```

```python
import functools
import math

import jax
import jax.numpy as jnp
from jax import lax
from jax.experimental import pallas as pl
from jax.experimental.pallas import tpu as pltpu

F32 = jnp.float32
BF16 = jnp.bfloat16

D_MODEL = 1024
FOX_HEADS = 8
FOX_HEAD_DIM = 64
FOX_WIDTH = FOX_HEADS * FOX_HEAD_DIM
RET_HEADS = 4
RET_HEAD_DIM = 128
RET_WIDTH = RET_HEADS * RET_HEAD_DIM
D_FF = 4 * D_MODEL
RET_CHUNK = 128
ROPE_BASE = 10000.0
LN_EPS = 1e-5
GN_EPS = 1e-5
LOG2E = math.log2(math.e)

LANES = 128
MXU_DEPTH = 256
VMEM_LIMIT = 56 * 1024 * 1024

TOKEN_TILE = 512
ATTN_TILE = 512
ATTN_UNROLL = 4
FIXED_UNROLL = 8
FIXED_REF_GAP = 100.0
ZERO_WEIGHT_LOG2 = 135.0
BOUND_SLACK = 2.0
NORM_INFLATE = 1.01
RET_TILE = 512
FF_CHUNK = 1024

FOX_PAIRS = FOX_HEADS // 2
V_ROWS = 80
BIAS_PIECES = 3

_OFF_FK = 0
_OFF_RQ = FOX_WIDTH
_OFF_RK, _OFF_RV, _OFF_RG = _OFF_RQ + RET_WIDTH, _OFF_RQ + 2 * RET_WIDTH, _OFF_RQ + 3 * RET_WIDTH
PACKED_COLS = _OFF_RQ + 4 * RET_WIDTH
LOGIT_ROWS = 16
_ROW_FV = FOX_WIDTH
_ROW_FL = _ROW_FV + FOX_HEADS * V_ROWS
PACKED_ROWS = _ROW_FL + LOGIT_ROWS

_NT = (((1,), (1,)), ((), ()))


def _params(*semantics):
    return pltpu.CompilerParams(dimension_semantics=semantics, vmem_limit_bytes=VMEM_LIMIT)


def _inproj_kernel(x_ref, w_ref, wt_ref, bf_ref, cs_ref, sn_ref, ones_ref,
                   fqt_ref, fk_ref, fvt_ref, lf_ref, kn2_ref, rq_ref, rk_ref, rv_ref, rg_ref):
    xb = x_ref[...].astype(BF16)

    def proj(lo, width):
        return jnp.dot(xb, w_ref[:, lo:lo + width], preferred_element_type=F32)

    def proj_t(lo, rows):
        return lax.dot_general(wt_ref[lo:lo + rows, :], xb, _NT, preferred_element_type=F32)

    fqt_ref[...] = (proj_t(0, FOX_WIDTH) * (FOX_HEAD_DIM ** -0.5 * LOG2E)).astype(BF16)
    kb = proj(_OFF_FK, FOX_WIDTH).astype(BF16)
    fk_ref[...] = kb
    kf = kb.astype(F32)
    col = lax.broadcasted_iota(jnp.int32, (FOX_WIDTH, LANES), 0)
    lane = lax.broadcasted_iota(jnp.int32, (FOX_WIDTH, LANES), 1)
    sel = (col // FOX_HEAD_DIM == lane).astype(BF16)
    kn2 = jnp.dot((kf * kf).astype(BF16), sel, preferred_element_type=F32)
    kn2_ref[0] = jnp.broadcast_to(jnp.max(kn2, axis=0, keepdims=True), kn2_ref.shape[1:])
    vt = proj_t(_ROW_FV, FOX_HEADS * V_ROWS)
    fvt_ref[...] = jnp.where(ones_ref[...] != 0.0, 1.0, vt).astype(BF16)
    rv_ref[...] = proj(_OFF_RV, RET_WIDTH).astype(BF16)
    rg_ref[...] = proj(_OFF_RG, RET_WIDTH)

    z = proj_t(_ROW_FL, LOGIT_ROWS) + bf_ref[...]
    lf_ref[...] = -(jnp.maximum(-z, 0.0) + jnp.log1p(jnp.exp(-jnp.abs(z))))

    cs = cs_ref[...]
    sn = sn_ref[...]

    def rotary(t, scale):
        for h in range(RET_HEADS):
            th = t[:, h * RET_HEAD_DIM:(h + 1) * RET_HEAD_DIM]
            r = th * cs + pltpu.roll(th, RET_HEAD_DIM // 2, 1) * sn
            if scale is not None:
                r = r * scale
            yield h, r.astype(BF16)

    for h, r in rotary(proj(_OFF_RQ, RET_WIDTH), None):
        rq_ref[:, h * RET_HEAD_DIM:(h + 1) * RET_HEAD_DIM] = r
    for h, r in rotary(proj(_OFF_RK, RET_WIDTH), RET_HEAD_DIM ** -0.5):
        rk_ref[:, h * RET_HEAD_DIM:(h + 1) * RET_HEAD_DIM] = r


def _inproj(x2, w_packed, wt_packed, b_forget, cs, sn, seq):
    n = x2.shape[0]
    tm = TOKEN_TILE
    assert tm == ATTN_TILE, "the key-norm output is one row per attention key tile"
    bf_rows = jnp.broadcast_to(jnp.pad(b_forget, (0, LOGIT_ROWS - FOX_HEADS))[:, None], (LOGIT_ROWS, tm))
    pos_blocks = seq // tm
    row = lambda i: (i, 0)
    col = lambda i: (0, i)
    const = lambda i: (0, 0)
    pos = lambda i: (i % pos_blocks, 0)
    wide = lambda w, dt: jax.ShapeDtypeStruct((n, w), dt)
    tall = lambda r: jax.ShapeDtypeStruct((r, n), BF16)
    return pl.pallas_call(
        _inproj_kernel,
        grid=(n // tm,),
        in_specs=[pl.BlockSpec((tm, D_MODEL), row),
                  pl.BlockSpec((D_MODEL, PACKED_COLS), const),
                  pl.BlockSpec((PACKED_ROWS, D_MODEL), const),
                  pl.BlockSpec((LOGIT_ROWS, tm), const),
                  pl.BlockSpec((tm, LANES), pos),
                  pl.BlockSpec((tm, LANES), pos),
                  pl.BlockSpec((FOX_HEADS * V_ROWS, tm), const)],
        out_specs=[pl.BlockSpec((FOX_WIDTH, tm), col),
                   pl.BlockSpec((tm, FOX_WIDTH), row),
                   pl.BlockSpec((FOX_HEADS * V_ROWS, tm), col),
                   pl.BlockSpec((LOGIT_ROWS, tm), col),
                   pl.BlockSpec((1, 8, LANES), lambda i: (i, 0, 0))]
                  + [pl.BlockSpec((tm, RET_WIDTH), row)] * 4,
        out_shape=[tall(FOX_WIDTH), wide(FOX_WIDTH, BF16), tall(FOX_HEADS * V_ROWS),
                   jax.ShapeDtypeStruct((LOGIT_ROWS, n), F32),
                   jax.ShapeDtypeStruct((n // tm, 8, LANES), F32)]
                  + [wide(RET_WIDTH, BF16)] * 3 + [wide(RET_WIDTH, F32)],
        compiler_params=_params("parallel"),
        name="in_proj",
    )(x2, w_packed, wt_packed, bf_rows, cs, sn, _ones_rows(tm))


def _ones_rows(width):
    row = jnp.arange(FOX_HEADS * V_ROWS) % V_ROWS
    return jnp.broadcast_to((row == FOX_HEAD_DIM).astype(F32)[:, None], (FOX_HEADS * V_ROWS, width))


def _split3(t):
    hi = t.astype(BF16)
    r1 = t - hi.astype(F32)
    mid = r1.astype(BF16)
    lo = (r1 - mid.astype(F32)).astype(BF16)
    return hi, mid, lo


def _dot3(pieces, m):
    return sum(jnp.dot(p, m, preferred_element_type=F32) for p in pieces)


def _cumsum_kernel(x_ref, o_ref):
    heads, rows, lanes = x_ref.shape[1:]
    j = lax.broadcasted_iota(jnp.int32, (lanes, lanes), 0)
    l = lax.broadcasted_iota(jnp.int32, (lanes, lanes), 1)
    upper = (j <= l).astype(BF16)
    ones = jnp.ones((lanes, lanes), BF16)
    r_out = lax.broadcasted_iota(jnp.int32, (rows, rows), 0)
    r_in = lax.broadcasted_iota(jnp.int32, (rows, rows), 1)
    strict = (r_in < r_out).astype(BF16)
    for h in range(heads):
        pieces = _split3(x_ref[0, h] * (-LOG2E))
        within = _dot3(pieces, upper)
        totals = _dot3(pieces, ones)
        offs = sum(jnp.dot(strict, p, preferred_element_type=F32) for p in _split3(totals))
        for t, piece in enumerate(_split3(within + offs)):
            o_ref[0, t, h] = piece


def _neg_cumsum_pieces(lf):
    b, h, s = lf.shape
    rows = s // LANES
    return pl.pallas_call(
        _cumsum_kernel,
        grid=(b,),
        in_specs=[pl.BlockSpec((1, h, rows, LANES), lambda i: (i, 0, 0, 0))],
        out_specs=pl.BlockSpec((1, BIAS_PIECES, h, rows, LANES), lambda i: (i, 0, 0, 0, 0)),
        out_shape=jax.ShapeDtypeStruct((b, BIAS_PIECES, h, rows, LANES), BF16),
        compiler_params=_params("parallel"),
        name="forget_cumsum",
    )(lf.reshape(b, h, rows, LANES)).reshape(b, BIAS_PIECES, h, s)


def _fox_kernel(nb_ref, kn_ref, qt_ref, k_ref, cb_ref, vt_ref, o_ref,
                qaug_ref, m_ref, acc_ref, s_ref, cmax_ref, plan_ref):
    t = ATTN_TILE
    n_tiles = qt_ref.shape[1] // t
    refs = (nb_ref, kn_ref, qt_ref, k_ref, cb_ref, vt_ref, o_ref, qaug_ref, m_ref, acc_ref, s_ref, cmax_ref,
            plan_ref)
    _fox_stage_diagonal(0, 0, *refs)

    def query_tile(qi, carry):
        _fox_query_tile(qi, n_tiles, *refs)
        return carry

    lax.fori_loop(0, n_tiles, query_tile, 0)


def _fox_query_operand(qi, slot, qt_ref, qaug_ref):
    t = ATTN_TILE
    qt = qt_ref[:, pl.ds(pl.multiple_of(qi * t, t), t)]
    sub = lax.broadcasted_iota(jnp.int32, (LANES, t), 0)
    for a in range(2):
        mine = (sub >= a * FOX_HEAD_DIM) & (sub < (a + 1) * FOX_HEAD_DIM)
        qaug_ref[slot, a, 0:LANES, :] = jnp.where(mine, qt, jnp.zeros_like(qt))
        pick = (sub >= a * BIAS_PIECES) & (sub < (a + 1) * BIAS_PIECES)
        qaug_ref[slot, a, LANES:MXU_DEPTH, :] = jnp.where(pick, 1.0, 0.0).astype(BF16)


def _fox_scores(qi, slot, ki, masked, k_ref, cb_ref, qaug_ref, s_ref, cmax_ref):
    t = ATTN_TILE
    start = pl.multiple_of(ki * t, t)
    kaug = jnp.concatenate([k_ref[0, pl.ds(start, t), :], cb_ref[0, 0, pl.ds(start, t), :]], axis=1)
    if masked:
        key = lax.broadcasted_iota(jnp.int32, (t, t), 0) + ki * t
        qry = lax.broadcasted_iota(jnp.int32, (t, t), 1) + qi * t
        keep = key <= qry
    for a in range(2):
        s = jnp.dot(kaug, qaug_ref[slot, a], preferred_element_type=F32)
        if masked:
            s = jnp.where(keep, s, -jnp.inf)
        s_ref[a] = s
        cmax_ref[a] = jnp.max(s, axis=0, keepdims=True)


def _fox_stage_diagonal(qi, slot, nb_ref, kn_ref, qt_ref, k_ref, cb_ref, vt_ref, o_ref,
                        qaug_ref, m_ref, acc_ref, s_ref, cmax_ref, plan_ref):
    t = ATTN_TILE
    _fox_query_operand(qi, slot, qt_ref, qaug_ref)
    _fox_scores(qi, slot, qi, True, k_ref, cb_ref, qaug_ref, s_ref, cmax_ref)
    qt = qt_ref[:, pl.ds(pl.multiple_of(qi * t, t), t)].astype(F32)
    q2 = qt * qt
    need, fixed = None, None
    newest_older = jnp.maximum(qi - 1, 0)
    for a in range(2):
        qnorm = jnp.sqrt(jnp.sum(q2[a * FOX_HEAD_DIM:(a + 1) * FOX_HEAD_DIM], axis=0, keepdims=True))
        seen = cmax_ref[a]
        bound = nb_ref[a] + qnorm * kn_ref[a] - seen
        hit = jnp.max(bound, axis=1, keepdims=True) >= -ZERO_WEIGHT_LOG2
        need = hit if need is None else need | hit
        top = (nb_ref[a, pl.ds(newest_older, 1), :]
               + qnorm * kn_ref[a, pl.ds(newest_older, 1), :] - seen)
        ok = jnp.max(top) <= FIXED_REF_GAP
        fixed = ok if fixed is None else fixed & ok
    older = lax.broadcasted_iota(jnp.int32, need.shape, 0) < qi
    plan_ref[0] = jnp.sum((need & older).astype(jnp.int32))
    plan_ref[1] = fixed.astype(jnp.int32)


def _fox_query_tile(qi, n_tiles, nb_ref, kn_ref, qt_ref, k_ref, cb_ref, vt_ref, o_ref,
                    qaug_ref, m_ref, acc_ref, s_ref, cmax_ref, plan_ref):
    t = tq = tk = ATTN_TILE
    slot = qi % 2
    q_start = pl.multiple_of(qi * t, t)
    n = plan_ref[0]
    fixed = plan_ref[1] != 0
    m_ref[...] = jnp.full(m_ref.shape, -jnp.inf, F32)
    acc_ref[...] = jnp.zeros(acc_ref.shape, F32)

    def scores(ki, masked):
        _fox_scores(qi, slot, ki, masked, k_ref, cb_ref, qaug_ref, s_ref, cmax_ref)

    def weights():
        out = []
        for a in range(2):
            m_prev = m_ref[a]
            m_new = jnp.maximum(m_prev, cmax_ref[a])
            out.append((jnp.exp2(s_ref[a] - m_new).astype(BF16), jnp.exp2(m_prev - m_new)))
            m_ref[a] = m_new
        return out

    def values(ki, pw):
        start = pl.multiple_of(ki * tk, tk)
        for a, (p, alpha) in enumerate(pw):
            vt = vt_ref[a * V_ROWS:(a + 1) * V_ROWS, pl.ds(start, tk)]
            acc_ref[a] = alpha * acc_ref[a] + jnp.dot(vt, p, preferred_element_type=F32)

    def step(k, masked=False):
        pw = weights()
        scores(k - 1, masked)
        values(k, pw)

    def body(g, carry):
        for u in range(ATTN_UNROLL):
            step(oldest_single - (g * ATTN_UNROLL + u))
        return carry

    def fixed_weights(ki):
        start = pl.multiple_of(ki * tk, tk)
        kaug = jnp.concatenate([k_ref[0, pl.ds(start, tk), :], cb_ref[0, 0, pl.ds(start, tk), :]], axis=1)
        return [jnp.exp2(jnp.dot(kaug, qaug_ref[slot, a], preferred_element_type=F32) - m_ref[a]).astype(BF16)
                for a in range(2)]

    def fixed_values(ki, ps):
        start = pl.multiple_of(ki * tk, tk)
        return [jnp.dot(vt_ref[a * V_ROWS:(a + 1) * V_ROWS, pl.ds(start, tk)], ps[a],
                        preferred_element_type=F32) for a in range(2)]

    def fixed_tiles(k0, count):
        ps = fixed_weights(k0)
        sums = None
        for u in range(count):
            nxt = fixed_weights(k0 - u - 1) if u + 1 < count else None
            pv = fixed_values(k0 - u, ps)
            sums = pv if sums is None else [x + y for x, y in zip(sums, pv)]
            ps = nxt
        for a in range(2):
            acc_ref[a] = acc_ref[a] + sums[a]

    first = qi
    running_n = jnp.where(fixed, 0, n)
    running_groups = running_n // ATTN_UNROLL
    running_singles = running_n - running_groups * ATTN_UNROLL
    for u in range(ATTN_UNROLL - 1):
        @pl.when(u < running_singles)
        def _():
            step(first - u)
    oldest_single = first - running_singles
    lax.fori_loop(0, running_groups, body, 0)
    pw = weights()
    upcoming = jnp.minimum(qi + 1, n_tiles - 1)
    _fox_stage_diagonal(upcoming, 1 - slot, nb_ref, kn_ref, qt_ref, k_ref, cb_ref, vt_ref, o_ref,
                        qaug_ref, m_ref, acc_ref, s_ref, cmax_ref, plan_ref)
    values(oldest_single - running_groups * ATTN_UNROLL, pw)
    fixed_n = n - running_n
    done = jnp.int32(0)
    count = FIXED_UNROLL // 2
    while count >= 1:
        take = (fixed_n & count) != 0

        @pl.when(take)
        def _(count=count, done=done):
            fixed_tiles(first - 1 - done, count)
        done = done + jnp.where(take, count, 0)
        count //= 2

    def fixed_body(g, carry):
        fixed_tiles(first - 1 - done - g * FIXED_UNROLL, FIXED_UNROLL)
        return carry

    lax.fori_loop(0, fixed_n // FIXED_UNROLL, fixed_body, 0)

    outs = [acc_ref[a, 0:FOX_HEAD_DIM, :] / acc_ref[a, FOX_HEAD_DIM:FOX_HEAD_DIM + 1, :] for a in range(2)]
    o_ref[0, pl.ds(q_start, tq), :] = jnp.concatenate(outs, axis=0).T.astype(o_ref.dtype)


def _fox_attention(nb, kn, fqt, fk, cb, fvt, batch, seq):
    tq = tk = ATTN_TILE
    bound = pl.BlockSpec((2, seq // tk, tq), lambda bi, hp: (bi * FOX_PAIRS + hp, 0, 0))
    return pl.pallas_call(
        _fox_kernel,
        grid=(batch, FOX_PAIRS),
        in_specs=[bound, bound,
                  pl.BlockSpec((LANES, seq), lambda bi, hp: (hp, bi)),
                  pl.BlockSpec((1, seq, LANES), lambda bi, hp: (bi, 0, hp)),
                  pl.BlockSpec((1, 1, seq, LANES), lambda bi, hp: (bi, hp, 0, 0)),
                  pl.BlockSpec((2 * V_ROWS, seq), lambda bi, hp: (hp, bi))],
        out_specs=pl.BlockSpec((1, seq, LANES), lambda bi, hp: (bi, 0, hp)),
        out_shape=jax.ShapeDtypeStruct((batch, seq, FOX_WIDTH), BF16),
        scratch_shapes=[pltpu.VMEM((2, 2, MXU_DEPTH, tq), BF16),
                        pltpu.VMEM((2, 1, tq), F32),
                        pltpu.VMEM((2, V_ROWS, tq), F32),
                        pltpu.VMEM((2, tk, tq), F32),
                        pltpu.VMEM((2, 1, tq), F32),
                        pltpu.SMEM((2,), jnp.int32)],
        compiler_params=_params("parallel", "parallel"),
        name="fox_attention",
    )(nb, kn, fqt, fk, cb, fvt)


def _score_bounds(pieces, kn2, batch, seq):
    nt = seq // ATTN_TILE
    negc = pieces.astype(F32).sum(axis=1)
    nb = lax.cummax(negc.reshape(batch, FOX_HEADS, nt, ATTN_TILE).max(axis=-1), axis=2) + BOUND_SLACK
    k2 = kn2[:, 0, :FOX_HEADS].reshape(batch, nt, FOX_HEADS)
    kn = lax.cummax(jnp.sqrt(k2).transpose(0, 2, 1), axis=2) * NORM_INFLATE
    rep = lambda b: jnp.broadcast_to(b.reshape(batch * FOX_HEADS, nt, 1), (batch * FOX_HEADS, nt, ATTN_TILE))
    return rep(nb), rep(kn)


def _ret_kernel(q_ref, k_ref, v_ref, g_ref, dm_ref, xi_ref, zeta_ref, cd_ref, gn_ref,
                o_ref, state_ref):
    @pl.when(pl.program_id(1) == 0)
    def _():
        state_ref[...] = jnp.zeros(state_ref.shape, F32)

    c = RET_CHUNK
    blocks = [(n, h) for n in range(RET_TILE // c) for h in range(RET_HEADS)]
    rows = lambda n: slice(n * c, (n + 1) * c)
    cols = lambda h: slice(h * RET_HEAD_DIM, (h + 1) * RET_HEAD_DIM)

    inner, kv = {}, {}
    for n, h in blocks:
        q, k = q_ref[0, rows(n), cols(h)], k_ref[0, rows(n), cols(h)]
        inner[n, h] = (lax.dot_general(q, k, _NT, preferred_element_type=F32) * dm_ref[h]).astype(BF16)
    for n, h in blocks:
        kz = (k_ref[0, rows(n), cols(h)].astype(F32) * zeta_ref[h]).astype(BF16)
        kv[n, h] = lax.dot_general(kz, v_ref[0, rows(n), cols(h)], (((0,), (0,)), ((), ())),
                                   preferred_element_type=F32)
    before = {}
    for h in range(RET_HEADS):
        st = state_ref[h]
        for n in range(RET_TILE // c):
            before[n, h] = st.astype(BF16)
            st = cd_ref[h] * st + kv[n, h]
        state_ref[h] = st
    for n, h in blocks:
        q, v = q_ref[0, rows(n), cols(h)], v_ref[0, rows(n), cols(h)]
        o = (jnp.dot(inner[n, h], v, preferred_element_type=F32)
             + xi_ref[h] * jnp.dot(q, before[n, h], preferred_element_type=F32))
        mu = jnp.mean(o, axis=-1, keepdims=True)
        d = o - mu
        var = jnp.mean(d * d, axis=-1, keepdims=True)
        y = d * lax.rsqrt(var + GN_EPS) * gn_ref[h:h + 1, :]
        g = g_ref[0, rows(n), cols(h)]
        o_ref[0, rows(n), cols(h)] = (g * jax.nn.sigmoid(g) * y).astype(o_ref.dtype)


def _retention(rq, rk, rv, rg, dm, xi, zeta, cd, gn):
    b, s, _ = rq.shape
    t = RET_TILE
    tok = pl.BlockSpec((1, t, RET_WIDTH), lambda bi, si: (bi, si, 0))
    tab = pl.BlockSpec((RET_HEADS, RET_CHUNK, RET_HEAD_DIM), lambda bi, si: (0, 0, 0))
    return pl.pallas_call(
        _ret_kernel,
        grid=(b, s // t),
        in_specs=[tok, tok, tok, tok, tab, tab, tab, tab,
                  pl.BlockSpec((RET_HEADS, RET_HEAD_DIM), lambda bi, si: (0, 0))],
        out_specs=tok,
        out_shape=jax.ShapeDtypeStruct((b, s, RET_WIDTH), BF16),
        scratch_shapes=[pltpu.VMEM((RET_HEADS, RET_HEAD_DIM, RET_HEAD_DIM), F32)],
        compiler_params=_params("parallel", "arbitrary"),
        name="retention",
    )(rq, rk, rv, rg, dm, xi, zeta, cd, gn)


def _layernorm(y, g, b):
    mu = jnp.mean(y, axis=-1, keepdims=True)
    d = y - mu
    var = jnp.mean(d * d, axis=-1, keepdims=True)
    return d * lax.rsqrt(var + LN_EPS) * g + b


def _tail_kernel(alpha, fox_ref, ret_ref, x_ref, wa_ref, wb_ref, w1_ref, w2_ref,
                 g1_ref, b1_ref, g2_ref, b2_ref, o_ref):
    mix = (jnp.dot(fox_ref[...], wa_ref[...], preferred_element_type=F32)
           + jnp.dot(ret_ref[...], wb_ref[...], preferred_element_type=F32))
    x = _layernorm(alpha * x_ref[...] + mix, g1_ref[...], b1_ref[...])
    xb = x.astype(BF16)
    acc = alpha * x
    for c in range(D_FF // FF_CHUNK):
        cols = slice(c * FF_CHUNK, (c + 1) * FF_CHUNK)
        h = jnp.maximum(jnp.dot(xb, w1_ref[:, cols], preferred_element_type=F32), 0.0)
        acc = acc + jnp.dot((h * h).astype(BF16), w2_ref[cols, :], preferred_element_type=F32)
    o_ref[...] = _layernorm(acc, g2_ref[...], b2_ref[...])


def _tail(alpha, fox, ret, x2, wa, wb, w1, w2, g1, b1, g2, b2):
    n = x2.shape[0]
    tm = TOKEN_TILE
    row = lambda i: (i, 0)
    const = lambda i: (0, 0)
    resident = lambda shape: pl.BlockSpec(shape, const, pipeline_mode=pl.Buffered(1))
    vec = pl.BlockSpec((1, D_MODEL), const)
    return pl.pallas_call(
        functools.partial(_tail_kernel, alpha),
        grid=(n // tm,),
        in_specs=[pl.BlockSpec((tm, FOX_WIDTH), row),
                  pl.BlockSpec((tm, RET_WIDTH), row),
                  pl.BlockSpec((tm, D_MODEL), row),
                  resident((FOX_WIDTH, D_MODEL)),
                  resident((RET_WIDTH, D_MODEL)),
                  resident((D_MODEL, D_FF)),
                  resident((D_FF, D_MODEL)),
                  vec, vec, vec, vec],
        out_specs=pl.BlockSpec((tm, D_MODEL), row),
        out_shape=jax.ShapeDtypeStruct((n, D_MODEL), F32),
        compiler_params=_params("parallel"),
        name="out_proj_ffn",
    )(fox, ret, x2, wa, wb, w1, w2, g1, b1, g2, b2)


def _rotary_tables(seq):
    half = RET_HEAD_DIM // 2
    inv_freq = ROPE_BASE ** (-jnp.arange(half, dtype=F32) / half)
    ang = jnp.arange(seq, dtype=F32)[:, None] * inv_freq[None, :]
    cos, sin = jnp.cos(ang), jnp.sin(ang)
    return jnp.concatenate([cos, cos], axis=-1), jnp.concatenate([-sin, sin], axis=-1)


def _decay_tables():
    gamma = 1.0 - jnp.exp2(-5.0 - jnp.arange(RET_HEADS, dtype=F32))
    log_g = jnp.log(gamma)
    idx = jnp.arange(RET_CHUNK, dtype=F32)
    diff = idx[:, None] - idx[None, :]
    dm = jnp.where(diff >= 0, jnp.exp(log_g[:, None, None] * jnp.maximum(diff, 0.0)), 0.0)
    shape = (RET_HEADS, RET_CHUNK, RET_HEAD_DIM)
    xi = jnp.broadcast_to(jnp.exp(log_g[:, None] * (idx + 1.0))[..., None], shape)
    zeta = jnp.broadcast_to(jnp.exp(log_g[:, None] * (RET_CHUNK - 1.0 - idx))[..., None], shape)
    cd = jnp.broadcast_to(jnp.exp(log_g * RET_CHUNK)[:, None, None], shape)
    return dm, xi, zeta, cd


def _pack_w_in(w, order):
    d = w.shape[0]
    by_head = lambda m: m.reshape(d, FOX_HEADS, FOX_HEAD_DIM)[:, order].reshape(d, FOX_WIDTH)
    wq, wk, wv = (by_head(w[:, i * FOX_WIDTH:(i + 1) * FOX_WIDTH]) for i in range(3))
    lo = 3 * FOX_WIDTH
    logits_t = jnp.pad(w[:, lo:lo + FOX_HEADS][:, order].T, ((0, LOGIT_ROWS - FOX_HEADS), (0, 0)))
    packed = jnp.concatenate([wk, w[:, lo + FOX_HEADS:]], axis=1).astype(BF16)
    wv_t = wv.T.reshape(FOX_HEADS, FOX_HEAD_DIM, d)
    wv_t = jnp.pad(wv_t, ((0, 0), (0, V_ROWS - FOX_HEAD_DIM), (0, 0))).reshape(FOX_HEADS * V_ROWS, d)
    packed_t = jnp.concatenate([wq.T, wv_t, logits_t], axis=0).astype(BF16)
    return packed, packed_t


def _bias_operand(pieces, batch, seq):
    p = pieces.reshape(batch, BIAS_PIECES, FOX_PAIRS, 2, seq)
    p = p.transpose(0, 2, 4, 3, 1).reshape(batch, FOX_PAIRS, seq, 2 * BIAS_PIECES)
    return jnp.pad(p, ((0, 0), (0, 0), (0, 0), (0, LANES - 2 * BIAS_PIECES)))


@jax.jit
def kernel(x, w_in, w_out, w_ff1, w_ff2, ln1_g, ln1_b, ln2_g, ln2_b, b_forget, ret_gn_g):
    batch, seq, _ = x.shape
    depth = w_in.shape[0]
    alpha = (2 * depth) ** 0.25
    cs, sn = _rotary_tables(seq)
    dm, xi, zeta, cd = _decay_tables()
    x2 = x.reshape(batch * seq, D_MODEL)
    for layer in range(depth):
        order = jnp.argsort(b_forget[layer])
        w_packed, wt_packed = _pack_w_in(w_in[layer], order)
        fqt, fk, fvt, lf_t, kn2, rq, rk, rv, rg = _inproj(x2, w_packed, wt_packed, b_forget[layer][order],
                                                          cs, sn, seq)
        seq3 = lambda t: t.reshape(batch, seq, t.shape[-1])
        lf = lf_t[:FOX_HEADS].reshape(FOX_HEADS, batch, seq).transpose(1, 0, 2)
        pieces = _neg_cumsum_pieces(lf)
        nb, kn = _score_bounds(pieces, kn2, batch, seq)
        fox = _fox_attention(nb, kn, fqt, seq3(fk), _bias_operand(pieces, batch, seq), fvt, batch, seq)
        ret = _retention(seq3(rq), seq3(rk), seq3(rv), seq3(rg), dm, xi, zeta, cd, ret_gn_g[layer])
        wo = w_out[layer].astype(BF16)
        wo_fox = wo[:FOX_WIDTH].reshape(FOX_HEADS, FOX_HEAD_DIM, D_MODEL)[order].reshape(FOX_WIDTH, D_MODEL)
        vec = lambda p: p[layer].reshape(1, D_MODEL)
        x2 = _tail(alpha, fox.reshape(batch * seq, FOX_WIDTH), ret.reshape(batch * seq, RET_WIDTH), x2,
                   wo_fox, wo[FOX_WIDTH:], w_ff1[layer].astype(BF16), w_ff2[layer].astype(BF16),
                   vec(ln1_g), vec(ln1_b), vec(ln2_g), vec(ln2_b))
    return x2.reshape(batch, seq, D_MODEL)
```

```python
import functools
import math

import jax
import jax.numpy as jnp
from jax import lax
from jax.experimental import pallas as pl
from jax.experimental.pallas import tpu as pltpu

F32 = jnp.float32
BF16 = jnp.bfloat16

D_MODEL = 1024
FOX_HEADS = 8
FOX_HEAD_DIM = 64
FOX_WIDTH = FOX_HEADS * FOX_HEAD_DIM
RET_HEADS = 4
RET_HEAD_DIM = 128
RET_WIDTH = RET_HEADS * RET_HEAD_DIM
D_FF = 4 * D_MODEL
RET_CHUNK = 128
ROPE_BASE = 10000.0
LN_EPS = 1e-5
GN_EPS = 1e-5
LOG2E = math.log2(math.e)

LANES = 128
MXU_DEPTH = 256
VMEM_LIMIT = 56 * 1024 * 1024

TOKEN_TILE = 512
ATTN_TILE = 512
ATTN_UNROLL = 4
FIXED_REF_GAP = 100.0
ZERO_WEIGHT_LOG2 = 135.0
BOUND_SLACK = 2.0
NORM_INFLATE = 1.01
RET_TILE = 512
TAIL_ROW_GROUPS = 2

FOX_PAIRS = FOX_HEADS // 2
V_ROWS = 80
BIAS_PIECES = 3

_OFF_FK = 0
_OFF_RQ = FOX_WIDTH
_OFF_RK, _OFF_RV, _OFF_RG = _OFF_RQ + RET_WIDTH, _OFF_RQ + 2 * RET_WIDTH, _OFF_RQ + 3 * RET_WIDTH
PACKED_COLS = _OFF_RQ + 4 * RET_WIDTH
LOGIT_ROWS = 16
_ROW_FV = FOX_WIDTH
_ROW_FL = _ROW_FV + FOX_HEADS * V_ROWS
PACKED_ROWS = _ROW_FL + LOGIT_ROWS

_NT = (((1,), (1,)), ((), ()))


def _params(*semantics):
    return pltpu.CompilerParams(dimension_semantics=semantics, vmem_limit_bytes=VMEM_LIMIT)


def _inproj_kernel(x_ref, w_ref, wt_ref, bf_ref, cs_ref, sn_ref, ones_ref,
                   fqt_ref, fk_ref, fvt_ref, lf_ref, kn2_ref, rq_ref, rk_ref, rv_ref, rg_ref):
    xb = x_ref[...].astype(BF16)

    projected = jnp.dot(xb, w_ref[...], preferred_element_type=F32)

    def proj(lo, width):
        return projected[:, lo:lo + width]

    transposed = lax.dot_general(wt_ref[...], xb, _NT, preferred_element_type=F32)

    def proj_t(lo, rows):
        return transposed[lo:lo + rows, :]

    fqt_ref[...] = (proj_t(0, FOX_WIDTH) * (FOX_HEAD_DIM ** -0.5 * LOG2E)).astype(BF16)
    kb = proj(_OFF_FK, FOX_WIDTH).astype(BF16)
    fk_ref[...] = kb
    kf = kb.astype(F32)
    col = lax.broadcasted_iota(jnp.int32, (FOX_WIDTH, LANES), 0)
    lane = lax.broadcasted_iota(jnp.int32, (FOX_WIDTH, LANES), 1)
    sel = (col // FOX_HEAD_DIM == lane).astype(BF16)
    kn2 = jnp.dot((kf * kf).astype(BF16), sel, preferred_element_type=F32)
    kn2_ref[0] = jnp.broadcast_to(jnp.max(kn2, axis=0, keepdims=True), kn2_ref.shape[1:])
    vt = proj_t(_ROW_FV, FOX_HEADS * V_ROWS)
    fvt_ref[...] = jnp.where(ones_ref[...] != 0.0, 1.0, vt).astype(BF16)
    rv_ref[...] = proj(_OFF_RV, RET_WIDTH).astype(BF16)
    rg_ref[...] = proj(_OFF_RG, RET_WIDTH)

    z = proj_t(_ROW_FL, LOGIT_ROWS) + bf_ref[...]
    lf_ref[...] = -(jnp.maximum(-z, 0.0) + jnp.log1p(jnp.exp(-jnp.abs(z))))

    cs = cs_ref[...]
    sn = sn_ref[...]

    def rotary(t, scale):
        for h in range(RET_HEADS):
            th = t[:, h * RET_HEAD_DIM:(h + 1) * RET_HEAD_DIM]
            r = th * cs + pltpu.roll(th, RET_HEAD_DIM // 2, 1) * sn
            if scale is not None:
                r = r * scale
            yield h, r.astype(BF16)

    for h, r in rotary(proj(_OFF_RQ, RET_WIDTH), None):
        rq_ref[:, h * RET_HEAD_DIM:(h + 1) * RET_HEAD_DIM] = r
    for h, r in rotary(proj(_OFF_RK, RET_WIDTH), RET_HEAD_DIM ** -0.5):
        rk_ref[:, h * RET_HEAD_DIM:(h + 1) * RET_HEAD_DIM] = r


def _inproj(x2, w_packed, wt_packed, b_forget, cs, sn, seq):
    n = x2.shape[0]
    tm = TOKEN_TILE
    assert tm == ATTN_TILE, "the key-norm output is one row per attention key tile"
    bf_rows = jnp.broadcast_to(jnp.pad(b_forget, (0, LOGIT_ROWS - FOX_HEADS))[:, None], (LOGIT_ROWS, tm))
    pos_blocks = seq // tm
    row = lambda i: (i, 0)
    col = lambda i: (0, i)
    const = lambda i: (0, 0)
    pos = lambda i: (i % pos_blocks, 0)
    wide = lambda w, dt: jax.ShapeDtypeStruct((n, w), dt)
    tall = lambda r: jax.ShapeDtypeStruct((r, n), BF16)
    return pl.pallas_call(
        _inproj_kernel,
        grid=(n // tm,),
        in_specs=[pl.BlockSpec((tm, D_MODEL), row),
                  pl.BlockSpec((D_MODEL, PACKED_COLS), const),
                  pl.BlockSpec((PACKED_ROWS, D_MODEL), const),
                  pl.BlockSpec((LOGIT_ROWS, tm), const),
                  pl.BlockSpec((tm, LANES), pos),
                  pl.BlockSpec((tm, LANES), pos),
                  pl.BlockSpec((FOX_HEADS * V_ROWS, tm), const)],
        out_specs=[pl.BlockSpec((FOX_WIDTH, tm), col),
                   pl.BlockSpec((tm, FOX_WIDTH), row),
                   pl.BlockSpec((FOX_HEADS * V_ROWS, tm), col),
                   pl.BlockSpec((LOGIT_ROWS, tm), col),
                   pl.BlockSpec((1, 8, LANES), lambda i: (i, 0, 0))]
                  + [pl.BlockSpec((tm, RET_WIDTH), row)] * 4,
        out_shape=[tall(FOX_WIDTH), wide(FOX_WIDTH, BF16), tall(FOX_HEADS * V_ROWS),
                   jax.ShapeDtypeStruct((LOGIT_ROWS, n), F32),
                   jax.ShapeDtypeStruct((n // tm, 8, LANES), F32)]
                  + [wide(RET_WIDTH, BF16)] * 3 + [wide(RET_WIDTH, F32)],
        compiler_params=_params("parallel"),
        name="in_proj",
    )(x2, w_packed, wt_packed, bf_rows, cs, sn, _ones_rows(tm))


def _ones_rows(width):
    row = jnp.arange(FOX_HEADS * V_ROWS) % V_ROWS
    return jnp.broadcast_to((row == FOX_HEAD_DIM).astype(F32)[:, None], (FOX_HEADS * V_ROWS, width))


def _split3(t):
    hi = t.astype(BF16)
    r1 = t - hi.astype(F32)
    mid = r1.astype(BF16)
    lo = (r1 - mid.astype(F32)).astype(BF16)
    return hi, mid, lo


def _dot3(pieces, m):
    return sum(jnp.dot(p, m, preferred_element_type=F32) for p in pieces)


def _cumsum_kernel(x_ref, o_ref):
    heads, rows, lanes = x_ref.shape[1:]
    j = lax.broadcasted_iota(jnp.int32, (lanes, lanes), 0)
    l = lax.broadcasted_iota(jnp.int32, (lanes, lanes), 1)
    upper = (j <= l).astype(BF16)
    ones = jnp.ones((lanes, lanes), BF16)
    r_out = lax.broadcasted_iota(jnp.int32, (rows, rows), 0)
    r_in = lax.broadcasted_iota(jnp.int32, (rows, rows), 1)
    strict = (r_in < r_out).astype(BF16)
    for h in range(heads):
        pieces = _split3(x_ref[0, h] * (-LOG2E))
        within = _dot3(pieces, upper)
        totals = _dot3(pieces, ones)
        offs = sum(jnp.dot(strict, p, preferred_element_type=F32) for p in _split3(totals))
        for t, piece in enumerate(_split3(within + offs)):
            o_ref[0, t, h] = piece


def _neg_cumsum_pieces(lf):
    b, h, s = lf.shape
    rows = s // LANES
    return pl.pallas_call(
        _cumsum_kernel,
        grid=(b,),
        in_specs=[pl.BlockSpec((1, h, rows, LANES), lambda i: (i, 0, 0, 0))],
        out_specs=pl.BlockSpec((1, BIAS_PIECES, h, rows, LANES), lambda i: (i, 0, 0, 0, 0)),
        out_shape=jax.ShapeDtypeStruct((b, BIAS_PIECES, h, rows, LANES), BF16),
        compiler_params=_params("parallel"),
        name="forget_cumsum",
    )(lf.reshape(b, h, rows, LANES)).reshape(b, BIAS_PIECES, h, s)


def _fox_kernel(nb_ref, kn_ref, qt_ref, k_ref, cb_ref, vt_ref, o_ref,
                qaug_ref, m_ref, acc_ref, s_ref, cmax_ref, plan_ref):
    t = ATTN_TILE
    n_tiles = qt_ref.shape[1] // t
    refs = (nb_ref, kn_ref, qt_ref, k_ref, cb_ref, vt_ref, o_ref, qaug_ref, m_ref, acc_ref, s_ref, cmax_ref,
            plan_ref)
    _fox_stage_diagonal(0, 0, *refs)

    def query_tile(qi, carry):
        _fox_query_tile(qi, n_tiles, *refs)
        return carry

    lax.fori_loop(0, n_tiles, query_tile, 0)


def _fox_query_operand(qi, slot, qt_ref, qaug_ref):
    t = ATTN_TILE
    qt = qt_ref[:, pl.ds(pl.multiple_of(qi * t, t), t)]
    sub = lax.broadcasted_iota(jnp.int32, (LANES, t), 0)
    for a in range(2):
        mine = (sub >= a * FOX_HEAD_DIM) & (sub < (a + 1) * FOX_HEAD_DIM)
        qaug_ref[slot, a, 0:LANES, :] = jnp.where(mine, qt, jnp.zeros_like(qt))
        pick = (sub >= a * BIAS_PIECES) & (sub < (a + 1) * BIAS_PIECES)
        qaug_ref[slot, a, LANES:MXU_DEPTH, :] = jnp.where(pick, 1.0, 0.0).astype(BF16)


def _fox_scores(qi, slot, ki, masked, k_ref, cb_ref, qaug_ref, s_ref, cmax_ref):
    t = ATTN_TILE
    start = pl.multiple_of(ki * t, t)
    kaug = jnp.concatenate([k_ref[0, pl.ds(start, t), :], cb_ref[0, 0, pl.ds(start, t), :]], axis=1)
    if masked:
        key = lax.broadcasted_iota(jnp.int32, (t, t), 0) + ki * t
        qry = lax.broadcasted_iota(jnp.int32, (t, t), 1) + qi * t
        keep = key <= qry
    for a in range(2):
        s = jnp.dot(kaug, qaug_ref[slot, a], preferred_element_type=F32)
        if masked:
            s = jnp.where(keep, s, -jnp.inf)
        s_ref[a] = s
        cmax_ref[a] = jnp.max(s, axis=0, keepdims=True)


def _fox_stage_diagonal(qi, slot, nb_ref, kn_ref, qt_ref, k_ref, cb_ref, vt_ref, o_ref,
                        qaug_ref, m_ref, acc_ref, s_ref, cmax_ref, plan_ref):
    t = ATTN_TILE
    _fox_query_operand(qi, slot, qt_ref, qaug_ref)
    _fox_scores(qi, slot, qi, True, k_ref, cb_ref, qaug_ref, s_ref, cmax_ref)
    qt = qt_ref[:, pl.ds(pl.multiple_of(qi * t, t), t)].astype(F32)
    q2 = qt * qt
    need, fixed = None, None
    newest_older = jnp.maximum(qi - 1, 0)
    for a in range(2):
        qnorm = jnp.sqrt(jnp.sum(q2[a * FOX_HEAD_DIM:(a + 1) * FOX_HEAD_DIM], axis=0, keepdims=True))
        seen = cmax_ref[a]
        bound = nb_ref[a] + qnorm * kn_ref[a] - seen
        hit = jnp.max(bound, axis=1, keepdims=True) >= -ZERO_WEIGHT_LOG2
        need = hit if need is None else need | hit
        top = (nb_ref[a, pl.ds(newest_older, 1), :]
               + qnorm * kn_ref[a, pl.ds(newest_older, 1), :] - seen)
        ok = jnp.max(top) <= FIXED_REF_GAP
        fixed = ok if fixed is None else fixed & ok
    older = lax.broadcasted_iota(jnp.int32, need.shape, 0) < qi
    plan_ref[0] = jnp.sum((need & older).astype(jnp.int32))
    plan_ref[1] = fixed.astype(jnp.int32)


def _fox_query_tile(qi, n_tiles, nb_ref, kn_ref, qt_ref, k_ref, cb_ref, vt_ref, o_ref,
                    qaug_ref, m_ref, acc_ref, s_ref, cmax_ref, plan_ref):
    t = tq = tk = ATTN_TILE
    slot = qi % 2
    q_start = pl.multiple_of(qi * t, t)
    n = plan_ref[0]
    fixed = plan_ref[1] != 0
    m_ref[...] = jnp.full(m_ref.shape, -jnp.inf, F32)
    acc_ref[...] = jnp.zeros(acc_ref.shape, F32)

    def scores(ki, masked):
        _fox_scores(qi, slot, ki, masked, k_ref, cb_ref, qaug_ref, s_ref, cmax_ref)

    def weights():
        out = []
        for a in range(2):
            m_prev = m_ref[a]
            m_new = jnp.maximum(m_prev, cmax_ref[a])
            out.append((jnp.exp2(s_ref[a] - m_new).astype(BF16), jnp.exp2(m_prev - m_new)))
            m_ref[a] = m_new
        return out

    def values(ki, pw):
        start = pl.multiple_of(ki * tk, tk)
        for a, (p, alpha) in enumerate(pw):
            vt = vt_ref[a * V_ROWS:(a + 1) * V_ROWS, pl.ds(start, tk)]
            acc_ref[a] = alpha * acc_ref[a] + jnp.dot(vt, p, preferred_element_type=F32)

    def step(k, masked=False):
        pw = weights()
        scores(k - 1, masked)
        values(k, pw)

    def body(g, carry):
        for u in range(ATTN_UNROLL):
            step(oldest_single - (g * ATTN_UNROLL + u))
        return carry

    def fixed_weights(ki):
        start = pl.multiple_of(ki * tk, tk)
        kaug = jnp.concatenate([k_ref[0, pl.ds(start, tk), :], cb_ref[0, 0, pl.ds(start, tk), :]], axis=1)
        return [jnp.exp2(jnp.dot(kaug, qaug_ref[slot, a], preferred_element_type=F32) - m_ref[a]).astype(BF16)
                for a in range(2)]

    def fixed_values(ki, ps):
        start = pl.multiple_of(ki * tk, tk)
        return [jnp.dot(vt_ref[a * V_ROWS:(a + 1) * V_ROWS, pl.ds(start, tk)], ps[a],
                        preferred_element_type=F32) for a in range(2)]

    def fixed_tiles(k0, count):
        ps = fixed_weights(k0)
        sums = None
        for u in range(count):
            nxt = fixed_weights(k0 - u - 1) if u + 1 < count else None
            pv = fixed_values(k0 - u, ps)
            sums = pv if sums is None else [x + y for x, y in zip(sums, pv)]
            ps = nxt
        for a in range(2):
            acc_ref[a] = acc_ref[a] + sums[a]

    def fixed_body(g, carry):
        fixed_tiles(first - 1 - singles - g * ATTN_UNROLL, ATTN_UNROLL)
        return carry

    first = qi
    groups = n // ATTN_UNROLL
    singles = n - groups * ATTN_UNROLL
    running_singles = jnp.where(fixed, 0, singles)
    for u in range(ATTN_UNROLL - 1):
        @pl.when(u < running_singles)
        def _():
            step(first - u)
    oldest_single = first - running_singles
    running_groups = jnp.where(fixed, 0, groups)
    lax.fori_loop(0, running_groups, body, 0)
    pw = weights()
    upcoming = jnp.minimum(qi + 1, n_tiles - 1)
    _fox_stage_diagonal(upcoming, 1 - slot, nb_ref, kn_ref, qt_ref, k_ref, cb_ref, vt_ref, o_ref,
                        qaug_ref, m_ref, acc_ref, s_ref, cmax_ref, plan_ref)
    values(oldest_single - running_groups * ATTN_UNROLL, pw)
    for count in range(1, ATTN_UNROLL):
        @pl.when(fixed & (singles == count))
        def _():
            fixed_tiles(first - 1, count)
    lax.fori_loop(0, groups - running_groups, fixed_body, 0)

    outs = [acc_ref[a, 0:FOX_HEAD_DIM, :] / acc_ref[a, FOX_HEAD_DIM:FOX_HEAD_DIM + 1, :] for a in range(2)]
    o_ref[0, pl.ds(q_start, tq), :] = jnp.concatenate(outs, axis=0).T.astype(o_ref.dtype)


def _fox_attention(nb, kn, fqt, fk, cb, fvt, batch, seq):
    tq = tk = ATTN_TILE
    bound = pl.BlockSpec((2, seq // tk, tq), lambda bi, hp: (bi * FOX_PAIRS + hp, 0, 0))
    return pl.pallas_call(
        _fox_kernel,
        grid=(batch, FOX_PAIRS),
        in_specs=[bound, bound,
                  pl.BlockSpec((LANES, seq), lambda bi, hp: (hp, bi)),
                  pl.BlockSpec((1, seq, LANES), lambda bi, hp: (bi, 0, hp)),
                  pl.BlockSpec((1, 1, seq, LANES), lambda bi, hp: (bi, hp, 0, 0)),
                  pl.BlockSpec((2 * V_ROWS, seq), lambda bi, hp: (hp, bi))],
        out_specs=pl.BlockSpec((1, seq, LANES), lambda bi, hp: (bi, 0, hp)),
        out_shape=jax.ShapeDtypeStruct((batch, seq, FOX_WIDTH), BF16),
        scratch_shapes=[pltpu.VMEM((2, 2, MXU_DEPTH, tq), BF16),
                        pltpu.VMEM((2, 1, tq), F32),
                        pltpu.VMEM((2, V_ROWS, tq), F32),
                        pltpu.VMEM((2, tk, tq), F32),
                        pltpu.VMEM((2, 1, tq), F32),
                        pltpu.SMEM((2,), jnp.int32)],
        compiler_params=_params("parallel", "parallel"),
        name="fox_attention",
    )(nb, kn, fqt, fk, cb, fvt)


def _score_bounds(pieces, kn2, batch, seq):
    nt = seq // ATTN_TILE
    negc = pieces.astype(F32).sum(axis=1)
    nb = lax.cummax(negc.reshape(batch, FOX_HEADS, nt, ATTN_TILE).max(axis=-1), axis=2) + BOUND_SLACK
    k2 = kn2[:, 0, :FOX_HEADS].reshape(batch, nt, FOX_HEADS)
    kn = lax.cummax(jnp.sqrt(k2).transpose(0, 2, 1), axis=2) * NORM_INFLATE
    rep = lambda b: jnp.broadcast_to(b.reshape(batch * FOX_HEADS, nt, 1), (batch * FOX_HEADS, nt, ATTN_TILE))
    return rep(nb), rep(kn)


def _ret_kernel(q_ref, k_ref, v_ref, g_ref, dm_ref, xi_ref, zeta_ref, cd_ref, gn_ref,
                o_ref, state_ref):
    @pl.when(pl.program_id(1) == 0)
    def _():
        state_ref[...] = jnp.zeros(state_ref.shape, F32)

    c = RET_CHUNK
    blocks = [(n, h) for n in range(RET_TILE // c) for h in range(RET_HEADS)]
    rows = lambda n: slice(n * c, (n + 1) * c)
    cols = lambda h: slice(h * RET_HEAD_DIM, (h + 1) * RET_HEAD_DIM)

    inner, kv = {}, {}
    for n, h in blocks:
        q, k = q_ref[0, rows(n), cols(h)], k_ref[0, rows(n), cols(h)]
        inner[n, h] = (lax.dot_general(q, k, _NT, preferred_element_type=F32) * dm_ref[h]).astype(BF16)
    for n, h in blocks:
        kz = (k_ref[0, rows(n), cols(h)].astype(F32) * zeta_ref[h]).astype(BF16)
        kv[n, h] = lax.dot_general(kz, v_ref[0, rows(n), cols(h)], (((0,), (0,)), ((), ())),
                                   preferred_element_type=F32)
    before = {}
    for h in range(RET_HEADS):
        st = state_ref[h]
        for n in range(RET_TILE // c):
            before[n, h] = st.astype(BF16)
            st = cd_ref[h] * st + kv[n, h]
        state_ref[h] = st
    for n, h in blocks:
        q, v = q_ref[0, rows(n), cols(h)], v_ref[0, rows(n), cols(h)]
        o = (jnp.dot(inner[n, h], v, preferred_element_type=F32)
             + xi_ref[h] * jnp.dot(q, before[n, h], preferred_element_type=F32))
        mu = jnp.mean(o, axis=-1, keepdims=True)
        d = o - mu
        var = jnp.mean(d * d, axis=-1, keepdims=True)
        y = d * lax.rsqrt(var + GN_EPS) * gn_ref[h:h + 1, :]
        g = g_ref[0, rows(n), cols(h)]
        o_ref[0, rows(n), cols(h)] = (g * jax.nn.sigmoid(g) * y).astype(o_ref.dtype)


def _retention(rq, rk, rv, rg, dm, xi, zeta, cd, gn):
    b, s, _ = rq.shape
    t = RET_TILE
    tok = pl.BlockSpec((1, t, RET_WIDTH), lambda bi, si: (bi, si, 0))
    tab = pl.BlockSpec((RET_HEADS, RET_CHUNK, RET_HEAD_DIM), lambda bi, si: (0, 0, 0))
    return pl.pallas_call(
        _ret_kernel,
        grid=(b, s // t),
        in_specs=[tok, tok, tok, tok, tab, tab, tab, tab,
                  pl.BlockSpec((RET_HEADS, RET_HEAD_DIM), lambda bi, si: (0, 0))],
        out_specs=tok,
        out_shape=jax.ShapeDtypeStruct((b, s, RET_WIDTH), BF16),
        scratch_shapes=[pltpu.VMEM((RET_HEADS, RET_HEAD_DIM, RET_HEAD_DIM), F32)],
        compiler_params=_params("parallel", "arbitrary"),
        name="retention",
    )(rq, rk, rv, rg, dm, xi, zeta, cd, gn)


def _layernorm(y, g, b):
    mu = jnp.mean(y, axis=-1, keepdims=True)
    d = y - mu
    var = jnp.mean(d * d, axis=-1, keepdims=True)
    return d * lax.rsqrt(var + LN_EPS) * g + b


def _tail_kernel(alpha, fox_ref, ret_ref, x_ref, wa_ref, wb_ref, w1_ref, w2_ref,
                 g1_ref, b1_ref, g2_ref, b2_ref, o_ref):
    size = TOKEN_TILE // TAIL_ROW_GROUPS
    groups = [slice(r * size, (r + 1) * size) for r in range(TAIL_ROW_GROUPS)]
    mix = [jnp.dot(fox_ref[g, :], wa_ref[...], preferred_element_type=F32)
           + jnp.dot(ret_ref[g, :], wb_ref[...], preferred_element_type=F32) for g in groups]
    x = [_layernorm(alpha * x_ref[g, :] + m, g1_ref[...], b1_ref[...]) for g, m in zip(groups, mix)]
    h = [jnp.maximum(jnp.dot(xg.astype(BF16), w1_ref[...], preferred_element_type=F32), 0.0) for xg in x]
    ff = [jnp.dot((hg * hg).astype(BF16), w2_ref[...], preferred_element_type=F32) for hg in h]
    for g, xg, fg in zip(groups, x, ff):
        o_ref[g, :] = _layernorm(alpha * xg + fg, g2_ref[...], b2_ref[...])


def _tail(alpha, fox, ret, x2, wa, wb, w1, w2, g1, b1, g2, b2):
    n = x2.shape[0]
    tm = TOKEN_TILE
    row = lambda i: (i, 0)
    const = lambda i: (0, 0)
    resident = lambda shape: pl.BlockSpec(shape, const, pipeline_mode=pl.Buffered(1))
    vec = pl.BlockSpec((1, D_MODEL), const)
    return pl.pallas_call(
        functools.partial(_tail_kernel, alpha),
        grid=(n // tm,),
        in_specs=[pl.BlockSpec((tm, FOX_WIDTH), row),
                  pl.BlockSpec((tm, RET_WIDTH), row),
                  pl.BlockSpec((tm, D_MODEL), row),
                  resident((FOX_WIDTH, D_MODEL)),
                  resident((RET_WIDTH, D_MODEL)),
                  resident((D_MODEL, D_FF)),
                  resident((D_FF, D_MODEL)),
                  vec, vec, vec, vec],
        out_specs=pl.BlockSpec((tm, D_MODEL), row),
        out_shape=jax.ShapeDtypeStruct((n, D_MODEL), F32),
        compiler_params=_params("parallel"),
        name="out_proj_ffn",
    )(fox, ret, x2, wa, wb, w1, w2, g1, b1, g2, b2)


def _rotary_tables(seq):
    half = RET_HEAD_DIM // 2
    inv_freq = ROPE_BASE ** (-jnp.arange(half, dtype=F32) / half)
    ang = jnp.arange(seq, dtype=F32)[:, None] * inv_freq[None, :]
    cos, sin = jnp.cos(ang), jnp.sin(ang)
    return jnp.concatenate([cos, cos], axis=-1), jnp.concatenate([-sin, sin], axis=-1)


def _decay_tables():
    gamma = 1.0 - jnp.exp2(-5.0 - jnp.arange(RET_HEADS, dtype=F32))
    log_g = jnp.log(gamma)
    idx = jnp.arange(RET_CHUNK, dtype=F32)
    diff = idx[:, None] - idx[None, :]
    dm = jnp.where(diff >= 0, jnp.exp(log_g[:, None, None] * jnp.maximum(diff, 0.0)), 0.0)
    shape = (RET_HEADS, RET_CHUNK, RET_HEAD_DIM)
    xi = jnp.broadcast_to(jnp.exp(log_g[:, None] * (idx + 1.0))[..., None], shape)
    zeta = jnp.broadcast_to(jnp.exp(log_g[:, None] * (RET_CHUNK - 1.0 - idx))[..., None], shape)
    cd = jnp.broadcast_to(jnp.exp(log_g * RET_CHUNK)[:, None, None], shape)
    return dm, xi, zeta, cd


def _pack_w_in(w, order):
    d = w.shape[0]
    by_head = lambda m: m.reshape(d, FOX_HEADS, FOX_HEAD_DIM)[:, order].reshape(d, FOX_WIDTH)
    wq, wk, wv = (by_head(w[:, i * FOX_WIDTH:(i + 1) * FOX_WIDTH]) for i in range(3))
    lo = 3 * FOX_WIDTH
    logits_t = jnp.pad(w[:, lo:lo + FOX_HEADS][:, order].T, ((0, LOGIT_ROWS - FOX_HEADS), (0, 0)))
    packed = jnp.concatenate([wk, w[:, lo + FOX_HEADS:]], axis=1).astype(BF16)
    wv_t = wv.T.reshape(FOX_HEADS, FOX_HEAD_DIM, d)
    wv_t = jnp.pad(wv_t, ((0, 0), (0, V_ROWS - FOX_HEAD_DIM), (0, 0))).reshape(FOX_HEADS * V_ROWS, d)
    packed_t = jnp.concatenate([wq.T, wv_t, logits_t], axis=0).astype(BF16)
    return packed, packed_t


def _bias_operand(pieces, batch, seq):
    p = pieces.reshape(batch, BIAS_PIECES, FOX_PAIRS, 2, seq)
    p = p.transpose(0, 2, 4, 3, 1).reshape(batch, FOX_PAIRS, seq, 2 * BIAS_PIECES)
    return jnp.pad(p, ((0, 0), (0, 0), (0, 0), (0, LANES - 2 * BIAS_PIECES)))


@jax.jit
def kernel(x, w_in, w_out, w_ff1, w_ff2, ln1_g, ln1_b, ln2_g, ln2_b, b_forget, ret_gn_g):
    batch, seq, _ = x.shape
    depth = w_in.shape[0]
    alpha = (2 * depth) ** 0.25
    cs, sn = _rotary_tables(seq)
    dm, xi, zeta, cd = _decay_tables()
    x2 = x.reshape(batch * seq, D_MODEL)
    for layer in range(depth):
        order = jnp.argsort(b_forget[layer])
        w_packed, wt_packed = _pack_w_in(w_in[layer], order)
        fqt, fk, fvt, lf_t, kn2, rq, rk, rv, rg = _inproj(x2, w_packed, wt_packed, b_forget[layer][order],
                                                          cs, sn, seq)
        seq3 = lambda t: t.reshape(batch, seq, t.shape[-1])
        lf = lf_t[:FOX_HEADS].reshape(FOX_HEADS, batch, seq).transpose(1, 0, 2)
        pieces = _neg_cumsum_pieces(lf)
        nb, kn = _score_bounds(pieces, kn2, batch, seq)
        fox = _fox_attention(nb, kn, fqt, seq3(fk), _bias_operand(pieces, batch, seq), fvt, batch, seq)
        ret = _retention(seq3(rq), seq3(rk), seq3(rv), seq3(rg), dm, xi, zeta, cd, ret_gn_g[layer])
        wo = w_out[layer].astype(BF16)
        wo_fox = wo[:FOX_WIDTH].reshape(FOX_HEADS, FOX_HEAD_DIM, D_MODEL)[order].reshape(FOX_WIDTH, D_MODEL)
        vec = lambda p: p[layer].reshape(1, D_MODEL)
        x2 = _tail(alpha, fox.reshape(batch * seq, FOX_WIDTH), ret.reshape(batch * seq, RET_WIDTH), x2,
                   wo_fox, wo[FOX_WIDTH:], w_ff1[layer].astype(BF16), w_ff2[layer].astype(BF16),
                   vec(ln1_g), vec(ln1_b), vec(ln2_g), vec(ln2_b))
    return x2.reshape(batch, seq, D_MODEL)
```

```python
import functools
import math

import jax
import jax.numpy as jnp
from jax import lax
from jax.experimental import pallas as pl
from jax.experimental.pallas import tpu as pltpu

F32 = jnp.float32
BF16 = jnp.bfloat16

D_MODEL = 1024
FOX_HEADS = 8
FOX_HEAD_DIM = 64
FOX_WIDTH = FOX_HEADS * FOX_HEAD_DIM
RET_HEADS = 4
RET_HEAD_DIM = 128
RET_WIDTH = RET_HEADS * RET_HEAD_DIM
D_FF = 4 * D_MODEL
RET_CHUNK = 128
ROPE_BASE = 10000.0
LN_EPS = 1e-5
GN_EPS = 1e-5
LOG2E = math.log2(math.e)

LANES = 128
SUBLANES = 8
MXU_DEPTH = 256
VMEM_LIMIT = 56 * 1024 * 1024

TOKEN_TILE = 512
ATTN_TILE = 512
ATTN_UNROLL = 4
FIXED_REF_GAP = 100.0
ZERO_WEIGHT_LOG2 = 135.0
BOUND_SLACK = 2.0
NORM_INFLATE = 1.01
RET_TILE = 1024
TAIL_ROW_GROUPS = 2

FOX_PAIRS = FOX_HEADS // 2
V_ROWS = 80
BIAS_PIECES = 3

_OFF_FK = 0
_OFF_RQ = FOX_WIDTH
_OFF_RK, _OFF_RV, _OFF_RG = _OFF_RQ + RET_WIDTH, _OFF_RQ + 2 * RET_WIDTH, _OFF_RQ + 3 * RET_WIDTH
PACKED_COLS = _OFF_RQ + 4 * RET_WIDTH
LOGIT_ROWS = 16
_ROW_FV = FOX_WIDTH
_ROW_FL = _ROW_FV + FOX_HEADS * V_ROWS
PACKED_ROWS = _ROW_FL + LOGIT_ROWS

_NT = (((1,), (1,)), ((), ()))


def _params(*semantics):
    return pltpu.CompilerParams(dimension_semantics=semantics, vmem_limit_bytes=VMEM_LIMIT)


def _inproj_kernel(x_ref, w_ref, wt_ref, bf_ref, cs_ref, sn_ref, ones_ref,
                   fqt_ref, fk_ref, fvt_ref, lf_ref, kn2_ref, rq_ref, rk_ref, rv_ref, rg_ref):
    xb = x_ref[...].astype(BF16)

    projected = jnp.dot(xb, w_ref[...], preferred_element_type=F32)

    def proj(lo, width):
        return projected[:, lo:lo + width]

    transposed = lax.dot_general(wt_ref[...], xb, _NT, preferred_element_type=F32)

    def proj_t(lo, rows):
        return transposed[lo:lo + rows, :]

    fqt_ref[...] = (proj_t(0, FOX_WIDTH) * (FOX_HEAD_DIM ** -0.5 * LOG2E)).astype(BF16)
    kb = proj(_OFF_FK, FOX_WIDTH).astype(BF16)
    fk_ref[...] = kb
    kf = kb.astype(F32)
    col = lax.broadcasted_iota(jnp.int32, (FOX_WIDTH, LANES), 0)
    lane = lax.broadcasted_iota(jnp.int32, (FOX_WIDTH, LANES), 1)
    sel = (col // FOX_HEAD_DIM == lane).astype(BF16)
    kn2 = jnp.dot((kf * kf).astype(BF16), sel, preferred_element_type=F32)
    kn2_ref[0] = jnp.broadcast_to(jnp.max(kn2, axis=0, keepdims=True), kn2_ref.shape[1:])
    vt = proj_t(_ROW_FV, FOX_HEADS * V_ROWS)
    fvt_ref[...] = jnp.where(ones_ref[...] != 0.0, 1.0, vt).astype(BF16)
    rv_ref[...] = proj(_OFF_RV, RET_WIDTH).astype(BF16)
    rg_ref[...] = proj(_OFF_RG, RET_WIDTH)

    z = proj_t(_ROW_FL, LOGIT_ROWS) + bf_ref[...]
    lf_ref[...] = -(jnp.maximum(-z, 0.0) + jnp.log1p(jnp.exp(-jnp.abs(z))))

    cs = cs_ref[...]
    sn = sn_ref[...]

    def rotary(t, scale):
        for h in range(RET_HEADS):
            th = t[:, h * RET_HEAD_DIM:(h + 1) * RET_HEAD_DIM]
            r = th * cs + pltpu.roll(th, RET_HEAD_DIM // 2, 1) * sn
            if scale is not None:
                r = r * scale
            yield h, r.astype(BF16)

    for h, r in rotary(proj(_OFF_RQ, RET_WIDTH), None):
        rq_ref[:, h * RET_HEAD_DIM:(h + 1) * RET_HEAD_DIM] = r
    for h, r in rotary(proj(_OFF_RK, RET_WIDTH), RET_HEAD_DIM ** -0.5):
        rk_ref[:, h * RET_HEAD_DIM:(h + 1) * RET_HEAD_DIM] = r


def _inproj(x2, w_packed, wt_packed, b_forget, cs, sn, seq):
    n = x2.shape[0]
    tm = TOKEN_TILE
    assert tm == ATTN_TILE, "the key-norm output is one row per attention key tile"
    bf_rows = jnp.broadcast_to(jnp.pad(b_forget, (0, LOGIT_ROWS - FOX_HEADS))[:, None], (LOGIT_ROWS, tm))
    pos_blocks = seq // tm
    row = lambda i: (i, 0)
    col = lambda i: (0, i)
    const = lambda i: (0, 0)
    pos = lambda i: (i % pos_blocks, 0)
    wide = lambda w, dt: jax.ShapeDtypeStruct((n, w), dt)
    tall = lambda r: jax.ShapeDtypeStruct((r, n), BF16)
    return pl.pallas_call(
        _inproj_kernel,
        grid=(n // tm,),
        in_specs=[pl.BlockSpec((tm, D_MODEL), row),
                  pl.BlockSpec((D_MODEL, PACKED_COLS), const),
                  pl.BlockSpec((PACKED_ROWS, D_MODEL), const),
                  pl.BlockSpec((LOGIT_ROWS, tm), const),
                  pl.BlockSpec((tm, LANES), pos),
                  pl.BlockSpec((tm, LANES), pos),
                  pl.BlockSpec((FOX_HEADS * V_ROWS, tm), const)],
        out_specs=[pl.BlockSpec((FOX_WIDTH, tm), col),
                   pl.BlockSpec((tm, FOX_WIDTH), row),
                   pl.BlockSpec((FOX_HEADS * V_ROWS, tm), col),
                   pl.BlockSpec((LOGIT_ROWS, tm), col),
                   pl.BlockSpec((1, SUBLANES, LANES), lambda i: (i, 0, 0))]
                  + [pl.BlockSpec((tm, RET_WIDTH), row)] * 4,
        out_shape=[tall(FOX_WIDTH), wide(FOX_WIDTH, BF16), tall(FOX_HEADS * V_ROWS),
                   jax.ShapeDtypeStruct((LOGIT_ROWS, n), F32),
                   jax.ShapeDtypeStruct((n // tm, SUBLANES, LANES), F32)]
                  + [wide(RET_WIDTH, BF16)] * 3 + [wide(RET_WIDTH, F32)],
        compiler_params=_params("parallel"),
        name="in_proj",
    )(x2, w_packed, wt_packed, bf_rows, cs, sn, _ones_rows(tm))


def _ones_rows(width):
    row = jnp.arange(FOX_HEADS * V_ROWS) % V_ROWS
    return jnp.broadcast_to((row == FOX_HEAD_DIM).astype(F32)[:, None], (FOX_HEADS * V_ROWS, width))


def _split3(t):
    hi = t.astype(BF16)
    r1 = t - hi.astype(F32)
    mid = r1.astype(BF16)
    lo = (r1 - mid.astype(F32)).astype(BF16)
    return hi, mid, lo


def _dot3(pieces, m):
    return sum(jnp.dot(p, m, preferred_element_type=F32) for p in pieces)


def _cumsum_kernel(x_ref, o_ref):
    heads, rows, lanes = x_ref.shape[1:]
    j = lax.broadcasted_iota(jnp.int32, (lanes, lanes), 0)
    l = lax.broadcasted_iota(jnp.int32, (lanes, lanes), 1)
    upper = (j <= l).astype(BF16)
    ones = jnp.ones((lanes, lanes), BF16)
    r_out = lax.broadcasted_iota(jnp.int32, (rows, rows), 0)
    r_in = lax.broadcasted_iota(jnp.int32, (rows, rows), 1)
    strict = (r_in < r_out).astype(BF16)
    for h in range(heads):
        pieces = _split3(x_ref[0, h] * (-LOG2E))
        within = _dot3(pieces, upper)
        totals = _dot3(pieces, ones)
        offs = sum(jnp.dot(strict, p, preferred_element_type=F32) for p in _split3(totals))
        for t, piece in enumerate(_split3(within + offs)):
            o_ref[0, t, h] = piece


def _neg_cumsum_pieces(lf):
    b, h, s = lf.shape
    rows = s // LANES
    return pl.pallas_call(
        _cumsum_kernel,
        grid=(b,),
        in_specs=[pl.BlockSpec((1, h, rows, LANES), lambda i: (i, 0, 0, 0))],
        out_specs=pl.BlockSpec((1, BIAS_PIECES, h, rows, LANES), lambda i: (i, 0, 0, 0, 0)),
        out_shape=jax.ShapeDtypeStruct((b, BIAS_PIECES, h, rows, LANES), BF16),
        compiler_params=_params("parallel"),
        name="forget_cumsum",
    )(lf.reshape(b, h, rows, LANES)).reshape(b, BIAS_PIECES, h, s)


def _fox_kernel(nb_ref, kn_ref, qt_ref, k_ref, cb_ref, vt_ref, o_ref,
                qaug_ref, m_ref, acc_ref, s_ref, cmax_ref, plan_ref):
    t = ATTN_TILE
    n_tiles = qt_ref.shape[1] // t
    refs = (nb_ref, kn_ref, qt_ref, k_ref, cb_ref, vt_ref, o_ref, qaug_ref, m_ref, acc_ref, s_ref, cmax_ref,
            plan_ref)
    _fox_stage_diagonal(0, 0, nb_ref, kn_ref, qt_ref, k_ref, cb_ref, qaug_ref, s_ref, cmax_ref, plan_ref)

    def query_tile(qi, carry):
        _fox_query_tile(qi, n_tiles, *refs)
        return carry

    lax.fori_loop(0, n_tiles, query_tile, 0)


def _fox_query_operand(qi, slot, qt_ref, qaug_ref):
    t = ATTN_TILE
    qt = qt_ref[:, pl.ds(pl.multiple_of(qi * t, t), t)]
    sub = lax.broadcasted_iota(jnp.int32, (LANES, t), 0)
    for a in range(2):
        mine = (sub >= a * FOX_HEAD_DIM) & (sub < (a + 1) * FOX_HEAD_DIM)
        qaug_ref[slot, a, 0:LANES, :] = jnp.where(mine, qt, jnp.zeros_like(qt))
        pick = (sub >= a * BIAS_PIECES) & (sub < (a + 1) * BIAS_PIECES)
        qaug_ref[slot, a, LANES:MXU_DEPTH, :] = jnp.where(pick, 1.0, 0.0).astype(BF16)


def _fox_scores(qi, slot, ki, masked, k_ref, cb_ref, qaug_ref, s_ref, cmax_ref):
    t = ATTN_TILE
    start = pl.multiple_of(ki * t, t)
    kaug = jnp.concatenate([k_ref[0, pl.ds(start, t), :], cb_ref[0, 0, pl.ds(start, t), :]], axis=1)
    if masked:
        key = lax.broadcasted_iota(jnp.int32, (t, t), 0) + ki * t
        qry = lax.broadcasted_iota(jnp.int32, (t, t), 1) + qi * t
        keep = key <= qry
    for a in range(2):
        s = jnp.dot(kaug, qaug_ref[slot, a], preferred_element_type=F32)
        if masked:
            s = jnp.where(keep, s, -jnp.inf)
        s_ref[a] = s
        cmax_ref[a] = jnp.max(s, axis=0, keepdims=True)


def _fox_stage_diagonal(qi, slot, nb_ref, kn_ref, qt_ref, k_ref, cb_ref, qaug_ref, s_ref, cmax_ref, plan_ref):
    t = ATTN_TILE
    _fox_query_operand(qi, slot, qt_ref, qaug_ref)
    _fox_scores(qi, slot, qi, True, k_ref, cb_ref, qaug_ref, s_ref, cmax_ref)
    qt = qt_ref[:, pl.ds(pl.multiple_of(qi * t, t), t)].astype(F32)
    q2 = qt * qt
    need, fixed = None, None
    newest_older = jnp.maximum(qi - 1, 0)
    for a in range(2):
        qnorm = jnp.sqrt(jnp.sum(q2[a * FOX_HEAD_DIM:(a + 1) * FOX_HEAD_DIM], axis=0, keepdims=True))
        seen = cmax_ref[a]
        bound = nb_ref[a] + qnorm * kn_ref[a] - seen
        hit = jnp.max(bound, axis=1, keepdims=True) >= -ZERO_WEIGHT_LOG2
        need = hit if need is None else need | hit
        top = (nb_ref[a, pl.ds(newest_older, 1), :]
               + qnorm * kn_ref[a, pl.ds(newest_older, 1), :] - seen)
        ok = jnp.max(top) <= FIXED_REF_GAP
        fixed = ok if fixed is None else fixed & ok
    older = lax.broadcasted_iota(jnp.int32, need.shape, 0) < qi
    plan_ref[0] = jnp.sum((need & older).astype(jnp.int32))
    plan_ref[1] = fixed.astype(jnp.int32)


def _fox_query_tile(qi, n_tiles, nb_ref, kn_ref, qt_ref, k_ref, cb_ref, vt_ref, o_ref,
                    qaug_ref, m_ref, acc_ref, s_ref, cmax_ref, plan_ref):
    t = tq = tk = ATTN_TILE
    slot = qi % 2
    q_start = pl.multiple_of(qi * t, t)
    n = plan_ref[0]
    fixed = plan_ref[1] != 0
    m_ref[...] = jnp.full(m_ref.shape, -jnp.inf, F32)
    acc_ref[...] = jnp.zeros(acc_ref.shape, F32)

    def scores(ki, masked):
        _fox_scores(qi, slot, ki, masked, k_ref, cb_ref, qaug_ref, s_ref, cmax_ref)

    def weights():
        out = []
        for a in range(2):
            m_prev = m_ref[a]
            m_new = jnp.maximum(m_prev, cmax_ref[a])
            out.append((jnp.exp2(s_ref[a] - m_new).astype(BF16), jnp.exp2(m_prev - m_new)))
            m_ref[a] = m_new
        return out

    def values(ki, pw):
        start = pl.multiple_of(ki * tk, tk)
        for a, (p, alpha) in enumerate(pw):
            vt = vt_ref[a * V_ROWS:(a + 1) * V_ROWS, pl.ds(start, tk)]
            acc_ref[a] = alpha * acc_ref[a] + jnp.dot(vt, p, preferred_element_type=F32)

    def step(k, masked=False):
        pw = weights()
        scores(k - 1, masked)
        values(k, pw)

    def body(g, carry):
        for u in range(ATTN_UNROLL):
            step(oldest_single - (g * ATTN_UNROLL + u))
        return carry

    def fixed_weights(ki):
        start = pl.multiple_of(ki * tk, tk)
        kaug = jnp.concatenate([k_ref[0, pl.ds(start, tk), :], cb_ref[0, 0, pl.ds(start, tk), :]], axis=1)
        return [jnp.exp2(jnp.dot(kaug, qaug_ref[slot, a], preferred_element_type=F32) - m_ref[a]).astype(BF16)
                for a in range(2)]

    def fixed_values(ki, ps):
        start = pl.multiple_of(ki * tk, tk)
        return [jnp.dot(vt_ref[a * V_ROWS:(a + 1) * V_ROWS, pl.ds(start, tk)], ps[a],
                        preferred_element_type=F32) for a in range(2)]

    def fixed_tiles(k0, count):
        ps = fixed_weights(k0)
        sums = None
        for u in range(count):
            nxt = fixed_weights(k0 - u - 1) if u + 1 < count else None
            pv = fixed_values(k0 - u, ps)
            sums = pv if sums is None else [x + y for x, y in zip(sums, pv)]
            ps = nxt
        for a in range(2):
            acc_ref[a] = acc_ref[a] + sums[a]

    def fixed_body(g, carry):
        fixed_tiles(first - 1 - singles - g * ATTN_UNROLL, ATTN_UNROLL)
        return carry

    first = qi
    groups = n // ATTN_UNROLL
    singles = n - groups * ATTN_UNROLL
    running_singles = jnp.where(fixed, 0, singles)
    for u in range(ATTN_UNROLL - 1):
        @pl.when(u < running_singles)
        def _():
            step(first - u)
    oldest_single = first - running_singles
    running_groups = jnp.where(fixed, 0, groups)
    lax.fori_loop(0, running_groups, body, 0)
    pw = weights()
    upcoming = jnp.minimum(qi + 1, n_tiles - 1)
    _fox_stage_diagonal(upcoming, 1 - slot, nb_ref, kn_ref, qt_ref, k_ref, cb_ref, qaug_ref, s_ref, cmax_ref,
                        plan_ref)
    values(oldest_single - running_groups * ATTN_UNROLL, pw)
    for count in range(1, ATTN_UNROLL):
        @pl.when(fixed & (singles == count))
        def _():
            fixed_tiles(first - 1, count)
    lax.fori_loop(0, groups - running_groups, fixed_body, 0)

    outs = [acc_ref[a, 0:FOX_HEAD_DIM, :] / acc_ref[a, FOX_HEAD_DIM:FOX_HEAD_DIM + 1, :] for a in range(2)]
    o_ref[0, pl.ds(q_start, tq), :] = jnp.concatenate(outs, axis=0).T.astype(o_ref.dtype)


def _fox_attention(nb, kn, fqt, fk, cb, fvt, batch, seq):
    tq = tk = ATTN_TILE
    bound = pl.BlockSpec((2, seq // tk, tq), lambda bi, hp: (bi * FOX_PAIRS + hp, 0, 0))
    return pl.pallas_call(
        _fox_kernel,
        grid=(batch, FOX_PAIRS),
        in_specs=[bound, bound,
                  pl.BlockSpec((LANES, seq), lambda bi, hp: (hp, bi)),
                  pl.BlockSpec((1, seq, LANES), lambda bi, hp: (bi, 0, hp)),
                  pl.BlockSpec((1, 1, seq, LANES), lambda bi, hp: (bi, hp, 0, 0)),
                  pl.BlockSpec((2 * V_ROWS, seq), lambda bi, hp: (hp, bi))],
        out_specs=pl.BlockSpec((1, seq, LANES), lambda bi, hp: (bi, 0, hp)),
        out_shape=jax.ShapeDtypeStruct((batch, seq, FOX_WIDTH), BF16),
        scratch_shapes=[pltpu.VMEM((2, 2, MXU_DEPTH, tq), BF16),
                        pltpu.VMEM((2, 1, tq), F32),
                        pltpu.VMEM((2, V_ROWS, tq), F32),
                        pltpu.VMEM((2, tk, tq), F32),
                        pltpu.VMEM((2, 1, tq), F32),
                        pltpu.SMEM((2,), jnp.int32)],
        compiler_params=_params("parallel", "parallel"),
        name="fox_attention",
    )(nb, kn, fqt, fk, cb, fvt)


def _score_bounds(pieces, kn2, batch, seq):
    nt = seq // ATTN_TILE
    negc = pieces.astype(F32).sum(axis=1)
    nb = lax.cummax(negc.reshape(batch, FOX_HEADS, nt, ATTN_TILE).max(axis=-1), axis=2) + BOUND_SLACK
    k2 = kn2[:, 0, :FOX_HEADS].reshape(batch, nt, FOX_HEADS)
    kn = lax.cummax(jnp.sqrt(k2).transpose(0, 2, 1), axis=2) * NORM_INFLATE
    rep = lambda b: jnp.broadcast_to(b.reshape(batch * FOX_HEADS, nt, 1), (batch * FOX_HEADS, nt, ATTN_TILE))
    return rep(nb), rep(kn)


def _ret_kernel(q_ref, k_ref, v_ref, g_ref, dm_ref, xi_ref, zeta_ref, cd_ref, gn_ref,
                o_ref, state_ref):
    @pl.when(pl.program_id(1) == 0)
    def _():
        state_ref[...] = jnp.zeros(state_ref.shape, F32)

    c = RET_CHUNK
    blocks = [(n, h) for n in range(RET_TILE // c) for h in range(RET_HEADS)]
    rows = lambda n: slice(n * c, (n + 1) * c)
    cols = lambda h: slice(h * RET_HEAD_DIM, (h + 1) * RET_HEAD_DIM)

    inner, kv = {}, {}
    for n, h in blocks:
        q, k = q_ref[0, rows(n), cols(h)], k_ref[0, rows(n), cols(h)]
        inner[n, h] = (lax.dot_general(q, k, _NT, preferred_element_type=F32) * dm_ref[h]).astype(BF16)
    for n, h in blocks:
        kz = (k_ref[0, rows(n), cols(h)].astype(F32) * zeta_ref[h]).astype(BF16)
        kv[n, h] = lax.dot_general(kz, v_ref[0, rows(n), cols(h)], (((0,), (0,)), ((), ())),
                                   preferred_element_type=F32)
    before = {}
    for h in range(RET_HEADS):
        st = state_ref[h]
        for n in range(RET_TILE // c):
            before[n, h] = st.astype(BF16)
            st = cd_ref[h] * st + kv[n, h]
        state_ref[h] = st
    for n, h in blocks:
        q, v = q_ref[0, rows(n), cols(h)], v_ref[0, rows(n), cols(h)]
        o = (jnp.dot(inner[n, h], v, preferred_element_type=F32)
             + xi_ref[h] * jnp.dot(q, before[n, h], preferred_element_type=F32))
        mu = jnp.mean(o, axis=-1, keepdims=True)
        d = o - mu
        var = jnp.mean(d * d, axis=-1, keepdims=True)
        y = d * lax.rsqrt(var + GN_EPS) * gn_ref[h:h + 1, :]
        g = g_ref[0, rows(n), cols(h)]
        o_ref[0, rows(n), cols(h)] = (g * jax.nn.sigmoid(g) * y).astype(o_ref.dtype)


def _retention(rq, rk, rv, rg, dm, xi, zeta, cd, gn):
    b, s, _ = rq.shape
    t = RET_TILE
    tok = pl.BlockSpec((1, t, RET_WIDTH), lambda bi, si: (bi, si, 0))
    tab = pl.BlockSpec((RET_HEADS, RET_CHUNK, RET_HEAD_DIM), lambda bi, si: (0, 0, 0))
    return pl.pallas_call(
        _ret_kernel,
        grid=(b, s // t),
        in_specs=[tok, tok, tok, tok, tab, tab, tab, tab,
                  pl.BlockSpec((RET_HEADS, RET_HEAD_DIM), lambda bi, si: (0, 0))],
        out_specs=tok,
        out_shape=jax.ShapeDtypeStruct((b, s, RET_WIDTH), BF16),
        scratch_shapes=[pltpu.VMEM((RET_HEADS, RET_HEAD_DIM, RET_HEAD_DIM), F32)],
        compiler_params=_params("parallel", "arbitrary"),
        name="retention",
    )(rq, rk, rv, rg, dm, xi, zeta, cd, gn)


def _layernorm(y, g, b):
    mu = jnp.mean(y, axis=-1, keepdims=True)
    d = y - mu
    var = jnp.mean(d * d, axis=-1, keepdims=True)
    return d * lax.rsqrt(var + LN_EPS) * g + b


def _tail_kernel(alpha, fox_ref, ret_ref, x_ref, wa_ref, wb_ref, w1_ref, w2_ref,
                 g1_ref, b1_ref, g2_ref, b2_ref, o_ref):
    size = TOKEN_TILE // TAIL_ROW_GROUPS
    groups = [slice(r * size, (r + 1) * size) for r in range(TAIL_ROW_GROUPS)]
    mix = [jnp.dot(fox_ref[g, :], wa_ref[...], preferred_element_type=F32)
           + jnp.dot(ret_ref[g, :], wb_ref[...], preferred_element_type=F32) for g in groups]
    x = [_layernorm(alpha * x_ref[g, :] + m, g1_ref[...], b1_ref[...]) for g, m in zip(groups, mix)]
    h = [jnp.maximum(jnp.dot(xg.astype(BF16), w1_ref[...], preferred_element_type=F32), 0.0) for xg in x]
    ff = [jnp.dot((hg * hg).astype(BF16), w2_ref[...], preferred_element_type=F32) for hg in h]
    for g, xg, fg in zip(groups, x, ff):
        o_ref[g, :] = _layernorm(alpha * xg + fg, g2_ref[...], b2_ref[...])


def _tail(alpha, fox, ret, x2, wa, wb, w1, w2, g1, b1, g2, b2):
    n = x2.shape[0]
    tm = TOKEN_TILE
    row = lambda i: (i, 0)
    const = lambda i: (0, 0)
    resident = lambda shape: pl.BlockSpec(shape, const, pipeline_mode=pl.Buffered(1))
    vec = pl.BlockSpec((1, D_MODEL), const)
    return pl.pallas_call(
        functools.partial(_tail_kernel, alpha),
        grid=(n // tm,),
        in_specs=[pl.BlockSpec((tm, FOX_WIDTH), row),
                  pl.BlockSpec((tm, RET_WIDTH), row),
                  pl.BlockSpec((tm, D_MODEL), row),
                  resident((FOX_WIDTH, D_MODEL)),
                  resident((RET_WIDTH, D_MODEL)),
                  resident((D_MODEL, D_FF)),
                  resident((D_FF, D_MODEL)),
                  vec, vec, vec, vec],
        out_specs=pl.BlockSpec((tm, D_MODEL), row),
        out_shape=jax.ShapeDtypeStruct((n, D_MODEL), F32),
        compiler_params=_params("parallel"),
        name="out_proj_ffn",
    )(fox, ret, x2, wa, wb, w1, w2, g1, b1, g2, b2)


def _rotary_tables(seq):
    half = RET_HEAD_DIM // 2
    inv_freq = ROPE_BASE ** (-jnp.arange(half, dtype=F32) / half)
    ang = jnp.arange(seq, dtype=F32)[:, None] * inv_freq[None, :]
    cos, sin = jnp.cos(ang), jnp.sin(ang)
    return jnp.concatenate([cos, cos], axis=-1), jnp.concatenate([-sin, sin], axis=-1)


def _decay_tables():
    gamma = 1.0 - jnp.exp2(-5.0 - jnp.arange(RET_HEADS, dtype=F32))
    log_g = jnp.log(gamma)
    idx = jnp.arange(RET_CHUNK, dtype=F32)
    diff = idx[:, None] - idx[None, :]
    dm = jnp.where(diff >= 0, jnp.exp(log_g[:, None, None] * jnp.maximum(diff, 0.0)), 0.0)
    shape = (RET_HEADS, RET_CHUNK, RET_HEAD_DIM)
    xi = jnp.broadcast_to(jnp.exp(log_g[:, None] * (idx + 1.0))[..., None], shape)
    zeta = jnp.broadcast_to(jnp.exp(log_g[:, None] * (RET_CHUNK - 1.0 - idx))[..., None], shape)
    cd = jnp.broadcast_to(jnp.exp(log_g * RET_CHUNK)[:, None, None], shape)
    return dm, xi, zeta, cd


def _pack_w_in(w, order):
    d = w.shape[0]
    by_head = lambda m: m.reshape(d, FOX_HEADS, FOX_HEAD_DIM)[:, order].reshape(d, FOX_WIDTH)
    wq, wk, wv = (by_head(w[:, i * FOX_WIDTH:(i + 1) * FOX_WIDTH]) for i in range(3))
    lo = 3 * FOX_WIDTH
    logits_t = jnp.pad(w[:, lo:lo + FOX_HEADS][:, order].T, ((0, LOGIT_ROWS - FOX_HEADS), (0, 0)))
    packed = jnp.concatenate([wk, w[:, lo + FOX_HEADS:]], axis=1).astype(BF16)
    wv_t = wv.T.reshape(FOX_HEADS, FOX_HEAD_DIM, d)
    wv_t = jnp.pad(wv_t, ((0, 0), (0, V_ROWS - FOX_HEAD_DIM), (0, 0))).reshape(FOX_HEADS * V_ROWS, d)
    packed_t = jnp.concatenate([wq.T, wv_t, logits_t], axis=0).astype(BF16)
    return packed, packed_t


def _bias_operand(pieces, batch, seq):
    p = pieces.reshape(batch, BIAS_PIECES, FOX_PAIRS, 2, seq)
    p = p.transpose(0, 2, 4, 3, 1).reshape(batch, FOX_PAIRS, seq, 2 * BIAS_PIECES)
    return jnp.pad(p, ((0, 0), (0, 0), (0, 0), (0, LANES - 2 * BIAS_PIECES)))


@jax.jit
def kernel(x, w_in, w_out, w_ff1, w_ff2, ln1_g, ln1_b, ln2_g, ln2_b, b_forget, ret_gn_g):
    batch, seq, _ = x.shape
    depth = w_in.shape[0]
    alpha = (2 * depth) ** 0.25
    cs, sn = _rotary_tables(seq)
    dm, xi, zeta, cd = _decay_tables()
    x2 = x.reshape(batch * seq, D_MODEL)
    for layer in range(depth):
        order = jnp.argsort(b_forget[layer])
        w_packed, wt_packed = _pack_w_in(w_in[layer], order)
        fqt, fk, fvt, lf_t, kn2, rq, rk, rv, rg = _inproj(x2, w_packed, wt_packed, b_forget[layer][order],
                                                          cs, sn, seq)
        seq3 = lambda t: t.reshape(batch, seq, t.shape[-1])
        lf = lf_t[:FOX_HEADS].reshape(FOX_HEADS, batch, seq).transpose(1, 0, 2)
        pieces = _neg_cumsum_pieces(lf)
        nb, kn = _score_bounds(pieces, kn2, batch, seq)
        fox = _fox_attention(nb, kn, fqt, seq3(fk), _bias_operand(pieces, batch, seq), fvt, batch, seq)
        ret = _retention(seq3(rq), seq3(rk), seq3(rv), seq3(rg), dm, xi, zeta, cd, ret_gn_g[layer])
        wo = w_out[layer].astype(BF16)
        wo_fox = wo[:FOX_WIDTH].reshape(FOX_HEADS, FOX_HEAD_DIM, D_MODEL)[order].reshape(FOX_WIDTH, D_MODEL)
        vec = lambda p: p[layer].reshape(1, D_MODEL)
        x2 = _tail(alpha, fox.reshape(batch * seq, FOX_WIDTH), ret.reshape(batch * seq, RET_WIDTH), x2,
                   wo_fox, wo[FOX_WIDTH:], w_ff1[layer].astype(BF16), w_ff2[layer].astype(BF16),
                   vec(ln1_g), vec(ln1_b), vec(ln2_g), vec(ln2_b))
    return x2.reshape(batch, seq, D_MODEL)
```

```python
import functools
import math

import jax
import jax.numpy as jnp
from jax import lax
from jax.experimental import pallas as pl
from jax.experimental.pallas import tpu as pltpu

F32 = jnp.float32
BF16 = jnp.bfloat16

D_MODEL = 1024
FOX_HEADS = 8
FOX_HEAD_DIM = 64
FOX_WIDTH = FOX_HEADS * FOX_HEAD_DIM
RET_HEADS = 4
RET_HEAD_DIM = 128
RET_WIDTH = RET_HEADS * RET_HEAD_DIM
D_FF = 4 * D_MODEL
RET_CHUNK = 128
ROPE_BASE = 10000.0
LN_EPS = 1e-5
GN_EPS = 1e-5
LOG2E = math.log2(math.e)

LANES = 128
SUBLANES = 8
MXU_DEPTH = 256
VMEM_LIMIT = 56 * 1024 * 1024

TOKEN_TILE = 512
ATTN_TILE = 512
ATTN_UNROLL = 4
FIXED_REF_GAP = 100.0
ZERO_WEIGHT_LOG2 = 135.0
BOUND_SLACK = 2.0
NORM_INFLATE = 1.01
RET_TILE = 1024
TAIL_ROW_GROUPS = 2

FOX_PAIRS = FOX_HEADS // 2
V_ROWS = 80
BIAS_PIECES = 3

_OFF_FK = 0
_OFF_RQ = FOX_WIDTH
_OFF_RK, _OFF_RV, _OFF_RG = _OFF_RQ + RET_WIDTH, _OFF_RQ + 2 * RET_WIDTH, _OFF_RQ + 3 * RET_WIDTH
PACKED_COLS = _OFF_RQ + 4 * RET_WIDTH
LOGIT_ROWS = 16
_ROW_FV = FOX_WIDTH
_ROW_FL = _ROW_FV + FOX_HEADS * V_ROWS
PACKED_ROWS = _ROW_FL + LOGIT_ROWS

_NT = (((1,), (1,)), ((), ()))


def _params(*semantics):
    return pltpu.CompilerParams(dimension_semantics=semantics, vmem_limit_bytes=VMEM_LIMIT)


def _inproj_kernel(x_ref, w_ref, wt_ref, bf_ref, cs_ref, sn_ref, ones_ref,
                   fqt_ref, fk_ref, fvt_ref, lf_ref, kn2_ref, rq_ref, rk_ref, rv_ref, rg_ref):
    xb = x_ref[...].astype(BF16)

    projected = jnp.dot(xb, w_ref[...], preferred_element_type=F32)

    def proj(lo, width):
        return projected[:, lo:lo + width]

    transposed = lax.dot_general(wt_ref[...], xb, _NT, preferred_element_type=F32)

    def proj_t(lo, rows):
        return transposed[lo:lo + rows, :]

    fqt_ref[...] = (proj_t(0, FOX_WIDTH) * (FOX_HEAD_DIM ** -0.5 * LOG2E)).astype(BF16)
    kb = proj(_OFF_FK, FOX_WIDTH).astype(BF16)
    fk_ref[...] = kb
    kf = kb.astype(F32)
    col = lax.broadcasted_iota(jnp.int32, (FOX_WIDTH, LANES), 0)
    lane = lax.broadcasted_iota(jnp.int32, (FOX_WIDTH, LANES), 1)
    sel = (col // FOX_HEAD_DIM == lane).astype(BF16)
    kn2 = jnp.dot((kf * kf).astype(BF16), sel, preferred_element_type=F32)
    kn2_ref[0] = jnp.broadcast_to(jnp.max(kn2, axis=0, keepdims=True), kn2_ref.shape[1:])
    vt = proj_t(_ROW_FV, FOX_HEADS * V_ROWS)
    fvt_ref[...] = jnp.where(ones_ref[...] != 0.0, 1.0, vt).astype(BF16)
    rv_ref[...] = proj(_OFF_RV, RET_WIDTH).astype(BF16)
    rg_ref[...] = proj(_OFF_RG, RET_WIDTH)

    z = proj_t(_ROW_FL, LOGIT_ROWS) + bf_ref[...]
    lf_ref[...] = -(jnp.maximum(-z, 0.0) + jnp.log1p(jnp.exp(-jnp.abs(z))))

    cs = cs_ref[...]
    sn = sn_ref[...]

    def rotary(t, scale):
        for h in range(RET_HEADS):
            th = t[:, h * RET_HEAD_DIM:(h + 1) * RET_HEAD_DIM]
            r = th * cs + pltpu.roll(th, RET_HEAD_DIM // 2, 1) * sn
            if scale is not None:
                r = r * scale
            yield h, r.astype(BF16)

    for h, r in rotary(proj(_OFF_RQ, RET_WIDTH), None):
        rq_ref[:, h * RET_HEAD_DIM:(h + 1) * RET_HEAD_DIM] = r
    for h, r in rotary(proj(_OFF_RK, RET_WIDTH), RET_HEAD_DIM ** -0.5):
        rk_ref[:, h * RET_HEAD_DIM:(h + 1) * RET_HEAD_DIM] = r


def _inproj(x2, w_packed, wt_packed, b_forget, cs, sn, seq):
    n = x2.shape[0]
    tm = TOKEN_TILE
    assert tm == ATTN_TILE, "the key-norm output is one row per attention key tile"
    bf_rows = jnp.broadcast_to(jnp.pad(b_forget, (0, LOGIT_ROWS - FOX_HEADS))[:, None], (LOGIT_ROWS, tm))
    pos_blocks = seq // tm
    row = lambda i: (i, 0)
    col = lambda i: (0, i)
    const = lambda i: (0, 0)
    pos = lambda i: (i % pos_blocks, 0)
    wide = lambda w, dt: jax.ShapeDtypeStruct((n, w), dt)
    tall = lambda r: jax.ShapeDtypeStruct((r, n), BF16)
    return pl.pallas_call(
        _inproj_kernel,
        grid=(n // tm,),
        in_specs=[pl.BlockSpec((tm, D_MODEL), row),
                  pl.BlockSpec((D_MODEL, PACKED_COLS), const),
                  pl.BlockSpec((PACKED_ROWS, D_MODEL), const),
                  pl.BlockSpec((LOGIT_ROWS, tm), const),
                  pl.BlockSpec((tm, LANES), pos),
                  pl.BlockSpec((tm, LANES), pos),
                  pl.BlockSpec((FOX_HEADS * V_ROWS, tm), const)],
        out_specs=[pl.BlockSpec((FOX_WIDTH, tm), col),
                   pl.BlockSpec((tm, FOX_WIDTH), row),
                   pl.BlockSpec((FOX_HEADS * V_ROWS, tm), col),
                   pl.BlockSpec((LOGIT_ROWS, tm), col),
                   pl.BlockSpec((1, SUBLANES, LANES), lambda i: (i, 0, 0))]
                  + [pl.BlockSpec((tm, RET_WIDTH), row)] * 4,
        out_shape=[tall(FOX_WIDTH), wide(FOX_WIDTH, BF16), tall(FOX_HEADS * V_ROWS),
                   jax.ShapeDtypeStruct((LOGIT_ROWS, n), F32),
                   jax.ShapeDtypeStruct((n // tm, SUBLANES, LANES), F32)]
                  + [wide(RET_WIDTH, BF16)] * 3 + [wide(RET_WIDTH, F32)],
        compiler_params=_params("parallel"),
        name="in_proj",
    )(x2, w_packed, wt_packed, bf_rows, cs, sn, _ones_rows(tm))


def _ones_rows(width):
    row = jnp.arange(FOX_HEADS * V_ROWS) % V_ROWS
    return jnp.broadcast_to((row == FOX_HEAD_DIM).astype(F32)[:, None], (FOX_HEADS * V_ROWS, width))


def _split3(t):
    hi = t.astype(BF16)
    r1 = t - hi.astype(F32)
    mid = r1.astype(BF16)
    lo = (r1 - mid.astype(F32)).astype(BF16)
    return hi, mid, lo


def _dot3(pieces, m):
    return sum(jnp.dot(p, m, preferred_element_type=F32) for p in pieces)


def _cumsum_kernel(x_ref, o_ref):
    heads, rows, lanes = x_ref.shape[1:]
    j = lax.broadcasted_iota(jnp.int32, (lanes, lanes), 0)
    l = lax.broadcasted_iota(jnp.int32, (lanes, lanes), 1)
    upper = (j <= l).astype(BF16)
    ones = jnp.ones((lanes, lanes), BF16)
    r_out = lax.broadcasted_iota(jnp.int32, (rows, rows), 0)
    r_in = lax.broadcasted_iota(jnp.int32, (rows, rows), 1)
    strict = (r_in < r_out).astype(BF16)
    for h in range(heads):
        pieces = _split3(x_ref[0, h] * (-LOG2E))
        within = _dot3(pieces, upper)
        totals = _dot3(pieces, ones)
        offs = sum(jnp.dot(strict, p, preferred_element_type=F32) for p in _split3(totals))
        for t, piece in enumerate(_split3(within + offs)):
            o_ref[0, t, h] = piece


def _neg_cumsum_pieces(lf):
    b, h, s = lf.shape
    rows = s // LANES
    return pl.pallas_call(
        _cumsum_kernel,
        grid=(b,),
        in_specs=[pl.BlockSpec((1, h, rows, LANES), lambda i: (i, 0, 0, 0))],
        out_specs=pl.BlockSpec((1, BIAS_PIECES, h, rows, LANES), lambda i: (i, 0, 0, 0, 0)),
        out_shape=jax.ShapeDtypeStruct((b, BIAS_PIECES, h, rows, LANES), BF16),
        compiler_params=_params("parallel"),
        name="forget_cumsum",
    )(lf.reshape(b, h, rows, LANES)).reshape(b, BIAS_PIECES, h, s)


def _fox_kernel(nb_ref, kn_ref, qt_ref, k_ref, pieces_ref, vt_ref, o_ref,
                qaug_ref, m_ref, acc_ref, s_ref, cmax_ref, plan_ref, cb_ref):
    t = ATTN_TILE
    n_tiles = qt_ref.shape[1] // t

    def bias_columns(g, carry):
        for u in range(ATTN_UNROLL):
            start = pl.multiple_of((g * ATTN_UNROLL + u) * t, t)
            rows = pieces_ref[0, 0, :, pl.ds(start, t)].astype(F32)
            full = jnp.concatenate([rows, jnp.zeros((LANES - rows.shape[0], t), F32)], axis=0)
            cb_ref[0, 0, pl.ds(start, t), :] = full.T.astype(BF16)
        return carry

    lax.fori_loop(0, n_tiles // ATTN_UNROLL, bias_columns, 0)
    refs = (nb_ref, kn_ref, qt_ref, k_ref, cb_ref, vt_ref, o_ref, qaug_ref, m_ref, acc_ref, s_ref, cmax_ref,
            plan_ref)
    _fox_stage_diagonal(0, 0, nb_ref, kn_ref, qt_ref, k_ref, cb_ref, qaug_ref, s_ref, cmax_ref, plan_ref)

    def query_tile(qi, carry):
        _fox_query_tile(qi, n_tiles, *refs)
        return carry

    lax.fori_loop(0, n_tiles, query_tile, 0)


def _fox_query_operand(qi, slot, qt_ref, qaug_ref):
    t = ATTN_TILE
    qt = qt_ref[:, pl.ds(pl.multiple_of(qi * t, t), t)]
    sub = lax.broadcasted_iota(jnp.int32, (LANES, t), 0)
    for a in range(2):
        mine = (sub >= a * FOX_HEAD_DIM) & (sub < (a + 1) * FOX_HEAD_DIM)
        qaug_ref[slot, a, 0:LANES, :] = jnp.where(mine, qt, jnp.zeros_like(qt))
        pick = (sub >= a * BIAS_PIECES) & (sub < (a + 1) * BIAS_PIECES)
        qaug_ref[slot, a, LANES:MXU_DEPTH, :] = jnp.where(pick, 1.0, 0.0).astype(BF16)


def _fox_scores(qi, slot, ki, masked, k_ref, cb_ref, qaug_ref, s_ref, cmax_ref):
    t = ATTN_TILE
    start = pl.multiple_of(ki * t, t)
    kaug = jnp.concatenate([k_ref[0, pl.ds(start, t), :], cb_ref[0, 0, pl.ds(start, t), :]], axis=1)
    if masked:
        key = lax.broadcasted_iota(jnp.int32, (t, t), 0) + ki * t
        qry = lax.broadcasted_iota(jnp.int32, (t, t), 1) + qi * t
        keep = key <= qry
    for a in range(2):
        s = jnp.dot(kaug, qaug_ref[slot, a], preferred_element_type=F32)
        if masked:
            s = jnp.where(keep, s, -jnp.inf)
        s_ref[a] = s
        cmax_ref[a] = jnp.max(s, axis=0, keepdims=True)


def _fox_stage_diagonal(qi, slot, nb_ref, kn_ref, qt_ref, k_ref, cb_ref, qaug_ref, s_ref, cmax_ref, plan_ref):
    t = ATTN_TILE
    _fox_query_operand(qi, slot, qt_ref, qaug_ref)
    _fox_scores(qi, slot, qi, True, k_ref, cb_ref, qaug_ref, s_ref, cmax_ref)
    qt = qt_ref[:, pl.ds(pl.multiple_of(qi * t, t), t)].astype(F32)
    q2 = qt * qt
    need, fixed = None, None
    newest_older = jnp.maximum(qi - 1, 0)
    for a in range(2):
        qnorm = jnp.sqrt(jnp.sum(q2[a * FOX_HEAD_DIM:(a + 1) * FOX_HEAD_DIM], axis=0, keepdims=True))
        seen = cmax_ref[a]
        bound = nb_ref[a] + qnorm * kn_ref[a] - seen
        hit = jnp.max(bound, axis=1, keepdims=True) >= -ZERO_WEIGHT_LOG2
        need = hit if need is None else need | hit
        top = (nb_ref[a, pl.ds(newest_older, 1), :]
               + qnorm * kn_ref[a, pl.ds(newest_older, 1), :] - seen)
        ok = jnp.max(top) <= FIXED_REF_GAP
        fixed = ok if fixed is None else fixed & ok
    older = lax.broadcasted_iota(jnp.int32, need.shape, 0) < qi
    plan_ref[0] = jnp.sum((need & older).astype(jnp.int32))
    plan_ref[1] = fixed.astype(jnp.int32)


def _fox_query_tile(qi, n_tiles, nb_ref, kn_ref, qt_ref, k_ref, cb_ref, vt_ref, o_ref,
                    qaug_ref, m_ref, acc_ref, s_ref, cmax_ref, plan_ref):
    t = tq = tk = ATTN_TILE
    slot = qi % 2
    q_start = pl.multiple_of(qi * t, t)
    n = plan_ref[0]
    fixed = plan_ref[1] != 0
    m_ref[...] = jnp.full(m_ref.shape, -jnp.inf, F32)
    acc_ref[...] = jnp.zeros(acc_ref.shape, F32)

    def scores(ki, masked):
        _fox_scores(qi, slot, ki, masked, k_ref, cb_ref, qaug_ref, s_ref, cmax_ref)

    def weights():
        out = []
        for a in range(2):
            m_prev = m_ref[a]
            m_new = jnp.maximum(m_prev, cmax_ref[a])
            out.append((jnp.exp2(s_ref[a] - m_new).astype(BF16), jnp.exp2(m_prev - m_new)))
            m_ref[a] = m_new
        return out

    def values(ki, pw):
        start = pl.multiple_of(ki * tk, tk)
        for a, (p, alpha) in enumerate(pw):
            vt = vt_ref[a * V_ROWS:(a + 1) * V_ROWS, pl.ds(start, tk)]
            acc_ref[a] = alpha * acc_ref[a] + jnp.dot(vt, p, preferred_element_type=F32)

    def step(k, masked=False):
        pw = weights()
        scores(k - 1, masked)
        values(k, pw)

    def body(g, carry):
        for u in range(ATTN_UNROLL):
            step(oldest_single - (g * ATTN_UNROLL + u))
        return carry

    def fixed_weights(ki):
        start = pl.multiple_of(ki * tk, tk)
        kaug = jnp.concatenate([k_ref[0, pl.ds(start, tk), :], cb_ref[0, 0, pl.ds(start, tk), :]], axis=1)
        return [jnp.exp2(jnp.dot(kaug, qaug_ref[slot, a], preferred_element_type=F32) - m_ref[a]).astype(BF16)
                for a in range(2)]

    def fixed_values(ki, ps):
        start = pl.multiple_of(ki * tk, tk)
        return [jnp.dot(vt_ref[a * V_ROWS:(a + 1) * V_ROWS, pl.ds(start, tk)], ps[a],
                        preferred_element_type=F32) for a in range(2)]

    def fixed_tiles(k0, count):
        ps = fixed_weights(k0)
        sums = None
        for u in range(count):
            nxt = fixed_weights(k0 - u - 1) if u + 1 < count else None
            pv = fixed_values(k0 - u, ps)
            sums = pv if sums is None else [x + y for x, y in zip(sums, pv)]
            ps = nxt
        for a in range(2):
            acc_ref[a] = acc_ref[a] + sums[a]

    def fixed_body(g, carry):
        fixed_tiles(first - 1 - singles - g * ATTN_UNROLL, ATTN_UNROLL)
        return carry

    first = qi
    groups = n // ATTN_UNROLL
    singles = n - groups * ATTN_UNROLL
    running_singles = jnp.where(fixed, 0, singles)
    for u in range(ATTN_UNROLL - 1):
        @pl.when(u < running_singles)
        def _():
            step(first - u)
    oldest_single = first - running_singles
    running_groups = jnp.where(fixed, 0, groups)
    lax.fori_loop(0, running_groups, body, 0)
    pw = weights()
    upcoming = jnp.minimum(qi + 1, n_tiles - 1)
    _fox_stage_diagonal(upcoming, 1 - slot, nb_ref, kn_ref, qt_ref, k_ref, cb_ref, qaug_ref, s_ref, cmax_ref,
                        plan_ref)
    values(oldest_single - running_groups * ATTN_UNROLL, pw)
    for count in range(1, ATTN_UNROLL):
        @pl.when(fixed & (singles == count))
        def _():
            fixed_tiles(first - 1, count)
    lax.fori_loop(0, groups - running_groups, fixed_body, 0)

    outs = [acc_ref[a, 0:FOX_HEAD_DIM, :] / acc_ref[a, FOX_HEAD_DIM:FOX_HEAD_DIM + 1, :] for a in range(2)]
    o_ref[0, pl.ds(q_start, tq), :] = jnp.concatenate(outs, axis=0).T.astype(o_ref.dtype)


def _fox_attention(nb, kn, fqt, fk, cb, fvt, batch, seq):
    tq = tk = ATTN_TILE
    bound = pl.BlockSpec((2, seq // tk, tq), lambda bi, hp: (bi * FOX_PAIRS + hp, 0, 0))
    return pl.pallas_call(
        _fox_kernel,
        grid=(batch, FOX_PAIRS),
        in_specs=[bound, bound,
                  pl.BlockSpec((LANES, seq), lambda bi, hp: (hp, bi)),
                  pl.BlockSpec((1, seq, LANES), lambda bi, hp: (bi, 0, hp)),
                  pl.BlockSpec((1, 1, SUBLANES, seq), lambda bi, hp: (bi, hp, 0, 0)),
                  pl.BlockSpec((2 * V_ROWS, seq), lambda bi, hp: (hp, bi))],
        out_specs=pl.BlockSpec((1, seq, LANES), lambda bi, hp: (bi, 0, hp)),
        out_shape=jax.ShapeDtypeStruct((batch, seq, FOX_WIDTH), BF16),
        scratch_shapes=[pltpu.VMEM((2, 2, MXU_DEPTH, tq), BF16),
                        pltpu.VMEM((2, 1, tq), F32),
                        pltpu.VMEM((2, V_ROWS, tq), F32),
                        pltpu.VMEM((2, tk, tq), F32),
                        pltpu.VMEM((2, 1, tq), F32),
                        pltpu.SMEM((2,), jnp.int32),
                        pltpu.VMEM((1, 1, seq, LANES), BF16)],
        compiler_params=_params("parallel", "parallel"),
        name="fox_attention",
    )(nb, kn, fqt, fk, cb, fvt)


def _score_bounds(pieces, kn2, batch, seq):
    nt = seq // ATTN_TILE
    negc = pieces.astype(F32).sum(axis=1)
    nb = lax.cummax(negc.reshape(batch, FOX_HEADS, nt, ATTN_TILE).max(axis=-1), axis=2) + BOUND_SLACK
    k2 = kn2[:, 0, :FOX_HEADS].reshape(batch, nt, FOX_HEADS)
    kn = lax.cummax(jnp.sqrt(k2).transpose(0, 2, 1), axis=2) * NORM_INFLATE
    rep = lambda b: jnp.broadcast_to(b.reshape(batch * FOX_HEADS, nt, 1), (batch * FOX_HEADS, nt, ATTN_TILE))
    return rep(nb), rep(kn)


def _ret_kernel(q_ref, k_ref, v_ref, g_ref, dm_ref, xi_ref, zeta_ref, cd_ref, gn_ref,
                o_ref, state_ref):
    @pl.when(pl.program_id(1) == 0)
    def _():
        state_ref[...] = jnp.zeros(state_ref.shape, F32)

    c = RET_CHUNK
    blocks = [(n, h) for n in range(RET_TILE // c) for h in range(RET_HEADS)]
    rows = lambda n: slice(n * c, (n + 1) * c)
    cols = lambda h: slice(h * RET_HEAD_DIM, (h + 1) * RET_HEAD_DIM)

    inner, kv = {}, {}
    for n, h in blocks:
        q, k = q_ref[0, rows(n), cols(h)], k_ref[0, rows(n), cols(h)]
        inner[n, h] = (lax.dot_general(q, k, _NT, preferred_element_type=F32) * dm_ref[h]).astype(BF16)
    for n, h in blocks:
        kz = (k_ref[0, rows(n), cols(h)].astype(F32) * zeta_ref[h]).astype(BF16)
        kv[n, h] = lax.dot_general(kz, v_ref[0, rows(n), cols(h)], (((0,), (0,)), ((), ())),
                                   preferred_element_type=F32)
    before = {}
    for h in range(RET_HEADS):
        st = state_ref[h]
        for n in range(RET_TILE // c):
            before[n, h] = st.astype(BF16)
            st = cd_ref[h] * st + kv[n, h]
        state_ref[h] = st
    for n, h in blocks:
        q, v = q_ref[0, rows(n), cols(h)], v_ref[0, rows(n), cols(h)]
        o = (jnp.dot(inner[n, h], v, preferred_element_type=F32)
             + xi_ref[h] * jnp.dot(q, before[n, h], preferred_element_type=F32))
        mu = jnp.mean(o, axis=-1, keepdims=True)
        d = o - mu
        var = jnp.mean(d * d, axis=-1, keepdims=True)
        y = d * lax.rsqrt(var + GN_EPS) * gn_ref[h:h + 1, :]
        g = g_ref[0, rows(n), cols(h)]
        o_ref[0, rows(n), cols(h)] = (g * jax.nn.sigmoid(g) * y).astype(o_ref.dtype)


def _retention(rq, rk, rv, rg, dm, xi, zeta, cd, gn):
    b, s, _ = rq.shape
    t = RET_TILE
    tok = pl.BlockSpec((1, t, RET_WIDTH), lambda bi, si: (bi, si, 0))
    tab = pl.BlockSpec((RET_HEADS, RET_CHUNK, RET_HEAD_DIM), lambda bi, si: (0, 0, 0))
    return pl.pallas_call(
        _ret_kernel,
        grid=(b, s // t),
        in_specs=[tok, tok, tok, tok, tab, tab, tab, tab,
                  pl.BlockSpec((RET_HEADS, RET_HEAD_DIM), lambda bi, si: (0, 0))],
        out_specs=tok,
        out_shape=jax.ShapeDtypeStruct((b, s, RET_WIDTH), BF16),
        scratch_shapes=[pltpu.VMEM((RET_HEADS, RET_HEAD_DIM, RET_HEAD_DIM), F32)],
        compiler_params=_params("parallel", "arbitrary"),
        name="retention",
    )(rq, rk, rv, rg, dm, xi, zeta, cd, gn)


def _layernorm(y, g, b):
    mu = jnp.mean(y, axis=-1, keepdims=True)
    d = y - mu
    var = jnp.mean(d * d, axis=-1, keepdims=True)
    return d * lax.rsqrt(var + LN_EPS) * g + b


def _tail_kernel(alpha, fox_ref, ret_ref, x_ref, wa_ref, wb_ref, w1_ref, w2_ref,
                 g1_ref, b1_ref, g2_ref, b2_ref, o_ref):
    size = TOKEN_TILE // TAIL_ROW_GROUPS
    groups = [slice(r * size, (r + 1) * size) for r in range(TAIL_ROW_GROUPS)]
    mix = [jnp.dot(fox_ref[g, :], wa_ref[...], preferred_element_type=F32)
           + jnp.dot(ret_ref[g, :], wb_ref[...], preferred_element_type=F32) for g in groups]
    x = [_layernorm(alpha * x_ref[g, :] + m, g1_ref[...], b1_ref[...]) for g, m in zip(groups, mix)]
    h = [jnp.maximum(jnp.dot(xg.astype(BF16), w1_ref[...], preferred_element_type=F32), 0.0) for xg in x]
    ff = [jnp.dot((hg * hg).astype(BF16), w2_ref[...], preferred_element_type=F32) for hg in h]
    for g, xg, fg in zip(groups, x, ff):
        o_ref[g, :] = _layernorm(alpha * xg + fg, g2_ref[...], b2_ref[...])


def _tail(alpha, fox, ret, x2, wa, wb, w1, w2, g1, b1, g2, b2):
    n = x2.shape[0]
    tm = TOKEN_TILE
    row = lambda i: (i, 0)
    const = lambda i: (0, 0)
    resident = lambda shape: pl.BlockSpec(shape, const, pipeline_mode=pl.Buffered(1))
    vec = pl.BlockSpec((1, D_MODEL), const)
    return pl.pallas_call(
        functools.partial(_tail_kernel, alpha),
        grid=(n // tm,),
        in_specs=[pl.BlockSpec((tm, FOX_WIDTH), row),
                  pl.BlockSpec((tm, RET_WIDTH), row),
                  pl.BlockSpec((tm, D_MODEL), row),
                  resident((FOX_WIDTH, D_MODEL)),
                  resident((RET_WIDTH, D_MODEL)),
                  resident((D_MODEL, D_FF)),
                  resident((D_FF, D_MODEL)),
                  vec, vec, vec, vec],
        out_specs=pl.BlockSpec((tm, D_MODEL), row),
        out_shape=jax.ShapeDtypeStruct((n, D_MODEL), F32),
        compiler_params=_params("parallel"),
        name="out_proj_ffn",
    )(fox, ret, x2, wa, wb, w1, w2, g1, b1, g2, b2)


def _rotary_tables(seq):
    half = RET_HEAD_DIM // 2
    inv_freq = ROPE_BASE ** (-jnp.arange(half, dtype=F32) / half)
    ang = jnp.arange(seq, dtype=F32)[:, None] * inv_freq[None, :]
    cos, sin = jnp.cos(ang), jnp.sin(ang)
    return jnp.concatenate([cos, cos], axis=-1), jnp.concatenate([-sin, sin], axis=-1)


def _decay_tables():
    gamma = 1.0 - jnp.exp2(-5.0 - jnp.arange(RET_HEADS, dtype=F32))
    log_g = jnp.log(gamma)
    idx = jnp.arange(RET_CHUNK, dtype=F32)
    diff = idx[:, None] - idx[None, :]
    dm = jnp.where(diff >= 0, jnp.exp(log_g[:, None, None] * jnp.maximum(diff, 0.0)), 0.0)
    shape = (RET_HEADS, RET_CHUNK, RET_HEAD_DIM)
    xi = jnp.broadcast_to(jnp.exp(log_g[:, None] * (idx + 1.0))[..., None], shape)
    zeta = jnp.broadcast_to(jnp.exp(log_g[:, None] * (RET_CHUNK - 1.0 - idx))[..., None], shape)
    cd = jnp.broadcast_to(jnp.exp(log_g * RET_CHUNK)[:, None, None], shape)
    return dm, xi, zeta, cd


def _pack_w_in(w, order):
    d = w.shape[0]
    by_head = lambda m: m.reshape(d, FOX_HEADS, FOX_HEAD_DIM)[:, order].reshape(d, FOX_WIDTH)
    wq, wk, wv = (by_head(w[:, i * FOX_WIDTH:(i + 1) * FOX_WIDTH]) for i in range(3))
    lo = 3 * FOX_WIDTH
    logits_t = jnp.pad(w[:, lo:lo + FOX_HEADS][:, order].T, ((0, LOGIT_ROWS - FOX_HEADS), (0, 0)))
    packed = jnp.concatenate([wk, w[:, lo + FOX_HEADS:]], axis=1).astype(BF16)
    wv_t = wv.T.reshape(FOX_HEADS, FOX_HEAD_DIM, d)
    wv_t = jnp.pad(wv_t, ((0, 0), (0, V_ROWS - FOX_HEAD_DIM), (0, 0))).reshape(FOX_HEADS * V_ROWS, d)
    packed_t = jnp.concatenate([wq.T, wv_t, logits_t], axis=0).astype(BF16)
    return packed, packed_t


def _bias_operand(pieces, batch, seq):
    p = pieces.reshape(batch, BIAS_PIECES, FOX_PAIRS, 2, seq)
    p = p.transpose(0, 2, 3, 1, 4).reshape(batch, FOX_PAIRS, 2 * BIAS_PIECES, seq)
    return jnp.pad(p, ((0, 0), (0, 0), (0, SUBLANES - 2 * BIAS_PIECES), (0, 0)))


@jax.jit
def kernel(x, w_in, w_out, w_ff1, w_ff2, ln1_g, ln1_b, ln2_g, ln2_b, b_forget, ret_gn_g):
    batch, seq, _ = x.shape
    depth = w_in.shape[0]
    alpha = (2 * depth) ** 0.25
    cs, sn = _rotary_tables(seq)
    dm, xi, zeta, cd = _decay_tables()
    x2 = x.reshape(batch * seq, D_MODEL)
    for layer in range(depth):
        order = jnp.argsort(b_forget[layer])
        w_packed, wt_packed = _pack_w_in(w_in[layer], order)
        fqt, fk, fvt, lf_t, kn2, rq, rk, rv, rg = _inproj(x2, w_packed, wt_packed, b_forget[layer][order],
                                                          cs, sn, seq)
        seq3 = lambda t: t.reshape(batch, seq, t.shape[-1])
        lf = lf_t[:FOX_HEADS].reshape(FOX_HEADS, batch, seq).transpose(1, 0, 2)
        pieces = _neg_cumsum_pieces(lf)
        nb, kn = _score_bounds(pieces, kn2, batch, seq)
        fox = _fox_attention(nb, kn, fqt, seq3(fk), _bias_operand(pieces, batch, seq), fvt, batch, seq)
        ret = _retention(seq3(rq), seq3(rk), seq3(rv), seq3(rg), dm, xi, zeta, cd, ret_gn_g[layer])
        wo = w_out[layer].astype(BF16)
        wo_fox = wo[:FOX_WIDTH].reshape(FOX_HEADS, FOX_HEAD_DIM, D_MODEL)[order].reshape(FOX_WIDTH, D_MODEL)
        vec = lambda p: p[layer].reshape(1, D_MODEL)
        x2 = _tail(alpha, fox.reshape(batch * seq, FOX_WIDTH), ret.reshape(batch * seq, RET_WIDTH), x2,
                   wo_fox, wo[FOX_WIDTH:], w_ff1[layer].astype(BF16), w_ff2[layer].astype(BF16),
                   vec(ln1_g), vec(ln1_b), vec(ln2_g), vec(ln2_b))
    return x2.reshape(batch, seq, D_MODEL)
```
